```python
import numpy as np
import jax
import jax.numpy as jnp
from jax import lax

D_MODEL = 2048
BATCH = 8
SEQ = 4096
DEPTH = 2

CTX_LEN = 256
GRID_W = 64
HEAD_DIM = 128
MLSTM_HEADS = 4
MLSTM_DQK = 128
MLSTM_DV = 256
MLSTM_CHUNK = 128
GQA_HEADS = 8
GQA_KV_HEADS = 2
NAT_HEADS = 8
NAT_WIN_ROWS = 8
NAT_WIN_COLS = 16
BRANCH_WIDTH = 1024
N_BRANCHES = 3
N_EXPERTS = 16
EC_CAPACITY_FACTOR = 2
D_EXPERT = 2048
Q_BLOCK = 128
ROPE_THETA = 10000.0
NORM_EPS = 1e-6
NEG_INF = -1e30
N_MOD = 6
SPLITS = (MLSTM_HEADS * MLSTM_DQK, MLSTM_HEADS * MLSTM_DQK, MLSTM_HEADS * MLSTM_DV, MLSTM_HEADS * MLSTM_DV,
          4 * MLSTM_HEADS,
          GQA_HEADS * HEAD_DIM, GQA_KV_HEADS * HEAD_DIM, GQA_KV_HEADS * HEAD_DIM,
          NAT_HEADS * HEAD_DIM, NAT_HEADS * HEAD_DIM, NAT_HEADS * HEAD_DIM,
          N_BRANCHES * D_MODEL)
D_IN = sum(SPLITS)

kernel_name = 'hybrid_mlstm_gqa_natten_ec_moe_diffusion'

F32 = jnp.float32


def rmsnorm(x, g):
    xf = x.astype(F32)
    y = xf * lax.rsqrt(jnp.mean(xf * xf, axis=-1, keepdims=True) + NORM_EPS)
    return (y * g.astype(F32)).astype(x.dtype)


def head_rms(x):
    xf = x.astype(F32)
    return xf * lax.rsqrt(jnp.mean(xf * xf, axis=-1, keepdims=True) + NORM_EPS)


def axial_rope(x, row, col):
    B, T, H, d = x.shape
    nf = d // 4
    inv = ROPE_THETA ** (-jnp.arange(nf, dtype=F32) / nf)
    ang = jnp.stack([row[:, None] * inv, col[:, None] * inv], axis=1)
    cos = jnp.cos(ang)[:, None].astype(x.dtype)
    sin = jnp.sin(ang)[:, None].astype(x.dtype)
    xr = x.reshape(B, T, H, 2, 2, nf)
    x1, x2 = xr[..., 0, :], xr[..., 1, :]
    out = jnp.stack([x1 * cos - x2 * sin, x1 * sin + x2 * cos], axis=-2)
    return out.reshape(B, T, H, d)


def to_heads(a, n):
    B, T, _ = a.shape
    return a.reshape(B, T, n, -1).transpose(0, 2, 1, 3).astype(F32)


def mlstm_gate_logs(pre, bias):
    B, T, _ = pre.shape
    g = (pre.astype(F32) + bias.astype(F32)).reshape(B, T, 2, 2, MLSTM_HEADS)
    g = jnp.transpose(g, (2, 3, 0, 4, 1))
    return g[:, 0], jax.nn.log_sigmoid(g[:, 1])


def mlstm_zero_state(B):
    return (jnp.zeros((B, MLSTM_HEADS, MLSTM_DV, MLSTM_DQK), F32),
            jnp.zeros((B, MLSTM_HEADS, MLSTM_DQK), F32),
            jnp.full((B, MLSTM_HEADS), NEG_INF, F32))


def mlstm_scan(k, v, logi, logf, state0, q=None):
    B, H, T, dk = k.shape
    dv = v.shape[-1]
    nc = T // MLSTM_CHUNK

    def chunks(a):
        a = a.reshape(B, H, nc, MLSTM_CHUNK, *a.shape[3:])
        return jnp.moveaxis(a, 2, 0)

    lower = jnp.tril(jnp.ones((MLSTM_CHUNK, MLSTM_CHUNK), dtype=bool))

    def step(carry, xs):
        C, n, m = carry
        kc, vc, li, lf = xs[:4]
        b = jnp.cumsum(lf, axis=-1)
        bL = b[..., -1]
        g = bL[..., None] - b + li
        m_new = jnp.maximum(bL + m, g.max(-1))
        decay = jnp.exp(bL + m - m_new)
        wk = jnp.exp(g - m_new[..., None])
        C_new = decay[..., None, None] * C + jnp.einsum('bhs,bhsv,bhsd->bhvd', wk, vc, kc)
        n_new = decay[..., None] * n + jnp.einsum('bhs,bhsd->bhd', wk, kc)
        if len(xs) == 4:
            return (C_new, n_new, m_new), None
        qc = xs[4]
        dmat = jnp.where(lower, b[..., :, None] - b[..., None, :] + li[..., None, :], -jnp.inf)
        inter = b + m[..., None]
        m_t = jnp.maximum(inter, dmat.max(-1))
        w = jnp.exp(dmat - m_t[..., None])
        a = jnp.exp(inter - m_t)
        s = jnp.einsum('bhtd,bhsd->bhts', qc, kc) * w
        num = jnp.einsum('bhts,bhsv->bhtv', s, vc) + a[..., None] * jnp.einsum('bhvd,bhtd->bhtv', C, qc)
        den = s.sum(-1) + a * jnp.einsum('bhd,bhtd->bht', n, qc)
        h = num / jnp.maximum(jnp.abs(den), jnp.exp(-m_t))[..., None]
        return (C_new, n_new, m_new), h

    xs = (chunks(k), chunks(v), chunks(logi), chunks(logf))
    if q is not None:
        xs = xs + (chunks(q),)
    state, h = lax.scan(step, state0, xs)
    if q is None:
        return None, state
    h = jnp.moveaxis(h, 0, 2).reshape(B, H, T, dv)
    return h, state


def mlstm_bidir(q, k, v, logi, logf, states0):
    h_f, st_f = mlstm_scan(k, v, logi[0], logf[0], states0[0], q)
    qb = None if q is None else jnp.flip(q, 2)
    h_b, st_b = mlstm_scan(jnp.flip(k, 2), jnp.flip(v, 2), jnp.flip(logi[1], -1), jnp.flip(logf[1], -1),
                           states0[1], qb)
    h = None if q is None else h_f + jnp.flip(h_b, 2)
    return h, (st_f, st_b)


def mlstm_out(h, o):
    B, H, T, dv = h.shape
    hn = head_rms(h).transpose(0, 2, 1, 3).reshape(B, T, H * dv)
    return (jax.nn.sigmoid(o.astype(F32)) * hn).astype(o.dtype)


def block_attention(q, k, v):
    B, T, G, R, d = q.shape
    nb = T // Q_BLOCK
    qb = jnp.moveaxis(q.reshape(B, nb, Q_BLOCK, G, R, d), 1, 0)

    def one(qblk):
        s = jnp.einsum('bqgrd,bkgd->bgrqk', qblk, k).astype(F32)
        p = jax.nn.softmax(s, axis=-1).astype(v.dtype)
        return jnp.einsum('bgrqk,bkgd->bqgrd', p, v)

    o = lax.map(one, qb)
    return jnp.moveaxis(o, 0, 1).reshape(B, T, G * R * d)


def neighborhood_attention(q, k, v, k_ctx, v_ctx, rel_bias):
    B, T, H, d = q.shape
    rows = T // GRID_W
    kh = min(NAT_WIN_ROWS, rows)
    nkr = min(kh + 1, rows)
    qbr = Q_BLOCK // GRID_W
    nbr = rows // qbr
    nk = nkr * GRID_W
    k_grid = k.reshape(B, rows, GRID_W, H, d)
    v_grid = v.reshape(B, rows, GRID_W, H, d)
    qb = jnp.moveaxis(q.reshape(B, nbr, Q_BLOCK, H, d), 1, 0)
    qr_off = jnp.arange(Q_BLOCK) // GRID_W
    qcol = jnp.arange(Q_BLOCK) % GRID_W
    kr_off = jnp.arange(nk) // GRID_W
    kcol = jnp.arange(nk) % GRID_W
    start_c = jnp.clip(qcol - NAT_WIN_COLS // 2, 0, GRID_W - NAT_WIN_COLS)
    col_in = (kcol[None] >= start_c[:, None]) & (kcol[None] < start_c[:, None] + NAT_WIN_COLS)
    dc_idx = jnp.clip(kcol[None] - qcol[:, None] + NAT_WIN_COLS - 1, 0, 2 * NAT_WIN_COLS - 2)

    def one(args):
        qblk, j = args
        qrow = j * qbr + qr_off
        start_r = jnp.clip(qrow - kh // 2, 0, rows - kh)
        kb = jnp.minimum(start_r[0], rows - nkr)
        krow = kb + kr_off
        in_win = col_in & (krow[None] >= start_r[:, None]) & (krow[None] < start_r[:, None] + kh)
        dr_idx = jnp.clip(krow[None] - qrow[:, None] + NAT_WIN_ROWS - 1, 0, 2 * NAT_WIN_ROWS - 2)
        bias = rel_bias[:, dr_idx, dc_idx].astype(F32)
        kblk = lax.dynamic_slice_in_dim(k_grid, kb, nkr, axis=1).reshape(B, nk, H, d)
        vblk = lax.dynamic_slice_in_dim(v_grid, kb, nkr, axis=1).reshape(B, nk, H, d)
        s_loc = jnp.einsum('bqhd,bkhd->bhqk', qblk, kblk).astype(F32) + bias
        s_loc = jnp.where(in_win, s_loc, NEG_INF)
        s_ctx = jnp.einsum('bqhd,bkhd->bhqk', qblk, k_ctx).astype(F32)
        p = jax.nn.softmax(jnp.concatenate([s_loc, s_ctx], axis=-1), axis=-1).astype(v.dtype)
        return (jnp.einsum('bhqk,bkhd->bqhd', p[..., :nk], vblk)
                + jnp.einsum('bhqk,bkhd->bqhd', p[..., nk:], v_ctx))

    o = lax.map(one, (qb, jnp.arange(nbr)))
    return jnp.moveaxis(o, 0, 1).reshape(B, T, H * d)


def merge_branches(ys, gate_pre, w_branch, w_out):
    B, T, _ = gate_pre.shape
    gates = jax.nn.sigmoid(gate_pre.astype(F32)).astype(gate_pre.dtype).reshape(B, T, N_BRANCHES, D_MODEL)
    merged = gates[:, :, 0] * (ys[0] @ w_branch[0])
    for i in range(1, N_BRANCHES):
        merged = merged + gates[:, :, i] * (ys[i] @ w_branch[i])
    return merged @ w_out


def token_mixer(hx, hc, w_in, gate_bias, q_norm, k_norm, rel_bias, w_branch, w_out, with_ctx_out):
    B, T, _ = hx.shape
    Lc = hc.shape[1]
    idx = np.cumsum(SPLITS)[:-1].tolist()
    (mq, mk, mv, mo, mg, gq, gk, gv, nq, nk, nv, mgate) = jnp.split(hx @ w_in, idx, axis=-1)
    (cmq, cmk, cmv, cmo, cmg, cgq, cgk, cgv, cnq, cnk, cnv, cmgate) = jnp.split(hc @ w_in, idx, axis=-1)
    t = jnp.arange(T)
    row = (t // GRID_W).astype(F32)
    col = (t % GRID_W).astype(F32)

    kscale = MLSTM_DQK ** -0.5
    li_c, lf_c = mlstm_gate_logs(cmg, gate_bias)
    hm_c, ctx_states = mlstm_bidir(to_heads(cmq, MLSTM_HEADS) if with_ctx_out else None,
                                   to_heads(cmk, MLSTM_HEADS) * kscale, to_heads(cmv, MLSTM_HEADS),
                                   li_c, lf_c, (mlstm_zero_state(B), mlstm_zero_state(B)))
    li_l, lf_l = mlstm_gate_logs(mg, gate_bias)
    hm_l, _ = mlstm_bidir(to_heads(mq, MLSTM_HEADS), to_heads(mk, MLSTM_HEADS) * kscale,
                          to_heads(mv, MLSTM_HEADS), li_l, lf_l, ctx_states)
    y_m = mlstm_out(hm_l, mo)

    rep = GQA_HEADS // GQA_KV_HEADS
    scale = HEAD_DIM ** -0.5
    q_l = axial_rope(rmsnorm(gq.reshape(B, T, GQA_HEADS, HEAD_DIM), q_norm), row, col) * scale
    k_l = axial_rope(rmsnorm(gk.reshape(B, T, GQA_KV_HEADS, HEAD_DIM), k_norm), row, col)
    v_l = gv.reshape(B, T, GQA_KV_HEADS, HEAD_DIM)
    k_c = rmsnorm(cgk.reshape(B, Lc, GQA_KV_HEADS, HEAD_DIM), k_norm)
    v_c = cgv.reshape(B, Lc, GQA_KV_HEADS, HEAD_DIM)
    y_g = block_attention(q_l.reshape(B, T, GQA_KV_HEADS, rep, HEAD_DIM),
                          jnp.concatenate([k_c, k_l], axis=1), jnp.concatenate([v_c, v_l], axis=1))

    nk_c = cnk.reshape(B, Lc, NAT_HEADS, HEAD_DIM)
    nv_c = cnv.reshape(B, Lc, NAT_HEADS, HEAD_DIM)
    y_n = neighborhood_attention(nq.reshape(B, T, NAT_HEADS, HEAD_DIM) * scale,
                                 nk.reshape(B, T, NAT_HEADS, HEAD_DIM), nv.reshape(B, T, NAT_HEADS, HEAD_DIM),
                                 nk_c, nv_c, rel_bias)

    y_x = merge_branches((y_m, y_g, y_n), mgate, w_branch, w_out)
    if not with_ctx_out:
        return y_x, None

    yc_m = mlstm_out(hm_c, cmo)
    qg_c = rmsnorm(cgq.reshape(B, Lc, GQA_HEADS, HEAD_DIM), q_norm) * scale
    yc_g = block_attention(qg_c.reshape(B, Lc, GQA_KV_HEADS, rep, HEAD_DIM), k_c, v_c)
    qn_c = cnq.reshape(B, Lc, NAT_HEADS, 1, HEAD_DIM) * scale
    yc_n = block_attention(qn_c, nk_c, nv_c)
    y_c = merge_branches((yc_m, yc_g, yc_n), cmgate, w_branch, w_out)
    return y_x, y_c


def expert_choice_moe(h, w_router, w_gate, w_up, w_down):
    B, T, D = h.shape
    cap = EC_CAPACITY_FACTOR * T // N_EXPERTS
    aff = jax.nn.softmax((h @ w_router).astype(F32), axis=-1)
    g, idx = lax.top_k(jnp.transpose(aff, (0, 2, 1)), cap)
    xs = jax.vmap(lambda hb, ib: hb[ib])(h, idx)
    hid = jax.nn.silu(jnp.einsum('becd,edf->becf', xs, w_gate)) * jnp.einsum('becd,edf->becf', xs, w_up)
    ye = jnp.einsum('becf,efd->becd', hid, w_down) * g[..., None].astype(h.dtype)
    return jax.vmap(lambda ib, yb: jnp.zeros((T, D), h.dtype).at[ib.reshape(-1)].add(yb.reshape(-1, D)))(idx, ye)


def setup_inputs(seed: int = 0) -> dict:
    key = jax.random.key(seed)
    ks = jax.random.split(key, 24)
    nrm = lambda k, shape: jax.random.normal(k, shape, dtype=F32)
    D = D_MODEL
    i_bias = -1.0 + 0.1 * nrm(ks[10], (DEPTH, 2, 1, MLSTM_HEADS))
    f_bias = jnp.linspace(3.0, 6.0, MLSTM_HEADS, dtype=F32) + 0.1 * nrm(ks[11], (DEPTH, 2, 1, MLSTM_HEADS))
    return {
        'x': nrm(ks[0], (BATCH, SEQ, D)),
        'c': nrm(ks[1], (BATCH, D)),
        'ctx': nrm(ks[2], (BATCH, CTX_LEN, D)),
        'c_ctx': nrm(ks[3], (D,)),
        'w_mod': nrm(ks[4], (DEPTH, D, N_MOD * D)) * (0.5 * D ** -0.5),
        'b_mod': 0.02 * nrm(ks[5], (DEPTH, N_MOD * D)),
        'g_pre_mix': 1.0 + 0.02 * nrm(ks[6], (DEPTH, D)),
        'g_post_mix': 1.0 + 0.02 * nrm(ks[7], (DEPTH, D)),
        'g_pre_ffn': 1.0 + 0.02 * nrm(ks[8], (DEPTH, D)),
        'g_post_ffn': 1.0 + 0.02 * nrm(ks[9], (DEPTH, D)),
        'w_in': nrm(ks[12], (DEPTH, D, D_IN)) * D ** -0.5,
        'mlstm_gate_bias': jnp.concatenate([i_bias, f_bias], axis=2).reshape(DEPTH, 4 * MLSTM_HEADS),
        'gqa_q_norm': 1.0 + 0.02 * nrm(ks[13], (DEPTH, HEAD_DIM)),
        'gqa_k_norm': 1.0 + 0.02 * nrm(ks[14], (DEPTH, HEAD_DIM)),
        'nat_rel_bias': 0.1 * nrm(ks[15], (DEPTH, NAT_HEADS, 2 * NAT_WIN_ROWS - 1, 2 * NAT_WIN_COLS - 1)),
        'w_branch': nrm(ks[16], (DEPTH, N_BRANCHES, BRANCH_WIDTH, D)) * BRANCH_WIDTH ** -0.5,
        'w_out': nrm(ks[17], (DEPTH, D, D)) * D ** -0.5,
        'w_router': nrm(ks[18], (DEPTH, D, N_EXPERTS)) * D ** -0.5,
        'w_expert_gate': nrm(ks[19], (DEPTH, N_EXPERTS, D, D_EXPERT)) * D ** -0.5,
        'w_expert_up': nrm(ks[20], (DEPTH, N_EXPERTS, D, D_EXPERT)) * D ** -0.5,
        'w_expert_down': nrm(ks[21], (DEPTH, N_EXPERTS, D_EXPERT, D)) * D_EXPERT ** -0.5,
    }


def reference(x, c, ctx, c_ctx, w_mod, b_mod, g_pre_mix, g_post_mix, g_pre_ffn, g_post_ffn, w_in,
              mlstm_gate_bias, gqa_q_norm, gqa_k_norm, nat_rel_bias, w_branch, w_out, w_router,
              w_expert_gate, w_expert_up, w_expert_down):
    for l in range(DEPTH):
        last = l == DEPTH - 1
        mod_x = jax.nn.silu(c) @ w_mod[l] + b_mod[l]
        mod_c = jax.nn.silu(c_ctx) @ w_mod[l] + b_mod[l]
        sh1, sc1, g1, sh2, sc2, g2 = [m[:, None, :] for m in jnp.split(mod_x, N_MOD, axis=-1)]
        csh1, csc1, cg1, csh2, csc2, cg2 = jnp.split(mod_c, N_MOD, axis=-1)

        hx = rmsnorm(x, g_pre_mix[l]) * (1.0 + sc1) + sh1
        hc = rmsnorm(ctx, g_pre_mix[l]) * (1.0 + csc1) + csh1
        y_x, y_c = token_mixer(hx, hc, w_in[l], mlstm_gate_bias[l], gqa_q_norm[l], gqa_k_norm[l],
                               nat_rel_bias[l], w_branch[l], w_out[l], not last)
        x = x + g1 * rmsnorm(y_x, g_post_mix[l])

        hx = rmsnorm(x, g_pre_ffn[l]) * (1.0 + sc2) + sh2
        x = x + g2 * rmsnorm(expert_choice_moe(hx, w_router[l], w_expert_gate[l], w_expert_up[l],
                                               w_expert_down[l]), g_post_ffn[l])
        if not last:
            ctx = ctx + cg1 * rmsnorm(y_c, g_post_mix[l])
            hc = rmsnorm(ctx, g_pre_ffn[l]) * (1.0 + csc2) + csh2
            ctx = ctx + cg2 * rmsnorm(expert_choice_moe(hc, w_router[l], w_expert_gate[l], w_expert_up[l],
                                                        w_expert_down[l]), g_post_ffn[l])
    return x
```

```python
import functools

import numpy as np
import jax
import jax.numpy as jnp
from jax import lax
from jax.experimental import pallas as pl
from jax.experimental.pallas import tpu as pltpu

F32 = jnp.float32
BF16 = jnp.bfloat16

LANES = 128
HEAD_DIM = 128
MLSTM_HEADS = 4
MLSTM_DQK = 128
MLSTM_DV = 256
MLSTM_CHUNK = 128
GQA_HEADS = 8
GQA_KV_HEADS = 2
NAT_HEADS = 8
NAT_WIN_ROWS = 8
NAT_WIN_COLS = 16
GRID_W = 64
Q_BLOCK = 128
N_EXPERTS = 16
EC_CAPACITY_FACTOR = 2
ROPE_THETA = 10000.0
NORM_EPS = 1e-6
NEG_INF = -1e30
N_MOD = 6
N_BRANCHES = 3
BRANCH_WIDTH = 1024
TR = 256
N_GATE_COLS = 4 * MLSTM_HEADS

C_MQ = 0
C_MK = C_MQ + MLSTM_HEADS * MLSTM_DQK
C_MV = C_MK + MLSTM_HEADS * MLSTM_DQK
C_MO = C_MV + MLSTM_HEADS * MLSTM_DV
C_GQ = C_MO + MLSTM_HEADS * MLSTM_DV
C_GK = C_GQ + GQA_HEADS * HEAD_DIM
C_GV = C_GK + GQA_KV_HEADS * HEAD_DIM
C_NQ = C_GV + GQA_KV_HEADS * HEAD_DIM
C_NK = C_NQ + NAT_HEADS * HEAD_DIM
C_NV = C_NK + NAT_HEADS * HEAD_DIM
C_GATE = C_NV + NAT_HEADS * HEAD_DIM
W_IN_GATE_COL = C_GQ


def _params(sem, vmem_mb=None):
    return pltpu.CompilerParams(
        dimension_semantics=sem,
        vmem_limit_bytes=None if vmem_mb is None else vmem_mb * 2 ** 20)


def _pick(n, cands):
    for c in cands:
        if n % c == 0:
            return c
    raise ValueError(f"no tile for {n} in {cands}")


def _dot(a, b):
    return jnp.dot(a, b, preferred_element_type=F32)


def _dot_nt(a, b):
    return lax.dot_general(a, b, (((1,), (1,)), ((), ())), preferred_element_type=F32)


def _sigmoid(x):
    return 1.0 / (1.0 + jnp.exp(-x))


def _rms(x):
    return x * lax.rsqrt(jnp.mean(x * x, axis=-1, keepdims=True) + NORM_EPS)


def _mod_kernel(c_ref, w_ref, b_ref, o_ref):
    c = c_ref[...]
    a = (c * _sigmoid(c)).astype(BF16)
    o_ref[0] = _dot(a, w_ref[0].astype(BF16)) + b_ref[0]


def _modulation(cc, w_mod, b_mod):
    L, D, N = w_mod.shape
    tn = _pick(N, (768, 512, 256, 128))
    return pl.pallas_call(
        _mod_kernel,
        grid=(L, N // tn),
        in_specs=[pl.BlockSpec((cc.shape[0], D), lambda l, n: (0, 0)),
                  pl.BlockSpec((1, D, tn), lambda l, n: (l, 0, n)),
                  pl.BlockSpec((1, 1, tn), lambda l, n: (l, 0, n))],
        out_specs=pl.BlockSpec((1, cc.shape[0], tn), lambda l, n: (l, 0, n)),
        out_shape=jax.ShapeDtypeStruct((L, cc.shape[0], N), F32),
        compiler_params=_params(("parallel", "parallel")),
    )(cc, w_mod, b_mod.reshape(L, 1, N))


def _norm_mod_kernel(x_ref, g_ref, sh_ref, sc_ref, o_ref):
    y = _rms(x_ref[0]) * g_ref[...]
    o_ref[0] = (y * (1.0 + sc_ref[0]) + sh_ref[0]).astype(BF16)


def _norm_mod_router_kernel(x_ref, g_ref, sh_ref, sc_ref, whi_ref, wlo_ref, o_ref, acol_ref, arow_ref):
    y = _rms(x_ref[0]) * g_ref[...]
    h = y * (1.0 + sc_ref[0]) + sh_ref[0]
    hb = h.astype(BF16)
    o_ref[0] = hb
    hlo = (h - hb.astype(F32)).astype(BF16)
    logits = _dot(hb, whi_ref[...]) + _dot(hlo, whi_ref[...]) + _dot(hb, wlo_ref[...])
    lane = lax.broadcasted_iota(jnp.int32, logits.shape, 1)
    valid = lane < N_EXPERTS
    logits = jnp.where(valid, logits, NEG_INF)
    ex = jnp.where(valid, jnp.exp(logits - jnp.max(logits, axis=1, keepdims=True)), 0.0)
    aff = ex / jnp.sum(ex, axis=1, keepdims=True)
    acol_ref[0] = aff
    arow_ref[0] = aff.T[:N_EXPERTS]


def _mod_specs(B, nT, D, shift_chunk, scale_chunk):
    row = lambda b, i: jnp.where(i == nT - 1, B, b)
    return [pl.BlockSpec((1, 1, D), lambda b, i: (row(b, i), 0, shift_chunk)),
            pl.BlockSpec((1, 1, D), lambda b, i: (row(b, i), 0, scale_chunk))]


def _norm_mod(xc, g, mod3, shift_chunk, scale_chunk, router=None):
    B, Tt, D = xc.shape
    nT = Tt // TR
    tile = pl.BlockSpec((1, TR, D), lambda b, i: (b, i, 0))
    in_specs = [tile, pl.BlockSpec((1, D), lambda b, i: (0, 0))] + _mod_specs(B, nT, D, shift_chunk, scale_chunk)
    args = [xc, g.reshape(1, D), mod3, mod3]
    if router is None:
        return pl.pallas_call(
            _norm_mod_kernel, grid=(B, nT), in_specs=in_specs, out_specs=tile,
            out_shape=jax.ShapeDtypeStruct((B, Tt, D), BF16),
            compiler_params=_params(("parallel", "parallel")),
        )(*args)
    whi, wlo = router
    wspec = pl.BlockSpec((D, LANES), lambda b, i: (0, 0))
    return pl.pallas_call(
        _norm_mod_router_kernel, grid=(B, nT), in_specs=in_specs + [wspec, wspec],
        out_specs=[tile, pl.BlockSpec((1, TR, LANES), lambda b, i: (b, i, 0)),
                   pl.BlockSpec((1, N_EXPERTS, TR), lambda b, i: (b, 0, i))],
        out_shape=[jax.ShapeDtypeStruct((B, Tt, D), BF16),
                   jax.ShapeDtypeStruct((B, Tt, LANES), F32),
                   jax.ShapeDtypeStruct((B, N_EXPERTS, Tt), F32)],
        compiler_params=_params(("parallel", "parallel")),
    )(*args, whi, wlo)


def _mm_kernel(a_ref, w_ref, o_ref):
    o_ref[...] = _dot(a_ref[...], w_ref[...]).astype(o_ref.dtype)


def _matmul(a, w, out_dtype):
    M, K = a.shape
    N = w.shape[1]
    tm = _pick(M, (1024, 512, 256))
    tn = _pick(N, (512, 256, 128))
    return pl.pallas_call(
        _mm_kernel, grid=(M // tm, N // tn),
        in_specs=[pl.BlockSpec((tm, K), lambda i, j: (i, 0)), pl.BlockSpec((K, tn), lambda i, j: (0, j))],
        out_specs=pl.BlockSpec((tm, tn), lambda i, j: (i, j)),
        out_shape=jax.ShapeDtypeStruct((M, N), out_dtype),
        compiler_params=_params(("parallel", "parallel")),
    )(a, w)


def _qk_prep_kernel(p_ref, nw_ref, cos_ref, sin_ref, o_ref):
    y = _rms(p_ref[0].astype(F32)) * nw_ref[0]
    lane = lax.broadcasted_iota(jnp.int32, y.shape, 1)
    first = (lane % (HEAD_DIM // 2)) < HEAD_DIM // 4
    partner = jnp.where(first, pltpu.roll(y, HEAD_DIM - HEAD_DIM // 4, 1), pltpu.roll(y, HEAD_DIM // 4, 1))
    o_ref[0] = (y * cos_ref[...] + partner * sin_ref[...]).astype(BF16)


def _qk_prep(P, nw, cos_t, sin_t):
    B, Tt, _ = P.shape
    nh = GQA_HEADS + GQA_KV_HEADS
    c0 = C_GQ // HEAD_DIM
    return pl.pallas_call(
        _qk_prep_kernel, grid=(B, Tt // TR, nh),
        in_specs=[pl.BlockSpec((1, TR, HEAD_DIM), lambda b, i, h: (b, i, c0 + h)),
                  pl.BlockSpec((1, 1, HEAD_DIM), lambda b, i, h: (h, 0, 0)),
                  pl.BlockSpec((TR, HEAD_DIM), lambda b, i, h: (i, 0)),
                  pl.BlockSpec((TR, HEAD_DIM), lambda b, i, h: (i, 0))],
        out_specs=pl.BlockSpec((1, TR, HEAD_DIM), lambda b, i, h: (b, i, h)),
        out_shape=jax.ShapeDtypeStruct((B, Tt, nh * HEAD_DIM), BF16),
        compiler_params=_params(("parallel", "parallel", "parallel")),
    )(P, nw, cos_t, sin_t)


def _flash_kernel(q_ref, k_ref, v_ref, o_ref, m_ref, l_ref, acc_ref, *, R, T, Lc, ck, scale, tile0, n_lat_tiles):
    tile = pl.program_id(2) + tile0
    q = jnp.concatenate([q_ref[0, :, r * HEAD_DIM:(r + 1) * HEAD_DIM] for r in range(R)], axis=0)

    m_ref[...] = jnp.full(m_ref.shape, NEG_INF, F32)
    l_ref[...] = jnp.zeros(l_ref.shape, F32)
    acc_ref[...] = jnp.zeros(acc_ref.shape, F32)

    def step(k, v):
        s = _dot_nt(q, k)
        if scale != 1.0:
            s = s * scale
        m_old = m_ref[...]
        m_new = jnp.maximum(m_old, jnp.max(s, axis=1, keepdims=True))
        p = jnp.exp(s - m_new)
        alpha = jnp.exp(m_old - m_new)
        l_ref[...] = alpha * l_ref[...] + jnp.sum(p, axis=1, keepdims=True)
        acc_ref[...] = alpha * acc_ref[...] + _dot(p.astype(BF16), v)
        m_ref[...] = m_new

    kv_rows = k_ref.shape[1]
    ctx0 = kv_rows - Lc
    step(k_ref[0, ctx0:, :], v_ref[0, ctx0:, :])
    if kv_rows > Lc:
        def body(j, carry):
            r0 = pl.multiple_of(j * ck, ck)
            step(k_ref[0, pl.ds(r0, ck), :], v_ref[0, pl.ds(r0, ck), :])
            return carry
        lax.fori_loop(0, jnp.where(tile < n_lat_tiles, T // ck, 0), body, 0)

    o = acc_ref[...] / l_ref[...]
    for r in range(R):
        o_ref[0, :, r * HEAD_DIM:(r + 1) * HEAD_DIM] = o[r * TR:(r + 1) * TR].astype(o_ref.dtype)


def _flash(q_arr, q_col, k_arr, k_col, v_arr, v_col, *, G, R, T, Lc, tile0, nq, kv_all, scale, out=None):
    B, Tt, _ = q_arr.shape
    qw = R * HEAD_DIM
    kv_rows = Tt if kv_all else Lc
    kv_blk = 0 if kv_all else T // Lc
    ck = _pick(T, (512, 256))
    kern = functools.partial(_flash_kernel, R=R, T=T, Lc=Lc, ck=ck, scale=scale, tile0=tile0,
                             n_lat_tiles=T // TR)
    in_specs = [pl.BlockSpec((1, TR, qw), lambda b, g, i: (b, i + tile0, q_col // qw + g)),
                pl.BlockSpec((1, kv_rows, HEAD_DIM), lambda b, g, i: (b, kv_blk, k_col // HEAD_DIM + g)),
                pl.BlockSpec((1, kv_rows, HEAD_DIM), lambda b, g, i: (b, kv_blk, v_col // HEAD_DIM + g))]
    args = [q_arr, k_arr, v_arr]
    aliases = {}
    if out is not None:
        in_specs.append(pl.BlockSpec(memory_space=pl.ANY))
        args.append(out)
        aliases = {3: 0}
        kern_fn = lambda q, k, v, _o, o, m, l, a: kern(q, k, v, o, m, l, a)
    else:
        kern_fn = kern
    return pl.pallas_call(
        kern_fn, grid=(B, G, nq), in_specs=in_specs,
        out_specs=pl.BlockSpec((1, TR, qw), lambda b, g, i: (b, i + tile0, g)),
        out_shape=jax.ShapeDtypeStruct((B, Tt, G * qw), BF16),
        scratch_shapes=[pltpu.VMEM((R * TR, 1), F32), pltpu.VMEM((R * TR, 1), F32),
                        pltpu.VMEM((R * TR, HEAD_DIM), F32)],
        input_output_aliases=aliases,
        compiler_params=_params(("parallel", "parallel", "arbitrary"), 48),
    )(*args)


def _nat_geometry(T):
    rows = T // GRID_W
    kh = min(NAT_WIN_ROWS, rows)
    nkr = min(kh + 1, rows)
    qbr = Q_BLOCK // GRID_W
    nbr = rows // qbr
    kb = np.zeros(nbr, np.int32)
    var = np.zeros(nbr, np.int32)
    variants = []
    for j in range(nbr):
        qrow = j * qbr + np.arange(qbr)
        start_r = np.clip(qrow - kh // 2, 0, rows - kh)
        kb[j] = min(start_r[0], rows - nkr)
        sig = (int(kb[j] - j * qbr), tuple(int(s - kb[j]) for s in start_r))
        if sig not in variants:
            variants.append(sig)
        var[j] = variants.index(sig)
    return kh, nkr, qbr, nbr, kb, var, variants


def _nat_bias_kernel(rb_ref, o_ref, *, variants, kh, nkr, qbr):
    h = pl.program_id(0)
    ndr = 2 * NAT_WIN_ROWS - 1
    ndc = 2 * NAT_WIN_COLS - 1
    qc = lax.broadcasted_iota(jnp.int32, (GRID_W, GRID_W), 0)
    kc = lax.broadcasted_iota(jnp.int32, (GRID_W, GRID_W), 1)
    dc = kc - qc + NAT_WIN_COLS - 1
    start_c = jnp.clip(qc - NAT_WIN_COLS // 2, 0, GRID_W - NAT_WIN_COLS)
    col_in = (kc >= start_c) & (kc < start_c + NAT_WIN_COLS)
    masked = jnp.full((GRID_W, GRID_W), NEG_INF, F32)
    toeplitz = []
    for dr in range(ndr):
        t = jnp.zeros((GRID_W, GRID_W), F32)
        for d in range(ndc):
            t = jnp.where(dc == d, rb_ref[(h * ndr + dr) * ndc + d], t)
        toeplitz.append(jnp.where(col_in, t, NEG_INF))
    for v, (delta, srel) in enumerate(variants):
        for qr in range(qbr):
            blocks = []
            for kr in range(nkr):
                if srel[qr] <= kr < srel[qr] + kh:
                    blocks.append(toeplitz[int(np.clip(kr + delta - qr + NAT_WIN_ROWS - 1, 0, ndr - 1))])
                else:
                    blocks.append(masked)
            rows = slice(qr * GRID_W, (qr + 1) * GRID_W)
            for kr in range(0, nkr - 1, 2):
                o_ref[v, 0, rows, kr * GRID_W:(kr + 2) * GRID_W] = jnp.concatenate(blocks[kr:kr + 2], axis=1)
            if nkr % 2:
                o_ref[v, 0, rows, (nkr - 1) * GRID_W:] = blocks[-1]


def _nat_bias_tables(rel_bias, T):
    kh, nkr, qbr, _, _, _, variants = _nat_geometry(T)
    V = len(variants)
    nk = nkr * GRID_W
    return pl.pallas_call(
        functools.partial(_nat_bias_kernel, variants=variants, kh=kh, nkr=nkr, qbr=qbr),
        grid=(NAT_HEADS,),
        in_specs=[pl.BlockSpec(memory_space=pltpu.SMEM)],
        out_specs=pl.BlockSpec((V, 1, Q_BLOCK, nk), lambda h: (0, h, 0, 0)),
        out_shape=jax.ShapeDtypeStruct((V, NAT_HEADS, Q_BLOCK, nk), F32),
        compiler_params=_params(("parallel",)),
    )(rel_bias.reshape(-1))


def _nat_kernel(kb_ref, var_ref, q_ref, k_ref, v_ref, bias_ref, o_ref, *, T, Lc, nk, scale):
    j = pl.program_id(2)
    q = q_ref[0]
    r0 = pl.multiple_of(kb_ref[j] * GRID_W, GRID_W)
    s1 = _dot_nt(q, k_ref[0, pl.ds(r0, nk), :]) * scale + bias_ref[0, 0]
    s2 = _dot_nt(q, k_ref[0, T:, :]) * scale
    m = jnp.maximum(jnp.max(s1, axis=1, keepdims=True), jnp.max(s2, axis=1, keepdims=True))
    p1 = jnp.exp(s1 - m)
    p2 = jnp.exp(s2 - m)
    l = jnp.sum(p1, axis=1, keepdims=True) + jnp.sum(p2, axis=1, keepdims=True)
    o = _dot(p1.astype(BF16), v_ref[0, pl.ds(r0, nk), :]) + _dot(p2.astype(BF16), v_ref[0, T:, :])
    o_ref[0] = (o / l).astype(o_ref.dtype)


def _nat(P, bias_tab, T, Lc, scale):
    B, Tt, _ = P.shape
    _, nkr, _, nbr, kb, var, _ = _nat_geometry(T)
    nk = nkr * GRID_W
    grid_spec = pltpu.PrefetchScalarGridSpec(
        num_scalar_prefetch=2, grid=(B, NAT_HEADS, nbr),
        in_specs=[pl.BlockSpec((1, Q_BLOCK, HEAD_DIM), lambda b, h, j, kb_, vr: (b, j, C_NQ // HEAD_DIM + h)),
                  pl.BlockSpec((1, Tt, HEAD_DIM), lambda b, h, j, kb_, vr: (b, 0, C_NK // HEAD_DIM + h)),
                  pl.BlockSpec((1, Tt, HEAD_DIM), lambda b, h, j, kb_, vr: (b, 0, C_NV // HEAD_DIM + h)),
                  pl.BlockSpec((1, 1, Q_BLOCK, nk), lambda b, h, j, kb_, vr: (vr[j], h, 0, 0))],
        out_specs=pl.BlockSpec((1, Q_BLOCK, HEAD_DIM), lambda b, h, j, kb_, vr: (b, j, h)))
    return pl.pallas_call(
        functools.partial(_nat_kernel, T=T, Lc=Lc, nk=nk, scale=scale),
        grid_spec=grid_spec,
        out_shape=jax.ShapeDtypeStruct((B, Tt, NAT_HEADS * HEAD_DIM), BF16),
        compiler_params=_params(("parallel", "parallel", "arbitrary")),
    )(jnp.asarray(kb), jnp.asarray(var), P, P, P, bias_tab)


def _mlstm_kernel(qf_ref, kf_ref, vf_ref, gf_ref, qb_ref, kb_ref, vb_ref, gb_ref, bias_ref,
                  hf_ref, hb_ref, st_ref, m_ref, *, kscale):
    L = MLSTM_CHUNK
    dv = MLSTM_DV

    @pl.when(pl.program_id(1) == 0)
    def _():
        st_ref[...] = jnp.zeros(st_ref.shape, F32)
        m_ref[...] = jnp.full(m_ref.shape, NEG_INF, F32)

    ti = lax.broadcasted_iota(jnp.int32, (L, L), 0)
    si = lax.broadcasted_iota(jnp.int32, (L, L), 1)
    ones_col = (lax.broadcasted_iota(jnp.int32, (L, LANES), 1) == 0).astype(BF16)

    for dirn, (q_ref, k_ref, v_ref, g_ref, h_ref) in enumerate(
            ((qf_ref, kf_ref, vf_ref, gf_ref, hf_ref), (qb_ref, kb_ref, vb_ref, gb_ref, hb_ref))):
        tri = (si <= ti) if dirn == 0 else (si >= ti)
        G = g_ref[0] + bias_ref[...]
        LF = jnp.minimum(G, 0.0) - jnp.log(1.0 + jnp.exp(-jnp.abs(G)))
        lf_hi = LF.astype(BF16)
        lf_lo = (LF - lf_hi.astype(F32)).astype(BF16)
        trib = tri.astype(BF16)
        Bc = _dot(trib, lf_hi) + _dot(trib, lf_lo)
        BcT = Bc.T
        GT = G.T
        bl_row = Bc[L - 1:L, :] if dirn == 0 else Bc[0:1, :]
        for h in range(MLSTM_HEADS):
            ci = dirn * 2 * MLSTM_HEADS + h
            cf = ci + MLSTM_HEADS
            bcol = Bc[:, cf:cf + 1]
            brow = BcT[cf:cf + 1, :]
            licol = G[:, ci:ci + 1]
            lirow = GT[ci:ci + 1, :]
            bl = bl_row[:, cf:cf + 1]
            m_old = m_ref[dirn, h]
            gcol = bl - bcol + licol
            m_new = jnp.maximum(bl + m_old, jnp.max(gcol, axis=0, keepdims=True))
            decay = jnp.exp(bl + m_old - m_new)
            wk = jnp.exp(gcol - m_new)
            dmat = jnp.where(tri, bcol - brow + lirow, NEG_INF)
            inter = bcol + m_old
            m_t = jnp.maximum(inter, jnp.max(dmat, axis=1, keepdims=True))
            w = jnp.exp(dmat - m_t)
            a = jnp.exp(inter - m_t)
            qh = q_ref[0, :, h * MLSTM_DQK:(h + 1) * MLSTM_DQK]
            kh = k_ref[0, :, h * MLSTM_DQK:(h + 1) * MLSTM_DQK]
            vaug = jnp.concatenate([v_ref[0, :, h * dv:(h + 1) * dv], ones_col], axis=1)
            smat = _dot_nt(qh, kh) * kscale * w
            state = st_ref[dirn, h]
            nd = _dot(smat.astype(BF16), vaug) + a * _dot(qh, state.astype(BF16))
            den = jnp.maximum(jnp.abs(nd[:, dv:dv + 1]), jnp.exp(-m_t))
            h_ref[0, :, h * dv:(h + 1) * dv] = nd[:, :dv] / den
            kT = kh.astype(F32).T.astype(BF16)
            upd = _dot(kT, (wk * vaug.astype(F32)).astype(BF16)) * kscale
            st_ref[dirn, h] = decay * state + upd
            m_ref[dirn, h] = m_new


def _mlstm(P, gates, gate_bias, T, Lc):
    B, Tt, _ = P.shape
    L = MLSTM_CHUNK
    ncl, ncc = T // L, Lc // L
    fwd = lambda i: jnp.where(i < ncc, ncl + i, i - ncc)
    bwd = lambda i: jnp.where(i < ncc, ncl + ncc - 1 - i, ncl - 1 - (i - ncc))
    qw = MLSTM_HEADS * MLSTM_DQK
    vw = MLSTM_HEADS * MLSTM_DV

    def specs(order):
        return [pl.BlockSpec((1, L, qw), lambda b, i: (b, order(i), C_MQ // qw)),
                pl.BlockSpec((1, L, qw), lambda b, i: (b, order(i), C_MK // qw)),
                pl.BlockSpec((1, L, vw), lambda b, i: (b, order(i), C_MV // vw)),
                pl.BlockSpec((1, L, LANES), lambda b, i: (b, order(i), 0))]

    out = jax.ShapeDtypeStruct((B, Tt, vw), F32)
    return pl.pallas_call(
        functools.partial(_mlstm_kernel, kscale=MLSTM_DQK ** -0.5),
        grid=(B, ncl + ncc),
        in_specs=specs(fwd) + specs(bwd) + [pl.BlockSpec((1, LANES), lambda b, i: (0, 0))],
        out_specs=[pl.BlockSpec((1, L, vw), lambda b, i: (b, fwd(i), 0)),
                   pl.BlockSpec((1, L, vw), lambda b, i: (b, bwd(i), 0))],
        out_shape=[out, out],
        scratch_shapes=[pltpu.VMEM((2, MLSTM_HEADS, MLSTM_DQK, MLSTM_DV + LANES), F32),
                        pltpu.VMEM((2, MLSTM_HEADS, 1, 1), F32)],
        compiler_params=_params(("parallel", "arbitrary")),
    )(P, P, P, gates, P, P, P, gates, gate_bias)


def _merge_kernel(hf_ref, hb_ref, mo_ref, yg_ref, yn_ref, g0_ref, g1_ref, g2_ref, wb_ref, o_ref, ym_ref):
    @pl.when(pl.program_id(1) == 0)
    def _():
        h = hf_ref[...] + hb_ref[...]
        for hh in range(MLSTM_HEADS):
            cols = slice(hh * MLSTM_DV, (hh + 1) * MLSTM_DV)
            ym_ref[:, cols] = (_sigmoid(mo_ref[:, cols].astype(F32)) * _rms(h[:, cols])).astype(BF16)

    gate = lambda r: _sigmoid(r[...].astype(F32))
    acc = gate(g0_ref) * _dot(ym_ref[...], wb_ref[0])
    acc += gate(g1_ref) * _dot(yg_ref[...], wb_ref[1])
    acc += gate(g2_ref) * _dot(yn_ref[...], wb_ref[2])
    o_ref[...] = acc.astype(BF16)


def _merge(hf, hb, P2, yg, yn, wb):
    M, W = hf.shape
    D = wb.shape[2]
    tm = _pick(M, (512, 256))
    tn = _pick(D, (512, 256))
    row = lambda c: pl.BlockSpec((tm, W), lambda i, n: (i, c))
    gate = lambda br: pl.BlockSpec((tm, tn), lambda i, n: (i, (C_GATE + br * D) // tn + n))
    return pl.pallas_call(
        _merge_kernel, grid=(M // tm, D // tn),
        in_specs=[row(0), row(0), row(C_MO // W), row(0), row(0), gate(0), gate(1), gate(2),
                  pl.BlockSpec((N_BRANCHES, W, tn), lambda i, n: (0, 0, n))],
        out_specs=pl.BlockSpec((tm, tn), lambda i, n: (i, n)),
        out_shape=jax.ShapeDtypeStruct((M, D), BF16),
        scratch_shapes=[pltpu.VMEM((tm, W), BF16)],
        compiler_params=_params(("parallel", "arbitrary")),
    )(hf, hb, P2, yg, yn, P2, P2, P2, wb)


def _wout_post_kernel(a_ref, w_ref, x_ref, gp_ref, gate_ref, o_ref):
    y = _dot(a_ref[0], w_ref[...])
    o_ref[0] = x_ref[0] + gate_ref[0] * (_rms(y) * gp_ref[...])


def _wout_post(merged, w_out, xc, g_post, mod3, gate_chunk):
    B, Tt, D = xc.shape
    nT = Tt // TR
    tile = pl.BlockSpec((1, TR, D), lambda b, i: (b, i, 0))
    row = lambda b, i: jnp.where(i == nT - 1, B, b)
    return pl.pallas_call(
        _wout_post_kernel, grid=(B, nT),
        in_specs=[tile, pl.BlockSpec((D, D), lambda b, i: (0, 0)), tile,
                  pl.BlockSpec((1, D), lambda b, i: (0, 0)),
                  pl.BlockSpec((1, 1, D), lambda b, i: (row(b, i), 0, gate_chunk))],
        out_specs=tile,
        out_shape=jax.ShapeDtypeStruct((B, Tt, D), F32),
        compiler_params=_params(("parallel", "parallel"), 48),
    )(merged, w_out, xc, g_post.reshape(1, D), mod3)


def _route_segment(arow, acol, k, chunk):
    E, T = arow.shape
    nchunk = T // chunk
    ind = lambda m: jnp.where(m, 1.0, 0.0)
    bits = pltpu.bitcast(arow, jnp.int32)
    thr = jnp.zeros((E, 1), jnp.int32)
    for bit in range(30, -1, -1):
        cand = thr | (1 << bit)
        cnt = jnp.sum(ind(bits >= cand), axis=1, keepdims=True)
        thr = jnp.where(cnt >= k, cand, thr)
    thr_val = pltpu.bitcast(thr, F32)
    gt = arow > thr_val
    eq = arow == thr_val
    need = k - jnp.sum(ind(gt), axis=1, keepdims=True)

    r_i = lax.broadcasted_iota(jnp.int32, (chunk, chunk), 0)
    c_i = lax.broadcasted_iota(jnp.int32, (chunk, chunk), 1)
    before = ind(r_i < c_i).astype(BF16)
    after = ind(c_i < r_i).astype(BF16)

    def prefix_rows(mask):
        carry = jnp.zeros((E, 1), F32)
        parts, carries = [], [carry]
        for c in range(nchunk):
            mc = ind(mask[:, c * chunk:(c + 1) * chunk]).astype(BF16)
            parts.append(_dot(mc, before) + carry)
            carry = carry + jnp.sum(mc.astype(F32), axis=1, keepdims=True)
            carries.append(carry)
        return jnp.concatenate(parts, axis=1), carries

    tie_rank, _ = prefix_rows(eq)
    sel = gt | (eq & (tie_rank < need))
    pos, carries = prefix_rows(sel)
    prow = jnp.where(sel, pos, -1.0)
    lane = lax.broadcasted_iota(jnp.int32, (E, LANES), 1)
    starts = jnp.zeros((E, LANES), F32)
    for c, cv in enumerate(carries):
        starts = jnp.where(lane == c, cv, starts)

    sub = lax.broadcasted_iota(jnp.int32, (E, LANES), 0)
    diag = sub == lane
    thr_row = jnp.sum(jnp.where(diag, thr_val, 0.0), axis=0, keepdims=True)
    need_row = jnp.sum(jnp.where(diag, need, 0.0), axis=0, keepdims=True)
    lane_ok = lax.broadcasted_iota(jnp.int32, (1, LANES), 1) < E
    gt_c = (acol > thr_row) & lane_ok
    eq_c = (acol == thr_row) & lane_ok

    def prefix_cols(mask):
        carry = jnp.zeros((1, LANES), F32)
        parts = []
        for c in range(nchunk):
            mc = ind(mask[c * chunk:(c + 1) * chunk]).astype(BF16)
            parts.append(_dot(after, mc) + carry)
            carry = carry + jnp.sum(mc.astype(F32), axis=0, keepdims=True)
        return jnp.concatenate(parts, axis=0)

    sel_c = gt_c | (eq_c & (prefix_cols(eq_c) < need_row))
    pcol = jnp.where(sel_c, prefix_cols(sel_c), -1.0)
    return prow, pcol, starts


def _route_kernel(arow_ref, acol_ref, prow_ref, pcol_ref, starts_ref, *, T, Lc, cap_l, cap_c):
    prow, pcol, starts = _route_segment(arow_ref[0, :, :T], acol_ref[0, :T], cap_l, TR)
    prow_ref[0, :, :T] = prow
    pcol_ref[0, :T] = pcol
    starts_ref[0] = starts.astype(jnp.int32)
    prow, pcol, _ = _route_segment(arow_ref[0, :, T:], acol_ref[0, T:], cap_c, TR)
    prow_ref[0, :, T:] = prow
    pcol_ref[0, T:] = pcol


def _route(arow, acol, T, Lc, cap_l, cap_c):
    B, E, Tt = arow.shape
    return pl.pallas_call(
        functools.partial(_route_kernel, T=T, Lc=Lc, cap_l=cap_l, cap_c=cap_c),
        grid=(B,),
        in_specs=[pl.BlockSpec((1, E, Tt), lambda b: (b, 0, 0)), pl.BlockSpec((1, Tt, LANES), lambda b: (b, 0, 0))],
        out_specs=[pl.BlockSpec((1, E, Tt), lambda b: (b, 0, 0)), pl.BlockSpec((1, Tt, LANES), lambda b: (b, 0, 0)),
                   pl.BlockSpec((1, E, LANES), lambda b: (b, 0, 0))],
        out_shape=[jax.ShapeDtypeStruct((B, E, Tt), F32), jax.ShapeDtypeStruct((B, Tt, LANES), F32),
                   jax.ShapeDtypeStruct((B, E, LANES), jnp.int32)],
        compiler_params=_params(("parallel",), 48),
    )(arow, acol)


def _gather_kernel(st_ref, p_ref, h_ref, o_ref, acc_ref, *, chunk0, nchunk, cap, st):
    b = pl.program_id(0)
    e = pl.program_id(1)
    base = (b * N_EXPERTS + e) * (nchunk + 1)
    for S in range(cap // st):
        acc_ref[...] = jnp.zeros(acc_ref.shape, F32)
        slot = (lax.broadcasted_iota(jnp.int32, (st, TR), 0) + S * st).astype(F32)

        def body(c, carry):
            lo = st_ref[base + c]
            hi = st_ref[base + c + 1]

            @pl.when((lo < (S + 1) * st) & (hi > S * st))
            def _():
                onehot = (p_ref[0, pl.ds(chunk0 + c, 1), :] == slot).astype(BF16)
                r0 = pl.multiple_of(c * TR, TR)
                acc_ref[...] += _dot(onehot, h_ref[0, pl.ds(r0, TR), :])
            return carry

        lax.fori_loop(0, nchunk, body, 0)
        o_ref[0, S * st:(S + 1) * st, :] = acc_ref[...].astype(BF16)


def _gather(starts, prow3, hbf, xs, *, chunk0, nchunk, cap, slot_blk0):
    B, Tt, D = hbf.shape
    E = N_EXPERTS
    nT = Tt // TR
    st = min(cap, 128)
    rows = nchunk * TR
    in_specs = [pl.BlockSpec((1, nT, TR), lambda b, e, s: (b * E + e, 0, 0)),
                pl.BlockSpec((1, rows, D), lambda b, e, s: (b, chunk0 * TR // rows, 0))]
    args = [prow3, hbf]
    kern = functools.partial(_gather_kernel, chunk0=chunk0, nchunk=nchunk, cap=cap, st=st)
    aliases = {}
    if not isinstance(xs, jax.ShapeDtypeStruct):
        in_specs.append(pl.BlockSpec(memory_space=pl.ANY))
        args.append(xs)
        aliases = {3: 0}
        inner = kern
        kern = lambda s, p, h, _x, o, acc: inner(s, p, h, o, acc)
    grid_spec = pltpu.PrefetchScalarGridSpec(
        num_scalar_prefetch=1, grid=(B, E), in_specs=in_specs,
        out_specs=pl.BlockSpec((1, cap, D), lambda b, e, s: (e, slot_blk0 + b, 0)),
        scratch_shapes=[pltpu.VMEM((st, D), F32)])
    return pl.pallas_call(
        kern, grid_spec=grid_spec,
        out_shape=jax.ShapeDtypeStruct(xs.shape, BF16),
        input_output_aliases=aliases,
        compiler_params=_params(("parallel", "arbitrary"), 56),
    )(starts, *args)


def _ffn1_kernel(x_ref, wg_ref, wu_ref, o_ref, wgb_ref, wub_ref):
    @pl.when(pl.program_id(2) == 0)
    def _():
        wgb_ref[...] = wg_ref[0].astype(BF16)
        wub_ref[...] = wu_ref[0].astype(BF16)
    x = x_ref[0]
    g = _dot(x, wgb_ref[...])
    u = _dot(x, wub_ref[...])
    o_ref[0] = (g * _sigmoid(g) * u).astype(BF16)


def _ffn2_kernel(x_ref, w_ref, o_ref, wb_ref):
    @pl.when(pl.program_id(2) == 0)
    def _():
        wb_ref[...] = w_ref[0].astype(BF16)
    o_ref[0] = _dot(x_ref[0], wb_ref[...]).astype(BF16)


def _expert_matmul(kernel, x, ws):
    E, S, K = x.shape
    N = ws[0].shape[2]
    tm = _pick(S, (1088, 1024, 544, 512, 272, 256, 160, 128))
    tn = _pick(N, (512, 256))
    wspec = pl.BlockSpec((1, K, tn), lambda e, n, m: (e, 0, n))
    return pl.pallas_call(
        kernel, grid=(E, N // tn, S // tm),
        in_specs=[pl.BlockSpec((1, tm, K), lambda e, n, m: (e, m, 0))] + [wspec] * len(ws),
        out_specs=pl.BlockSpec((1, tm, tn), lambda e, n, m: (e, m, n)),
        out_shape=jax.ShapeDtypeStruct((E, S, N), BF16),
        scratch_shapes=[pltpu.VMEM((K, tn), BF16)] * len(ws),
        compiler_params=_params(("parallel", "parallel", "arbitrary"), 48),
    )(x, *ws)


def _combine_kernel(st_ref, pc_ref, ac_ref, ye_ref, x_ref, gp_ref, gate_ref, o_ref, acc_ref,
                    *, nchunk, cpt, st):
    b = pl.program_id(0)
    t = pl.program_id(1)
    e = pl.program_id(2)
    rows = acc_ref.shape[0]

    @pl.when(e == 0)
    def _():
        acc_ref[...] = jnp.zeros(acc_ref.shape, F32)

    base = (b * N_EXPERTS + e) * (nchunk + 1) + t * cpt
    lo = st_ref[base]
    hi = st_ref[base + cpt]
    lane = lax.broadcasted_iota(jnp.int32, (rows, LANES), 1)
    mine = lane == e
    pos = jnp.sum(jnp.where(mine, pc_ref[0], 0.0), axis=1, keepdims=True)
    aff = jnp.sum(jnp.where(mine, ac_ref[0], 0.0), axis=1, keepdims=True)
    slot = lax.broadcasted_iota(jnp.int32, (rows, st), 1).astype(F32)

    def body(S, carry):
        onehot_t = (pos - (S * st).astype(F32) == slot).astype(BF16)
        r0 = pl.multiple_of(S * st, st)
        acc_ref[...] += aff * _dot(onehot_t, ye_ref[0, pl.ds(r0, st), :])
        return carry

    lax.fori_loop(lo // st, (hi + st - 1) // st, body, 0)

    @pl.when(e == N_EXPERTS - 1)
    def _():
        o_ref[0] = x_ref[0] + gate_ref[0] * (_rms(acc_ref[...]) * gp_ref[...])


def _combine(starts, pcol, acol, ye, xc, g_post, mod3, gate_chunk, *, rows, tile0, ntile, nchunk, cap,
             slot_blk0, mod_row_ctx, out):
    B, Tt, D = xc.shape
    E = N_EXPERTS
    st = min(cap, 128)
    cpt = rows // TR
    fresh = out is not None
    if not fresh:
        out = xc
    tok = lambda w: pl.BlockSpec((1, rows, w), lambda b, t, e, s: (b, tile0 + t, 0))
    mrow = (lambda b: B) if mod_row_ctx else (lambda b: b)
    in_specs = [tok(LANES), tok(LANES),
                pl.BlockSpec((1, cap, D), lambda b, t, e, s: (e, slot_blk0 + b, 0)),
                tok(D), pl.BlockSpec((1, D), lambda b, t, e, s: (0, 0)),
                pl.BlockSpec((1, 1, D), lambda b, t, e, s: (mrow(b), 0, gate_chunk))]
    args = [pcol, acol, ye, xc, g_post.reshape(1, D), mod3]
    kern = functools.partial(_combine_kernel, nchunk=nchunk, cpt=cpt, st=st)
    if fresh:
        out_spec = pl.BlockSpec((1, rows, D), lambda b, t, e, s: (b, t, 0))
        aliases = {}
    else:
        out_spec = tok(D)
        aliases = {4: 0}
    grid_spec = pltpu.PrefetchScalarGridSpec(
        num_scalar_prefetch=1, grid=(B, ntile, E), in_specs=in_specs, out_specs=out_spec,
        scratch_shapes=[pltpu.VMEM((rows, D), F32)])
    return pl.pallas_call(
        kern, grid_spec=grid_spec,
        out_shape=jax.ShapeDtypeStruct(out.shape, F32),
        input_output_aliases=aliases,
        compiler_params=_params(("parallel", "parallel", "arbitrary"), 48),
    )(starts, *args)


def _rope_tables(T, Lc):
    nf = HEAD_DIM // 4
    t = np.arange(T)
    inv = ROPE_THETA ** (-jnp.arange(nf, dtype=F32) / nf)
    row = jnp.asarray(t // GRID_W, F32)
    col = jnp.asarray(t % GRID_W, F32)
    ang = jnp.stack([row[:, None] * inv, col[:, None] * inv], axis=1)
    cos = jnp.broadcast_to(jnp.cos(ang)[:, :, None, :], (T, 2, 2, nf)).reshape(T, HEAD_DIM)
    sin = jnp.broadcast_to(jnp.sin(ang)[:, :, None, :], (T, 2, 2, nf)).reshape(T, HEAD_DIM)
    sign = jnp.asarray(np.tile(np.repeat([-1.0, 1.0], nf), 2), F32)
    cos = jnp.concatenate([cos, jnp.ones((Lc, HEAD_DIM), F32)], axis=0)
    sin = jnp.concatenate([sin * sign, jnp.zeros((Lc, HEAD_DIM), F32)], axis=0)
    return cos, sin


def _split_bf16(w):
    hi = w.astype(BF16)
    return hi, (w - hi.astype(F32)).astype(BF16)


def kernel(x, c, ctx, c_ctx, w_mod, b_mod, g_pre_mix, g_post_mix, g_pre_ffn, g_post_ffn, w_in, mlstm_gate_bias, gqa_q_norm, gqa_k_norm, nat_rel_bias, w_branch, w_out, w_router, w_expert_gate, w_expert_up, w_expert_down):
    B, T, D = x.shape
    Lc = ctx.shape[1]
    depth = w_mod.shape[0]
    E = N_EXPERTS
    assert Lc == TR and T % (2 * TR) == 0 and w_router.shape[2] == E
    Tt = T + Lc
    nT = Tt // TR
    M = B * Tt
    cap_l = EC_CAPACITY_FACTOR * T // E
    cap_c = EC_CAPACITY_FACTOR * Lc // E
    scale = HEAD_DIM ** -0.5
    nchunk = T // TR

    xc = jnp.concatenate([x, ctx], axis=1)
    rows_c = -(-(B + 1) // 8) * 8
    cc = jnp.zeros((rows_c, D), F32).at[:B].set(c).at[B].set(c_ctx)
    mod = _modulation(cc, w_mod, b_mod)
    cos_t, sin_t = _rope_tables(T, Lc)

    for l in range(depth):
        last = l == depth - 1
        mod3 = mod[l].reshape(rows_c, 1, N_MOD * D)
        w_l = w_in[l]
        w_main = jnp.concatenate([w_l[:, :W_IN_GATE_COL], w_l[:, W_IN_GATE_COL + N_GATE_COLS:]], axis=1).astype(BF16)
        w_g = jnp.pad(w_l[:, W_IN_GATE_COL:W_IN_GATE_COL + N_GATE_COLS], ((0, 0), (0, LANES - N_GATE_COLS))).astype(BF16)
        gate_bias = jnp.pad(mlstm_gate_bias[l], (0, LANES - N_GATE_COLS)).reshape(1, LANES)
        nw = jnp.concatenate([jnp.tile(gqa_q_norm[l][None] * scale, (GQA_HEADS, 1)),
                              jnp.tile(gqa_k_norm[l][None], (GQA_KV_HEADS, 1))], axis=0)[:, None, :]

        hx = _norm_mod(xc, g_pre_mix[l], mod3, 0, 1)
        hx2 = hx.reshape(M, D)
        P2 = _matmul(hx2, w_main, BF16)
        P = P2.reshape(B, Tt, -1)
        gates = _matmul(hx2, w_g, F32).reshape(B, Tt, LANES)

        hf, hb = _mlstm(P, gates, gate_bias, T, Lc)

        qk = _qk_prep(P, nw, cos_t, sin_t)
        yg = _flash(qk, 0, qk, GQA_HEADS * HEAD_DIM, P, C_GV, G=GQA_KV_HEADS, R=GQA_HEADS // GQA_KV_HEADS,
                    T=T, Lc=Lc, tile0=0, nq=nT, kv_all=True, scale=1.0)

        bias_tab = _nat_bias_tables(nat_rel_bias[l], T)
        yn = _nat(P, bias_tab, T, Lc, scale)
        yn = _flash(P, C_NQ, P, C_NK, P, C_NV, G=NAT_HEADS, R=1, T=T, Lc=Lc, tile0=nT - 1, nq=1,
                    kv_all=False, scale=scale, out=yn)

        merged = _merge(hf.reshape(M, -1), hb.reshape(M, -1), P2, yg.reshape(M, -1), yn.reshape(M, -1),
                        w_branch[l].astype(BF16))
        xc = _wout_post(merged.reshape(B, Tt, D), w_out[l].astype(BF16), xc, g_post_mix[l], mod3, 2)

        wr = jnp.pad(w_router[l], ((0, 0), (0, LANES - E)))
        hx, acol, arow = _norm_mod(xc, g_pre_ffn[l], mod3, 3, 4, router=_split_bf16(wr))
        prow, pcol, starts = _route(arow, acol, T, Lc, cap_l, cap_c)
        starts_flat = starts[:, :, :nchunk + 1].reshape(-1)
        prow3 = prow.reshape(B * E, nT, TR)

        S = B * cap_l + (0 if last else B * cap_c)
        xs = _gather(starts_flat, prow3, hx, jax.ShapeDtypeStruct((E, S, D), BF16),
                     chunk0=0, nchunk=nchunk, cap=cap_l, slot_blk0=0)
        ctx_starts = jnp.tile(jnp.asarray([0, cap_c], jnp.int32), B * E)
        if not last:
            xs = _gather(ctx_starts, prow3, hx, xs, chunk0=nchunk, nchunk=1, cap=cap_c,
                         slot_blk0=B * cap_l // cap_c)
        hid = _expert_matmul(_ffn1_kernel, xs, (w_expert_gate[l], w_expert_up[l]))
        ye = _expert_matmul(_ffn2_kernel, hid, (w_expert_down[l],))

        rows = _pick(T, (512, 256))
        if last:
            return _combine(starts_flat, pcol, acol, ye, xc, g_post_ffn[l], mod3, 5, rows=rows, tile0=0,
                            ntile=T // rows, nchunk=nchunk, cap=cap_l, slot_blk0=0, mod_row_ctx=False,
                            out=jax.ShapeDtypeStruct((B, T, D), F32))
        xc = _combine(starts_flat, pcol, acol, ye, xc, g_post_ffn[l], mod3, 5, rows=rows, tile0=0,
                      ntile=T // rows, nchunk=nchunk, cap=cap_l, slot_blk0=0, mod_row_ctx=False, out=None)
        xc = _combine(ctx_starts, pcol, acol, ye, xc, g_post_ffn[l], mod3, 5, rows=TR, tile0=nT - 1,
                      ntile=1, nchunk=1, cap=cap_c, slot_blk0=B * cap_l // cap_c, mod_row_ctx=True, out=None)
```

```python
import functools

import numpy as np
import jax
import jax.numpy as jnp
from jax import lax
from jax.experimental import pallas as pl
from jax.experimental.pallas import tpu as pltpu

F32 = jnp.float32
BF16 = jnp.bfloat16

LANES = 128
HEAD_DIM = 128
MLSTM_HEADS = 4
MLSTM_DQK = 128
MLSTM_DV = 256
MLSTM_CHUNK = 128
GQA_HEADS = 8
GQA_KV_HEADS = 2
NAT_HEADS = 8
NAT_WIN_ROWS = 8
NAT_WIN_COLS = 16
GRID_W = 64
Q_BLOCK = 128
N_EXPERTS = 16
EC_CAPACITY_FACTOR = 2
ROPE_THETA = 10000.0
NORM_EPS = 1e-6
NEG_INF = -1e30
N_MOD = 6
N_BRANCHES = 3
BRANCH_WIDTH = 1024
TR = 256
N_GATE_COLS = 4 * MLSTM_HEADS

C_MQ = 0
C_MK = C_MQ + MLSTM_HEADS * MLSTM_DQK
C_MV = C_MK + MLSTM_HEADS * MLSTM_DQK
C_MO = C_MV + MLSTM_HEADS * MLSTM_DV
C_GQ = C_MO + MLSTM_HEADS * MLSTM_DV
C_GK = C_GQ + GQA_HEADS * HEAD_DIM
C_GV = C_GK + GQA_KV_HEADS * HEAD_DIM
C_NQ = C_GV + GQA_KV_HEADS * HEAD_DIM
C_NK = C_NQ + NAT_HEADS * HEAD_DIM
C_NV = C_NK + NAT_HEADS * HEAD_DIM
C_GATE = C_NV + NAT_HEADS * HEAD_DIM
W_IN_GATE_COL = C_GQ


def _params(sem, vmem_mb=None):
    return pltpu.CompilerParams(
        dimension_semantics=sem,
        vmem_limit_bytes=None if vmem_mb is None else vmem_mb * 2 ** 20)


def _pick(n, cands):
    for c in cands:
        if n % c == 0:
            return c
    raise ValueError(f"no tile for {n} in {cands}")


def _dot(a, b):
    return jnp.dot(a, b, preferred_element_type=F32)


def _dot_nt(a, b):
    return lax.dot_general(a, b, (((1,), (1,)), ((), ())), preferred_element_type=F32)


def _sigmoid(x):
    return 1.0 / (1.0 + jnp.exp(-x))


def _rms(x):
    return x * lax.rsqrt(jnp.mean(x * x, axis=-1, keepdims=True) + NORM_EPS)


def _mod_kernel(c_ref, w_ref, b_ref, o_ref):
    c = c_ref[...]
    a = (c * _sigmoid(c)).astype(BF16)
    o_ref[0] = _dot(a, w_ref[0].astype(BF16)) + b_ref[0]


def _modulation(cc, w_mod, b_mod):
    L, D, N = w_mod.shape
    tn = _pick(N, (768, 512, 256, 128))
    return pl.pallas_call(
        _mod_kernel,
        grid=(L, N // tn),
        in_specs=[pl.BlockSpec((cc.shape[0], D), lambda l, n: (0, 0)),
                  pl.BlockSpec((1, D, tn), lambda l, n: (l, 0, n)),
                  pl.BlockSpec((1, 1, tn), lambda l, n: (l, 0, n))],
        out_specs=pl.BlockSpec((1, cc.shape[0], tn), lambda l, n: (l, 0, n)),
        out_shape=jax.ShapeDtypeStruct((L, cc.shape[0], N), F32),
        compiler_params=_params(("parallel", "parallel")),
    )(cc, w_mod, b_mod.reshape(L, 1, N))


def _norm_mod_kernel(x_ref, g_ref, sh_ref, sc_ref, o_ref):
    y = _rms(x_ref[0]) * g_ref[...]
    o_ref[0] = (y * (1.0 + sc_ref[0]) + sh_ref[0]).astype(BF16)


def _norm_mod_router_kernel(x_ref, g_ref, sh_ref, sc_ref, whi_ref, wlo_ref, o_ref, acol_ref, arow_ref):
    y = _rms(x_ref[0]) * g_ref[...]
    h = y * (1.0 + sc_ref[0]) + sh_ref[0]
    hb = h.astype(BF16)
    o_ref[0] = hb
    hlo = (h - hb.astype(F32)).astype(BF16)
    logits = _dot(hb, whi_ref[...]) + _dot(hlo, whi_ref[...]) + _dot(hb, wlo_ref[...])
    lane = lax.broadcasted_iota(jnp.int32, logits.shape, 1)
    valid = lane < N_EXPERTS
    logits = jnp.where(valid, logits, NEG_INF)
    ex = jnp.where(valid, jnp.exp(logits - jnp.max(logits, axis=1, keepdims=True)), 0.0)
    aff = ex / jnp.sum(ex, axis=1, keepdims=True)
    acol_ref[0] = aff
    arow_ref[0] = aff.T[:N_EXPERTS]


def _mod_specs(B, nT, D, shift_chunk, scale_chunk):
    row = lambda b, i: jnp.where(i == nT - 1, B, b)
    return [pl.BlockSpec((1, 1, D), lambda b, i: (row(b, i), 0, shift_chunk)),
            pl.BlockSpec((1, 1, D), lambda b, i: (row(b, i), 0, scale_chunk))]


def _norm_mod(xc, g, mod3, shift_chunk, scale_chunk, router=None):
    B, Tt, D = xc.shape
    nT = Tt // TR
    tile = pl.BlockSpec((1, TR, D), lambda b, i: (b, i, 0))
    in_specs = [tile, pl.BlockSpec((1, D), lambda b, i: (0, 0))] + _mod_specs(B, nT, D, shift_chunk, scale_chunk)
    args = [xc, g.reshape(1, D), mod3, mod3]
    if router is None:
        return pl.pallas_call(
            _norm_mod_kernel, grid=(B, nT), in_specs=in_specs, out_specs=tile,
            out_shape=jax.ShapeDtypeStruct((B, Tt, D), BF16),
            compiler_params=_params(("parallel", "parallel")),
        )(*args)
    whi, wlo = router
    wspec = pl.BlockSpec((D, LANES), lambda b, i: (0, 0))
    return pl.pallas_call(
        _norm_mod_router_kernel, grid=(B, nT), in_specs=in_specs + [wspec, wspec],
        out_specs=[tile, pl.BlockSpec((1, TR, LANES), lambda b, i: (b, i, 0)),
                   pl.BlockSpec((1, N_EXPERTS, TR), lambda b, i: (b, 0, i))],
        out_shape=[jax.ShapeDtypeStruct((B, Tt, D), BF16),
                   jax.ShapeDtypeStruct((B, Tt, LANES), F32),
                   jax.ShapeDtypeStruct((B, N_EXPERTS, Tt), F32)],
        compiler_params=_params(("parallel", "parallel")),
    )(*args, whi, wlo)


def _mm_kernel(a_ref, w_ref, o_ref):
    o_ref[...] = _dot(a_ref[...], w_ref[...]).astype(o_ref.dtype)


def _matmul(a, w, out_dtype):
    M, K = a.shape
    N = w.shape[1]
    tm = _pick(M, (1024, 512, 256))
    tn = _pick(N, (512, 256, 128))
    return pl.pallas_call(
        _mm_kernel, grid=(M // tm, N // tn),
        in_specs=[pl.BlockSpec((tm, K), lambda i, j: (i, 0)), pl.BlockSpec((K, tn), lambda i, j: (0, j))],
        out_specs=pl.BlockSpec((tm, tn), lambda i, j: (i, j)),
        out_shape=jax.ShapeDtypeStruct((M, N), out_dtype),
        compiler_params=_params(("parallel", "parallel")),
    )(a, w)


def _qk_prep_kernel(q_ref, k_ref, nw_ref, cos_ref, sin_ref, o_ref):
    lane = lax.broadcasted_iota(jnp.int32, (q_ref.shape[1], HEAD_DIM), 1)
    first = (lane % (HEAD_DIM // 2)) < HEAD_DIM // 4
    cos = cos_ref[...]
    sin = sin_ref[...]
    for h in range(o_ref.shape[2] // HEAD_DIM):
        src, hh = (q_ref, h) if h < GQA_HEADS else (k_ref, h - GQA_HEADS)
        y = _rms(src[0, :, hh * HEAD_DIM:(hh + 1) * HEAD_DIM].astype(F32)) * nw_ref[:, h * HEAD_DIM:(h + 1) * HEAD_DIM]
        partner = jnp.where(first, pltpu.roll(y, HEAD_DIM - HEAD_DIM // 4, 1), pltpu.roll(y, HEAD_DIM // 4, 1))
        o_ref[0, :, h * HEAD_DIM:(h + 1) * HEAD_DIM] = (y * cos + partner * sin).astype(BF16)


def _qk_prep(P, nw, cos_t, sin_t):
    B, Tt, _ = P.shape
    qw, kw = GQA_HEADS * HEAD_DIM, GQA_KV_HEADS * HEAD_DIM
    w = qw + kw
    tr = _pick(Tt, (544, TR))
    return pl.pallas_call(
        _qk_prep_kernel, grid=(B, Tt // tr),
        in_specs=[pl.BlockSpec((1, tr, qw), lambda b, i: (b, i, C_GQ // qw)),
                  pl.BlockSpec((1, tr, kw), lambda b, i: (b, i, C_GK // kw)),
                  pl.BlockSpec((1, w), lambda b, i: (0, 0)),
                  pl.BlockSpec((tr, HEAD_DIM), lambda b, i: (i, 0)),
                  pl.BlockSpec((tr, HEAD_DIM), lambda b, i: (i, 0))],
        out_specs=pl.BlockSpec((1, tr, w), lambda b, i: (b, i, 0)),
        out_shape=jax.ShapeDtypeStruct((B, Tt, w), BF16),
        compiler_params=_params(("parallel", "parallel")),
    )(P, P, nw.reshape(1, w), cos_t, sin_t)


def _flash_kernel(q_ref, k_ref, v_ref, o_ref, m_ref, l_ref, acc_ref, *, R, T, Lc, ck, scale, tile0, n_lat_tiles):
    tile = pl.program_id(2) + tile0

    m_ref[...] = jnp.full(m_ref.shape, NEG_INF, F32)
    l_ref[...] = jnp.zeros(l_ref.shape, F32)
    acc_ref[...] = jnp.zeros(acc_ref.shape, F32)

    def step(k, v):
        for r in range(R):
            rows = slice(r * TR, (r + 1) * TR)
            s = _dot_nt(q_ref[0, :, r * HEAD_DIM:(r + 1) * HEAD_DIM], k)
            if scale != 1.0:
                s = s * scale
            slabs = [s[:, c * LANES:(c + 1) * LANES] for c in range(s.shape[1] // LANES)]
            m_old = m_ref[rows, :]
            m_new = jnp.maximum(m_old, jnp.max(functools.reduce(jnp.maximum, slabs), axis=1, keepdims=True))
            alpha = jnp.exp(m_old - m_new)
            ps = [jnp.exp(sl - m_new) for sl in slabs]
            l_ref[rows, :] = alpha * l_ref[rows, :] + functools.reduce(jnp.add, ps)
            p = jnp.concatenate(ps, axis=1).astype(BF16)
            acc_ref[rows, :] = alpha * acc_ref[rows, :] + _dot(p, v)
            m_ref[rows, :] = m_new

    kv_rows = k_ref.shape[1]
    if kv_rows == Lc:
        step(k_ref[0], v_ref[0])
    else:
        @pl.when(tile >= n_lat_tiles)
        def _():
            step(k_ref[0, kv_rows - Lc:, :], v_ref[0, kv_rows - Lc:, :])

        @pl.when(tile < n_lat_tiles)
        def _():
            def body(j, carry):
                r0 = pl.multiple_of(j * ck, ck)
                step(k_ref[0, pl.ds(r0, ck), :], v_ref[0, pl.ds(r0, ck), :])
                return carry
            lax.fori_loop(0, kv_rows // ck, body, 0)

    o = acc_ref[...] / jnp.sum(l_ref[...], axis=1, keepdims=True)
    for r in range(R):
        o_ref[0, :, r * HEAD_DIM:(r + 1) * HEAD_DIM] = o[r * TR:(r + 1) * TR].astype(o_ref.dtype)


def _flash(q_arr, q_col, k_arr, k_col, v_arr, v_col, *, G, R, T, Lc, tile0, nq, kv_all, scale, out=None):
    B, Tt, _ = q_arr.shape
    qw = R * HEAD_DIM
    kv_rows = Tt if kv_all else Lc
    kv_blk = 0 if kv_all else T // Lc
    ck = max(c for c in range(LANES, 2304 + 1, LANES) if Tt % c == 0)
    kern = functools.partial(_flash_kernel, R=R, T=T, Lc=Lc, ck=ck, scale=scale, tile0=tile0,
                             n_lat_tiles=T // TR)
    in_specs = [pl.BlockSpec((1, TR, qw), lambda b, g, i: (b, i + tile0, q_col // qw + g)),
                pl.BlockSpec((1, kv_rows, HEAD_DIM), lambda b, g, i: (b, kv_blk, k_col // HEAD_DIM + g)),
                pl.BlockSpec((1, kv_rows, HEAD_DIM), lambda b, g, i: (b, kv_blk, v_col // HEAD_DIM + g))]
    args = [q_arr, k_arr, v_arr]
    aliases = {}
    if out is not None:
        in_specs.append(pl.BlockSpec(memory_space=pl.ANY))
        args.append(out)
        aliases = {3: 0}
        kern_fn = lambda q, k, v, _o, o, m, l, a: kern(q, k, v, o, m, l, a)
    else:
        kern_fn = kern
    return pl.pallas_call(
        kern_fn, grid=(B, G, nq), in_specs=in_specs,
        out_specs=pl.BlockSpec((1, TR, qw), lambda b, g, i: (b, i + tile0, g)),
        out_shape=jax.ShapeDtypeStruct((B, Tt, G * qw), BF16),
        scratch_shapes=[pltpu.VMEM((R * TR, LANES), F32), pltpu.VMEM((R * TR, LANES), F32),
                        pltpu.VMEM((R * TR, HEAD_DIM), F32)],
        input_output_aliases=aliases,
        compiler_params=_params(("parallel", "parallel", "arbitrary"), 48),
    )(*args)


def _nat_geometry(T):
    rows = T // GRID_W
    kh = min(NAT_WIN_ROWS, rows)
    nkr = min(kh + 1, rows)
    qbr = Q_BLOCK // GRID_W
    nbr = rows // qbr
    kb = np.zeros(nbr, np.int32)
    var = np.zeros(nbr, np.int32)
    variants = []
    for j in range(nbr):
        qrow = j * qbr + np.arange(qbr)
        start_r = np.clip(qrow - kh // 2, 0, rows - kh)
        kb[j] = min(start_r[0], rows - nkr)
        sig = (int(kb[j] - j * qbr), tuple(int(s - kb[j]) for s in start_r))
        if sig not in variants:
            variants.append(sig)
        var[j] = variants.index(sig)
    return kh, nkr, qbr, nbr, kb, var, variants


def _nat_bias_kernel(rb_ref, o_ref, *, variants, kh, nkr, qbr):
    h = pl.program_id(0)
    ndr = 2 * NAT_WIN_ROWS - 1
    ndc = 2 * NAT_WIN_COLS - 1
    qc = lax.broadcasted_iota(jnp.int32, (GRID_W, GRID_W), 0)
    kc = lax.broadcasted_iota(jnp.int32, (GRID_W, GRID_W), 1)
    dc = kc - qc + NAT_WIN_COLS - 1
    start_c = jnp.clip(qc - NAT_WIN_COLS // 2, 0, GRID_W - NAT_WIN_COLS)
    col_in = (kc >= start_c) & (kc < start_c + NAT_WIN_COLS)
    masked = jnp.full((GRID_W, GRID_W), NEG_INF, F32)
    toeplitz = []
    for dr in range(ndr):
        t = jnp.zeros((GRID_W, GRID_W), F32)
        for d in range(ndc):
            t = jnp.where(dc == d, rb_ref[(h * ndr + dr) * ndc + d], t)
        toeplitz.append(jnp.where(col_in, t, NEG_INF))
    for v, (delta, srel) in enumerate(variants):
        for qr in range(qbr):
            blocks = []
            for kr in range(nkr):
                if srel[qr] <= kr < srel[qr] + kh:
                    blocks.append(toeplitz[int(np.clip(kr + delta - qr + NAT_WIN_ROWS - 1, 0, ndr - 1))])
                else:
                    blocks.append(masked)
            rows = slice(qr * GRID_W, (qr + 1) * GRID_W)
            for kr in range(0, nkr - 1, 2):
                o_ref[v, 0, rows, kr * GRID_W:(kr + 2) * GRID_W] = jnp.concatenate(blocks[kr:kr + 2], axis=1)
            if nkr % 2:
                o_ref[v, 0, rows, (nkr - 1) * GRID_W:] = blocks[-1]


def _nat_bias_tables(rel_bias, T):
    kh, nkr, qbr, _, _, _, variants = _nat_geometry(T)
    V = len(variants)
    nk = nkr * GRID_W
    return pl.pallas_call(
        functools.partial(_nat_bias_kernel, variants=variants, kh=kh, nkr=nkr, qbr=qbr),
        grid=(NAT_HEADS,),
        in_specs=[pl.BlockSpec(memory_space=pltpu.SMEM)],
        out_specs=pl.BlockSpec((V, 1, Q_BLOCK, nk), lambda h: (0, h, 0, 0)),
        out_shape=jax.ShapeDtypeStruct((V, NAT_HEADS, Q_BLOCK, nk), F32),
        compiler_params=_params(("parallel",)),
    )(rel_bias.reshape(-1))


def _nat_kernel(kb_ref, var_ref, q_ref, k_ref, v_ref, bias_ref, o_ref, *, T, nbr, nk, scale):
    kc = k_ref[0, T:, :]
    vc = v_ref[0, T:, :]

    def body(j, carry):
        q0 = pl.multiple_of(j * Q_BLOCK, Q_BLOCK)
        q = q_ref[0, pl.ds(q0, Q_BLOCK), :]
        r0 = pl.multiple_of(kb_ref[j] * GRID_W, GRID_W)
        s1 = _dot_nt(q, k_ref[0, pl.ds(r0, nk), :]) * scale + bias_ref[var_ref[j], 0]
        s2 = _dot_nt(q, kc) * scale
        m = jnp.maximum(jnp.max(s1, axis=1, keepdims=True), jnp.max(s2, axis=1, keepdims=True))
        p1 = jnp.exp(s1 - m)
        p2 = jnp.exp(s2 - m)
        l = jnp.sum(p1, axis=1, keepdims=True) + jnp.sum(p2, axis=1, keepdims=True)
        o = _dot(p1.astype(BF16), v_ref[0, pl.ds(r0, nk), :]) + _dot(p2.astype(BF16), vc)
        o_ref[0, pl.ds(q0, Q_BLOCK), :] = (o / l).astype(o_ref.dtype)
        return carry

    lax.fori_loop(0, nbr, body, 0, unroll=2)


def _nat(P, bias_tab, T, Lc, scale):
    B, Tt, _ = P.shape
    _, nkr, _, nbr, kb, var, _ = _nat_geometry(T)
    nk = nkr * GRID_W
    V = bias_tab.shape[0]
    grid_spec = pltpu.PrefetchScalarGridSpec(
        num_scalar_prefetch=2, grid=(B, NAT_HEADS),
        in_specs=[pl.BlockSpec((1, T, HEAD_DIM), lambda b, h, kb_, vr: (b, 0, C_NQ // HEAD_DIM + h)),
                  pl.BlockSpec((1, Tt, HEAD_DIM), lambda b, h, kb_, vr: (b, 0, C_NK // HEAD_DIM + h)),
                  pl.BlockSpec((1, Tt, HEAD_DIM), lambda b, h, kb_, vr: (b, 0, C_NV // HEAD_DIM + h)),
                  pl.BlockSpec((V, 1, Q_BLOCK, nk), lambda b, h, kb_, vr: (0, h, 0, 0))],
        out_specs=pl.BlockSpec((1, T, HEAD_DIM), lambda b, h, kb_, vr: (b, 0, h)))
    return pl.pallas_call(
        functools.partial(_nat_kernel, T=T, nbr=nbr, nk=nk, scale=scale),
        grid_spec=grid_spec,
        out_shape=jax.ShapeDtypeStruct((B, Tt, NAT_HEADS * HEAD_DIM), BF16),
        compiler_params=_params(("parallel", "parallel")),
    )(jnp.asarray(kb), jnp.asarray(var), P, P, P, bias_tab)


def _mlstm_kernel(qf_ref, kf_ref, vf_ref, gf_ref, qb_ref, kb_ref, vb_ref, gb_ref, bias_ref,
                  hf_ref, hb_ref, st_ref, m_ref, *, kscale):
    L = MLSTM_CHUNK
    dv = MLSTM_DV

    @pl.when(pl.program_id(1) == 0)
    def _():
        st_ref[...] = jnp.zeros(st_ref.shape, F32)
        m_ref[...] = jnp.full(m_ref.shape, NEG_INF, F32)

    ti = lax.broadcasted_iota(jnp.int32, (L, L), 0)
    si = lax.broadcasted_iota(jnp.int32, (L, L), 1)
    ones_col = (lax.broadcasted_iota(jnp.int32, (L, LANES), 1) == 0).astype(BF16)

    for dirn, (q_ref, k_ref, v_ref, g_ref, h_ref) in enumerate(
            ((qf_ref, kf_ref, vf_ref, gf_ref, hf_ref), (qb_ref, kb_ref, vb_ref, gb_ref, hb_ref))):
        tri = (si <= ti) if dirn == 0 else (si >= ti)
        G = g_ref[0] + bias_ref[...]
        LF = jnp.minimum(G, 0.0) - jnp.log(1.0 + jnp.exp(-jnp.abs(G)))
        lf_hi = LF.astype(BF16)
        lf_lo = (LF - lf_hi.astype(F32)).astype(BF16)
        trib = tri.astype(BF16)
        Bc = _dot(trib, lf_hi) + _dot(trib, lf_lo)
        BcT = Bc.T
        GT = G.T
        bl_row = Bc[L - 1:L, :] if dirn == 0 else Bc[0:1, :]
        for h in range(MLSTM_HEADS):
            ci = dirn * 2 * MLSTM_HEADS + h
            cf = ci + MLSTM_HEADS
            bcol = Bc[:, cf:cf + 1]
            brow = BcT[cf:cf + 1, :]
            licol = G[:, ci:ci + 1]
            lirow = GT[ci:ci + 1, :]
            bl = bl_row[:, cf:cf + 1]
            m_old = m_ref[dirn, h]
            gcol = bl - bcol + licol
            m_new = jnp.maximum(bl + m_old, jnp.max(gcol, axis=0, keepdims=True))
            decay = jnp.exp(bl + m_old - m_new)
            wk = jnp.exp(gcol - m_new)
            dmat = jnp.where(tri, bcol - brow + lirow, NEG_INF)
            inter = bcol + m_old
            m_t = jnp.maximum(inter, jnp.max(dmat, axis=1, keepdims=True))
            w = jnp.exp(dmat - m_t)
            a = jnp.exp(inter - m_t)
            qh = q_ref[0, :, h * MLSTM_DQK:(h + 1) * MLSTM_DQK]
            kh = k_ref[0, :, h * MLSTM_DQK:(h + 1) * MLSTM_DQK]
            vaug = jnp.concatenate([v_ref[0, :, h * dv:(h + 1) * dv], ones_col], axis=1)
            smat = _dot_nt(qh, kh) * kscale * w
            state = st_ref[dirn, h]
            nd = _dot(smat.astype(BF16), vaug) + a * _dot(qh, state.astype(BF16))
            den = jnp.maximum(jnp.abs(nd[:, dv:dv + 1]), jnp.exp(-m_t))
            h_ref[0, :, h * dv:(h + 1) * dv] = nd[:, :dv] / den
            kT = kh.astype(F32).T.astype(BF16)
            upd = _dot(kT, (wk * vaug.astype(F32)).astype(BF16)) * kscale
            st_ref[dirn, h] = decay * state + upd
            m_ref[dirn, h] = m_new


def _mlstm(P, gates, gate_bias, T, Lc):
    B, Tt, _ = P.shape
    L = MLSTM_CHUNK
    ncl, ncc = T // L, Lc // L
    fwd = lambda i: jnp.where(i < ncc, ncl + i, i - ncc)
    bwd = lambda i: jnp.where(i < ncc, ncl + ncc - 1 - i, ncl - 1 - (i - ncc))
    qw = MLSTM_HEADS * MLSTM_DQK
    vw = MLSTM_HEADS * MLSTM_DV

    def specs(order):
        return [pl.BlockSpec((1, L, qw), lambda b, i: (b, order(i), C_MQ // qw)),
                pl.BlockSpec((1, L, qw), lambda b, i: (b, order(i), C_MK // qw)),
                pl.BlockSpec((1, L, vw), lambda b, i: (b, order(i), C_MV // vw)),
                pl.BlockSpec((1, L, LANES), lambda b, i: (b, order(i), 0))]

    out = jax.ShapeDtypeStruct((B, Tt, vw), F32)
    return pl.pallas_call(
        functools.partial(_mlstm_kernel, kscale=MLSTM_DQK ** -0.5),
        grid=(B, ncl + ncc),
        in_specs=specs(fwd) + specs(bwd) + [pl.BlockSpec((1, LANES), lambda b, i: (0, 0))],
        out_specs=[pl.BlockSpec((1, L, vw), lambda b, i: (b, fwd(i), 0)),
                   pl.BlockSpec((1, L, vw), lambda b, i: (b, bwd(i), 0))],
        out_shape=[out, out],
        scratch_shapes=[pltpu.VMEM((2, MLSTM_HEADS, MLSTM_DQK, MLSTM_DV + LANES), F32),
                        pltpu.VMEM((2, MLSTM_HEADS, 1, 1), F32)],
        compiler_params=_params(("parallel", "arbitrary")),
    )(P, P, P, gates, P, P, P, gates, gate_bias)


def _merge_kernel(hf_ref, hb_ref, mo_ref, yg_ref, yn_ref, g0_ref, g1_ref, g2_ref, wb_ref, o_ref, ym_ref):
    @pl.when(pl.program_id(1) == 0)
    def _():
        h = hf_ref[...] + hb_ref[...]
        for hh in range(MLSTM_HEADS):
            cols = slice(hh * MLSTM_DV, (hh + 1) * MLSTM_DV)
            ym_ref[:, cols] = (_sigmoid(mo_ref[:, cols].astype(F32)) * _rms(h[:, cols])).astype(BF16)

    gate = lambda r: _sigmoid(r[...].astype(F32))
    acc = gate(g0_ref) * _dot(ym_ref[...], wb_ref[0])
    acc += gate(g1_ref) * _dot(yg_ref[...], wb_ref[1])
    acc += gate(g2_ref) * _dot(yn_ref[...], wb_ref[2])
    o_ref[...] = acc.astype(BF16)


def _merge(hf, hb, P2, yg, yn, wb):
    M, W = hf.shape
    D = wb.shape[2]
    tm = _pick(M, (512, 256))
    tn = _pick(D, (512, 256))
    row = lambda c: pl.BlockSpec((tm, W), lambda i, n: (i, c))
    gate = lambda br: pl.BlockSpec((tm, tn), lambda i, n: (i, (C_GATE + br * D) // tn + n))
    return pl.pallas_call(
        _merge_kernel, grid=(M // tm, D // tn),
        in_specs=[row(0), row(0), row(C_MO // W), row(0), row(0), gate(0), gate(1), gate(2),
                  pl.BlockSpec((N_BRANCHES, W, tn), lambda i, n: (0, 0, n))],
        out_specs=pl.BlockSpec((tm, tn), lambda i, n: (i, n)),
        out_shape=jax.ShapeDtypeStruct((M, D), BF16),
        scratch_shapes=[pltpu.VMEM((tm, W), BF16)],
        compiler_params=_params(("parallel", "arbitrary")),
    )(hf, hb, P2, yg, yn, P2, P2, P2, wb)


def _wout_post_kernel(a_ref, w_ref, x_ref, gp_ref, gate_ref, o_ref):
    y = _dot(a_ref[0], w_ref[...])
    o_ref[0] = x_ref[0] + gate_ref[0] * (_rms(y) * gp_ref[...])


def _wout_post(merged, w_out, xc, g_post, mod3, gate_chunk):
    B, Tt, D = xc.shape
    nT = Tt // TR
    tile = pl.BlockSpec((1, TR, D), lambda b, i: (b, i, 0))
    row = lambda b, i: jnp.where(i == nT - 1, B, b)
    return pl.pallas_call(
        _wout_post_kernel, grid=(B, nT),
        in_specs=[tile, pl.BlockSpec((D, D), lambda b, i: (0, 0)), tile,
                  pl.BlockSpec((1, D), lambda b, i: (0, 0)),
                  pl.BlockSpec((1, 1, D), lambda b, i: (row(b, i), 0, gate_chunk))],
        out_specs=tile,
        out_shape=jax.ShapeDtypeStruct((B, Tt, D), F32),
        compiler_params=_params(("parallel", "parallel"), 48),
    )(merged, w_out, xc, g_post.reshape(1, D), mod3)


def _route_segment(arow, acol, k, chunk):
    E, T = arow.shape
    nchunk = T // chunk
    ind = lambda m: jnp.where(m, 1.0, 0.0)
    bits = pltpu.bitcast(arow, jnp.int32)
    thr = jnp.zeros((E, 1), jnp.int32)
    for bit in range(30, -1, -1):
        cand = thr | (1 << bit)
        cnt = jnp.sum(ind(bits >= cand), axis=1, keepdims=True)
        thr = jnp.where(cnt >= k, cand, thr)
    thr_val = pltpu.bitcast(thr, F32)
    gt = arow > thr_val
    eq = arow == thr_val
    need = k - jnp.sum(ind(gt), axis=1, keepdims=True)

    r_i = lax.broadcasted_iota(jnp.int32, (chunk, chunk), 0)
    c_i = lax.broadcasted_iota(jnp.int32, (chunk, chunk), 1)
    before = ind(r_i < c_i).astype(BF16)
    after = ind(c_i < r_i).astype(BF16)

    def prefix_rows(mask):
        carry = jnp.zeros((E, 1), F32)
        parts, carries = [], [carry]
        for c in range(nchunk):
            mc = ind(mask[:, c * chunk:(c + 1) * chunk]).astype(BF16)
            parts.append(_dot(mc, before) + carry)
            carry = carry + jnp.sum(mc.astype(F32), axis=1, keepdims=True)
            carries.append(carry)
        return jnp.concatenate(parts, axis=1), carries

    tie_rank, _ = prefix_rows(eq)
    sel = gt | (eq & (tie_rank < need))
    pos, carries = prefix_rows(sel)
    prow = jnp.where(sel, pos, -1.0)
    lane = lax.broadcasted_iota(jnp.int32, (E, LANES), 1)
    starts = jnp.zeros((E, LANES), F32)
    for c, cv in enumerate(carries):
        starts = jnp.where(lane == c, cv, starts)

    sub = lax.broadcasted_iota(jnp.int32, (E, LANES), 0)
    diag = sub == lane
    thr_row = jnp.sum(jnp.where(diag, thr_val, 0.0), axis=0, keepdims=True)
    need_row = jnp.sum(jnp.where(diag, need, 0.0), axis=0, keepdims=True)
    lane_ok = lax.broadcasted_iota(jnp.int32, (1, LANES), 1) < E
    gt_c = (acol > thr_row) & lane_ok
    eq_c = (acol == thr_row) & lane_ok

    def prefix_cols(mask):
        carry = jnp.zeros((1, LANES), F32)
        parts = []
        for c in range(nchunk):
            mc = ind(mask[c * chunk:(c + 1) * chunk]).astype(BF16)
            parts.append(_dot(after, mc) + carry)
            carry = carry + jnp.sum(mc.astype(F32), axis=0, keepdims=True)
        return jnp.concatenate(parts, axis=0)

    sel_c = gt_c | (eq_c & (prefix_cols(eq_c) < need_row))
    pcol = jnp.where(sel_c, prefix_cols(sel_c), -1.0)
    return prow, pcol, starts


def _route_kernel(arow_ref, acol_ref, prow_ref, pcol_ref, starts_ref, *, T, Lc, cap_l, cap_c):
    prow, pcol, starts = _route_segment(arow_ref[0, :, :T], acol_ref[0, :T], cap_l, TR)
    prow_ref[0, :, :T] = prow
    pcol_ref[0, :T] = pcol
    starts_ref[0] = starts.astype(jnp.int32)
    prow, pcol, _ = _route_segment(arow_ref[0, :, T:], acol_ref[0, T:], cap_c, TR)
    prow_ref[0, :, T:] = prow
    pcol_ref[0, T:] = pcol


def _route(arow, acol, T, Lc, cap_l, cap_c):
    B, E, Tt = arow.shape
    return pl.pallas_call(
        functools.partial(_route_kernel, T=T, Lc=Lc, cap_l=cap_l, cap_c=cap_c),
        grid=(B,),
        in_specs=[pl.BlockSpec((1, E, Tt), lambda b: (b, 0, 0)), pl.BlockSpec((1, Tt, LANES), lambda b: (b, 0, 0))],
        out_specs=[pl.BlockSpec((1, E, Tt), lambda b: (b, 0, 0)), pl.BlockSpec((1, Tt, LANES), lambda b: (b, 0, 0)),
                   pl.BlockSpec((1, E, LANES), lambda b: (b, 0, 0))],
        out_shape=[jax.ShapeDtypeStruct((B, E, Tt), F32), jax.ShapeDtypeStruct((B, Tt, LANES), F32),
                   jax.ShapeDtypeStruct((B, E, LANES), jnp.int32)],
        compiler_params=_params(("parallel",), 48),
    )(arow, acol)


def _gather_kernel(st_ref, p_ref, h_ref, o_ref, acc_ref, *, chunk0, nchunk, cap, st):
    b = pl.program_id(0)
    e = pl.program_id(1)
    base = (b * N_EXPERTS + e) * (nchunk + 1)
    for S in range(cap // st):
        acc_ref[...] = jnp.zeros(acc_ref.shape, F32)
        slot = (lax.broadcasted_iota(jnp.int32, (st, TR), 0) + S * st).astype(F32)

        def body(c, carry):
            lo = st_ref[base + c]
            hi = st_ref[base + c + 1]

            @pl.when((lo < (S + 1) * st) & (hi > S * st))
            def _():
                onehot = (p_ref[0, pl.ds(chunk0 + c, 1), :] == slot).astype(BF16)
                r0 = pl.multiple_of(c * TR, TR)
                acc_ref[...] += _dot(onehot, h_ref[0, pl.ds(r0, TR), :])
            return carry

        lax.fori_loop(0, nchunk, body, 0)
        o_ref[0, S * st:(S + 1) * st, :] = acc_ref[...].astype(BF16)


def _gather(starts, prow3, hbf, xs, *, chunk0, nchunk, cap, slot_blk0):
    B, Tt, D = hbf.shape
    E = N_EXPERTS
    nT = Tt // TR
    st = min(cap, 128)
    rows = nchunk * TR
    in_specs = [pl.BlockSpec((1, nT, TR), lambda b, e, s: (b * E + e, 0, 0)),
                pl.BlockSpec((1, rows, D), lambda b, e, s: (b, chunk0 * TR // rows, 0))]
    args = [prow3, hbf]
    kern = functools.partial(_gather_kernel, chunk0=chunk0, nchunk=nchunk, cap=cap, st=st)
    aliases = {}
    if not isinstance(xs, jax.ShapeDtypeStruct):
        in_specs.append(pl.BlockSpec(memory_space=pl.ANY))
        args.append(xs)
        aliases = {3: 0}
        inner = kern
        kern = lambda s, p, h, _x, o, acc: inner(s, p, h, o, acc)
    grid_spec = pltpu.PrefetchScalarGridSpec(
        num_scalar_prefetch=1, grid=(B, E), in_specs=in_specs,
        out_specs=pl.BlockSpec((1, cap, D), lambda b, e, s: (e, slot_blk0 + b, 0)),
        scratch_shapes=[pltpu.VMEM((st, D), F32)])
    return pl.pallas_call(
        kern, grid_spec=grid_spec,
        out_shape=jax.ShapeDtypeStruct(xs.shape, BF16),
        input_output_aliases=aliases,
        compiler_params=_params(("parallel", "arbitrary"), 56),
    )(starts, *args)


def _ffn1_kernel(x_ref, wg_ref, wu_ref, o_ref, wgb_ref, wub_ref):
    @pl.when(pl.program_id(2) == 0)
    def _():
        wgb_ref[...] = wg_ref[0].astype(BF16)
        wub_ref[...] = wu_ref[0].astype(BF16)
    x = x_ref[0]
    g = _dot(x, wgb_ref[...])
    u = _dot(x, wub_ref[...])
    o_ref[0] = (g * _sigmoid(g) * u).astype(BF16)


def _ffn2_kernel(x_ref, w_ref, o_ref, wb_ref):
    @pl.when(pl.program_id(2) == 0)
    def _():
        wb_ref[...] = w_ref[0].astype(BF16)
    o_ref[0] = _dot(x_ref[0], wb_ref[...]).astype(BF16)


def _expert_matmul(kernel, x, ws, layer):
    E, S, K = x.shape
    N = ws[0].shape[3]
    ws = [w.reshape(-1, K, N) for w in ws]
    tm = _pick(S, (1088, 1024, 544, 512, 272, 256, 160, 128))
    tn = _pick(N, (512, 256))
    wspec = pl.BlockSpec((1, K, tn), lambda e, n, m: (layer * E + e, 0, n))
    return pl.pallas_call(
        kernel, grid=(E, N // tn, S // tm),
        in_specs=[pl.BlockSpec((1, tm, K), lambda e, n, m: (e, m, 0))] + [wspec] * len(ws),
        out_specs=pl.BlockSpec((1, tm, tn), lambda e, n, m: (e, m, n)),
        out_shape=jax.ShapeDtypeStruct((E, S, N), BF16),
        scratch_shapes=[pltpu.VMEM((K, tn), BF16)] * len(ws),
        compiler_params=_params(("parallel", "parallel", "arbitrary"), 48),
    )(x, *ws)


def _combine_kernel(st_ref, pc_ref, ac_ref, ye_ref, x_ref, gp_ref, gate_ref, o_ref, acc_ref,
                    *, nchunk, cpt, st):
    b = pl.program_id(0)
    t = pl.program_id(1)
    e = pl.program_id(2)
    rows = acc_ref.shape[0]

    @pl.when(e == 0)
    def _():
        acc_ref[...] = jnp.zeros(acc_ref.shape, F32)

    base = (b * N_EXPERTS + e) * (nchunk + 1) + t * cpt
    lo = st_ref[base]
    hi = st_ref[base + cpt]
    lane = lax.broadcasted_iota(jnp.int32, (rows, LANES), 1)
    mine = lane == e
    pos = jnp.sum(jnp.where(mine, pc_ref[0], 0.0), axis=1, keepdims=True)
    aff = jnp.sum(jnp.where(mine, ac_ref[0], 0.0), axis=1, keepdims=True)
    slot = lax.broadcasted_iota(jnp.int32, (rows, st), 1).astype(F32)

    def body(S, carry):
        onehot_t = (pos - (S * st).astype(F32) == slot).astype(BF16)
        r0 = pl.multiple_of(S * st, st)
        acc_ref[...] += aff * _dot(onehot_t, ye_ref[0, pl.ds(r0, st), :])
        return carry

    lax.fori_loop(lo // st, (hi + st - 1) // st, body, 0)

    @pl.when(e == N_EXPERTS - 1)
    def _():
        o_ref[0] = x_ref[0] + gate_ref[0] * (_rms(acc_ref[...]) * gp_ref[...])


def _combine(starts, pcol, acol, ye, xc, g_post, mod3, gate_chunk, *, rows, tile0, ntile, nchunk, cap,
             slot_blk0, mod_row_ctx, out):
    B, Tt, D = xc.shape
    E = N_EXPERTS
    st = min(cap, 128)
    cpt = rows // TR
    fresh = out is not None
    if not fresh:
        out = xc
    tok = lambda w: pl.BlockSpec((1, rows, w), lambda b, t, e, s: (b, tile0 + t, 0))
    mrow = (lambda b: B) if mod_row_ctx else (lambda b: b)
    in_specs = [tok(LANES), tok(LANES),
                pl.BlockSpec((1, cap, D), lambda b, t, e, s: (e, slot_blk0 + b, 0)),
                tok(D), pl.BlockSpec((1, D), lambda b, t, e, s: (0, 0)),
                pl.BlockSpec((1, 1, D), lambda b, t, e, s: (mrow(b), 0, gate_chunk))]
    args = [pcol, acol, ye, xc, g_post.reshape(1, D), mod3]
    kern = functools.partial(_combine_kernel, nchunk=nchunk, cpt=cpt, st=st)
    if fresh:
        out_spec = pl.BlockSpec((1, rows, D), lambda b, t, e, s: (b, t, 0))
        aliases = {}
    else:
        out_spec = tok(D)
        aliases = {4: 0}
    grid_spec = pltpu.PrefetchScalarGridSpec(
        num_scalar_prefetch=1, grid=(B, ntile, E), in_specs=in_specs, out_specs=out_spec,
        scratch_shapes=[pltpu.VMEM((rows, D), F32)])
    return pl.pallas_call(
        kern, grid_spec=grid_spec,
        out_shape=jax.ShapeDtypeStruct(out.shape, F32),
        input_output_aliases=aliases,
        compiler_params=_params(("parallel", "parallel", "arbitrary"), 48),
    )(starts, *args)


def _rope_tables(T, Lc):
    nf = HEAD_DIM // 4
    t = np.arange(T)
    inv = ROPE_THETA ** (-jnp.arange(nf, dtype=F32) / nf)
    row = jnp.asarray(t // GRID_W, F32)
    col = jnp.asarray(t % GRID_W, F32)
    ang = jnp.stack([row[:, None] * inv, col[:, None] * inv], axis=1)
    cos = jnp.broadcast_to(jnp.cos(ang)[:, :, None, :], (T, 2, 2, nf)).reshape(T, HEAD_DIM)
    sin = jnp.broadcast_to(jnp.sin(ang)[:, :, None, :], (T, 2, 2, nf)).reshape(T, HEAD_DIM)
    sign = jnp.asarray(np.tile(np.repeat([-1.0, 1.0], nf), 2), F32)
    cos = jnp.concatenate([cos, jnp.ones((Lc, HEAD_DIM), F32)], axis=0)
    sin = jnp.concatenate([sin * sign, jnp.zeros((Lc, HEAD_DIM), F32)], axis=0)
    return cos, sin


def _split_bf16(w):
    hi = w.astype(BF16)
    return hi, (w - hi.astype(F32)).astype(BF16)


def kernel(x, c, ctx, c_ctx, w_mod, b_mod, g_pre_mix, g_post_mix, g_pre_ffn, g_post_ffn, w_in, mlstm_gate_bias, gqa_q_norm, gqa_k_norm, nat_rel_bias, w_branch, w_out, w_router, w_expert_gate, w_expert_up, w_expert_down):
    B, T, D = x.shape
    Lc = ctx.shape[1]
    depth = w_mod.shape[0]
    E = N_EXPERTS
    assert Lc == TR and T % (2 * TR) == 0 and w_router.shape[2] == E
    Tt = T + Lc
    nT = Tt // TR
    M = B * Tt
    cap_l = EC_CAPACITY_FACTOR * T // E
    cap_c = EC_CAPACITY_FACTOR * Lc // E
    scale = HEAD_DIM ** -0.5
    nchunk = T // TR

    xc = jnp.concatenate([x, ctx], axis=1)
    rows_c = -(-(B + 1) // 8) * 8
    cc = jnp.zeros((rows_c, D), F32).at[:B].set(c).at[B].set(c_ctx)
    mod = _modulation(cc, w_mod, b_mod)
    cos_t, sin_t = _rope_tables(T, Lc)

    for l in range(depth):
        last = l == depth - 1
        mod3 = mod[l].reshape(rows_c, 1, N_MOD * D)
        w_l = w_in[l]
        w_main = jnp.concatenate([w_l[:, :W_IN_GATE_COL], w_l[:, W_IN_GATE_COL + N_GATE_COLS:]], axis=1).astype(BF16)
        w_g = jnp.pad(w_l[:, W_IN_GATE_COL:W_IN_GATE_COL + N_GATE_COLS], ((0, 0), (0, LANES - N_GATE_COLS))).astype(BF16)
        gate_bias = jnp.pad(mlstm_gate_bias[l], (0, LANES - N_GATE_COLS)).reshape(1, LANES)
        nw = jnp.concatenate([jnp.tile(gqa_q_norm[l][None] * scale, (GQA_HEADS, 1)),
                              jnp.tile(gqa_k_norm[l][None], (GQA_KV_HEADS, 1))], axis=0)[:, None, :]

        hx = _norm_mod(xc, g_pre_mix[l], mod3, 0, 1)
        hx2 = hx.reshape(M, D)
        P2 = _matmul(hx2, w_main, BF16)
        P = P2.reshape(B, Tt, -1)
        gates = _matmul(hx2, w_g, F32).reshape(B, Tt, LANES)

        hf, hb = _mlstm(P, gates, gate_bias, T, Lc)

        qk = _qk_prep(P, nw, cos_t, sin_t)
        yg = _flash(qk, 0, qk, GQA_HEADS * HEAD_DIM, P, C_GV, G=GQA_KV_HEADS, R=GQA_HEADS // GQA_KV_HEADS,
                    T=T, Lc=Lc, tile0=0, nq=nT, kv_all=True, scale=1.0)

        bias_tab = _nat_bias_tables(nat_rel_bias[l], T)
        yn = _nat(P, bias_tab, T, Lc, scale)
        yn = _flash(P, C_NQ, P, C_NK, P, C_NV, G=NAT_HEADS, R=1, T=T, Lc=Lc, tile0=nT - 1, nq=1,
                    kv_all=False, scale=scale, out=yn)

        merged = _merge(hf.reshape(M, -1), hb.reshape(M, -1), P2, yg.reshape(M, -1), yn.reshape(M, -1),
                        w_branch[l].astype(BF16))
        xc = _wout_post(merged.reshape(B, Tt, D), w_out[l].astype(BF16), xc, g_post_mix[l], mod3, 2)

        wr = jnp.pad(w_router[l], ((0, 0), (0, LANES - E)))
        hx, acol, arow = _norm_mod(xc, g_pre_ffn[l], mod3, 3, 4, router=_split_bf16(wr))
        prow, pcol, starts = _route(arow, acol, T, Lc, cap_l, cap_c)
        starts_flat = starts[:, :, :nchunk + 1].reshape(-1)
        prow3 = prow.reshape(B * E, nT, TR)

        S = B * cap_l + (0 if last else B * cap_c)
        xs = _gather(starts_flat, prow3, hx, jax.ShapeDtypeStruct((E, S, D), BF16),
                     chunk0=0, nchunk=nchunk, cap=cap_l, slot_blk0=0)
        ctx_starts = jnp.tile(jnp.asarray([0, cap_c], jnp.int32), B * E)
        if not last:
            xs = _gather(ctx_starts, prow3, hx, xs, chunk0=nchunk, nchunk=1, cap=cap_c,
                         slot_blk0=B * cap_l // cap_c)
        hid = _expert_matmul(_ffn1_kernel, xs, (w_expert_gate, w_expert_up), l)
        ye = _expert_matmul(_ffn2_kernel, hid, (w_expert_down,), l)

        rows = _pick(T, (512, 256))
        if last:
            return _combine(starts_flat, pcol, acol, ye, xc, g_post_ffn[l], mod3, 5, rows=rows, tile0=0,
                            ntile=T // rows, nchunk=nchunk, cap=cap_l, slot_blk0=0, mod_row_ctx=False,
                            out=jax.ShapeDtypeStruct((B, T, D), F32))
        xc = _combine(starts_flat, pcol, acol, ye, xc, g_post_ffn[l], mod3, 5, rows=rows, tile0=0,
                      ntile=T // rows, nchunk=nchunk, cap=cap_l, slot_blk0=0, mod_row_ctx=False, out=None)
        xc = _combine(ctx_starts, pcol, acol, ye, xc, g_post_ffn[l], mod3, 5, rows=TR, tile0=nT - 1,
                      ntile=1, nchunk=1, cap=cap_c, slot_blk0=B * cap_l // cap_c, mod_row_ctx=True, out=None)
```

```python
import functools

import numpy as np
import jax
import jax.numpy as jnp
from jax import lax
from jax.experimental import pallas as pl
from jax.experimental.pallas import tpu as pltpu

F32 = jnp.float32
BF16 = jnp.bfloat16

LANES = 128
HEAD_DIM = 128
MLSTM_HEADS = 4
MLSTM_DQK = 128
MLSTM_DV = 256
MLSTM_CHUNK = 128
GQA_HEADS = 8
GQA_KV_HEADS = 2
NAT_HEADS = 8
NAT_WIN_ROWS = 8
NAT_WIN_COLS = 16
GRID_W = 64
Q_BLOCK = 128
N_EXPERTS = 16
EC_CAPACITY_FACTOR = 2
ROPE_THETA = 10000.0
NORM_EPS = 1e-6
NEG_INF = -1e30
N_MOD = 6
N_BRANCHES = 3
BRANCH_WIDTH = 1024
TR = 256
N_GATE_COLS = 4 * MLSTM_HEADS

C_MQ = 0
C_MK = C_MQ + MLSTM_HEADS * MLSTM_DQK
C_MV = C_MK + MLSTM_HEADS * MLSTM_DQK
C_MO = C_MV + MLSTM_HEADS * MLSTM_DV
C_GQ = C_MO + MLSTM_HEADS * MLSTM_DV
C_GK = C_GQ + GQA_HEADS * HEAD_DIM
C_GV = C_GK + GQA_KV_HEADS * HEAD_DIM
C_NQ = C_GV + GQA_KV_HEADS * HEAD_DIM
C_NK = C_NQ + NAT_HEADS * HEAD_DIM
C_NV = C_NK + NAT_HEADS * HEAD_DIM
C_GATE = C_NV + NAT_HEADS * HEAD_DIM
W_IN_GATE_COL = C_GQ


def _params(sem, vmem_mb=None):
    return pltpu.CompilerParams(
        dimension_semantics=sem,
        vmem_limit_bytes=None if vmem_mb is None else vmem_mb * 2 ** 20)


def _pick(n, cands):
    for c in cands:
        if n % c == 0:
            return c
    raise ValueError(f"no tile for {n} in {cands}")


def _dot(a, b):
    return jnp.dot(a, b, preferred_element_type=F32)


def _dot_nt(a, b):
    return lax.dot_general(a, b, (((1,), (1,)), ((), ())), preferred_element_type=F32)


def _sigmoid(x):
    return 0.5 * jnp.tanh(0.5 * x) + 0.5


def _rms(x):
    return x * lax.rsqrt(jnp.mean(x * x, axis=-1, keepdims=True) + NORM_EPS)


def _mod_kernel(c_ref, w_ref, b_ref, o_ref):
    c = c_ref[...]
    a = (c * _sigmoid(c)).astype(BF16)
    o_ref[0] = _dot(a, w_ref[0].astype(BF16)) + b_ref[0]


def _modulation(cc, w_mod, b_mod):
    L, D, N = w_mod.shape
    tn = _pick(N, (768, 512, 256, 128))
    return pl.pallas_call(
        _mod_kernel,
        grid=(L, N // tn),
        in_specs=[pl.BlockSpec((cc.shape[0], D), lambda l, n: (0, 0)),
                  pl.BlockSpec((1, D, tn), lambda l, n: (l, 0, n)),
                  pl.BlockSpec((1, 1, tn), lambda l, n: (l, 0, n))],
        out_specs=pl.BlockSpec((1, cc.shape[0], tn), lambda l, n: (l, 0, n)),
        out_shape=jax.ShapeDtypeStruct((L, cc.shape[0], N), F32),
        compiler_params=_params(("parallel", "parallel")),
    )(cc, w_mod, b_mod.reshape(L, 1, N))


def _norm_mod_kernel(x_ref, g_ref, sh_ref, sc_ref, o_ref):
    y = _rms(x_ref[0]) * g_ref[...]
    o_ref[0] = (y * (1.0 + sc_ref[0]) + sh_ref[0]).astype(BF16)


def _norm_mod_router_kernel(x_ref, g_ref, sh_ref, sc_ref, whi_ref, wlo_ref, o_ref, acol_ref, arow_ref):
    y = _rms(x_ref[0]) * g_ref[...]
    h = y * (1.0 + sc_ref[0]) + sh_ref[0]
    hb = h.astype(BF16)
    o_ref[0] = hb
    hlo = (h - hb.astype(F32)).astype(BF16)
    logits = _dot(hb, whi_ref[...]) + _dot(hlo, whi_ref[...]) + _dot(hb, wlo_ref[...])
    lane = lax.broadcasted_iota(jnp.int32, logits.shape, 1)
    valid = lane < N_EXPERTS
    logits = jnp.where(valid, logits, NEG_INF)
    ex = jnp.where(valid, jnp.exp(logits - jnp.max(logits, axis=1, keepdims=True)), 0.0)
    aff = ex / jnp.sum(ex, axis=1, keepdims=True)
    acol_ref[0] = aff
    arow_ref[0] = aff.T[:N_EXPERTS]


def _mod_specs(B, nT, D, shift_chunk, scale_chunk):
    row = lambda b, i: jnp.where(i == nT - 1, B, b)
    return [pl.BlockSpec((1, 1, D), lambda b, i: (row(b, i), 0, shift_chunk)),
            pl.BlockSpec((1, 1, D), lambda b, i: (row(b, i), 0, scale_chunk))]


def _norm_mod(xc, g, mod3, shift_chunk, scale_chunk, router=None):
    B, Tt, D = xc.shape
    nT = Tt // TR
    tile = pl.BlockSpec((1, TR, D), lambda b, i: (b, i, 0))
    in_specs = [tile, pl.BlockSpec((1, D), lambda b, i: (0, 0))] + _mod_specs(B, nT, D, shift_chunk, scale_chunk)
    args = [xc, g.reshape(1, D), mod3, mod3]
    if router is None:
        return pl.pallas_call(
            _norm_mod_kernel, grid=(B, nT), in_specs=in_specs, out_specs=tile,
            out_shape=jax.ShapeDtypeStruct((B, Tt, D), BF16),
            compiler_params=_params(("parallel", "parallel")),
        )(*args)
    whi, wlo = router
    wspec = pl.BlockSpec((D, LANES), lambda b, i: (0, 0))
    return pl.pallas_call(
        _norm_mod_router_kernel, grid=(B, nT), in_specs=in_specs + [wspec, wspec],
        out_specs=[tile, pl.BlockSpec((1, TR, LANES), lambda b, i: (b, i, 0)),
                   pl.BlockSpec((1, N_EXPERTS, TR), lambda b, i: (b, 0, i))],
        out_shape=[jax.ShapeDtypeStruct((B, Tt, D), BF16),
                   jax.ShapeDtypeStruct((B, Tt, LANES), F32),
                   jax.ShapeDtypeStruct((B, N_EXPERTS, Tt), F32)],
        compiler_params=_params(("parallel", "parallel")),
    )(*args, whi, wlo)


def _mm_kernel(a_ref, w_ref, o_ref):
    o_ref[...] = _dot(a_ref[...], w_ref[...]).astype(o_ref.dtype)


def _matmul(a, w, out_dtype):
    M, K = a.shape
    N = w.shape[1]
    tm = _pick(M, (1024, 512, 256))
    tn = _pick(N, (512, 256, 128))
    return pl.pallas_call(
        _mm_kernel, grid=(M // tm, N // tn),
        in_specs=[pl.BlockSpec((tm, K), lambda i, j: (i, 0)), pl.BlockSpec((K, tn), lambda i, j: (0, j))],
        out_specs=pl.BlockSpec((tm, tn), lambda i, j: (i, j)),
        out_shape=jax.ShapeDtypeStruct((M, N), out_dtype),
        compiler_params=_params(("parallel", "parallel")),
    )(a, w)


def _qk_prep_kernel(q_ref, k_ref, nw_ref, cos_ref, sin_ref, o_ref):
    lane = lax.broadcasted_iota(jnp.int32, (q_ref.shape[1], HEAD_DIM), 1)
    first = (lane % (HEAD_DIM // 2)) < HEAD_DIM // 4
    cos = cos_ref[...]
    sin = sin_ref[...]
    for h in range(o_ref.shape[2] // HEAD_DIM):
        src, hh = (q_ref, h) if h < GQA_HEADS else (k_ref, h - GQA_HEADS)
        y = _rms(src[0, :, hh * HEAD_DIM:(hh + 1) * HEAD_DIM].astype(F32)) * nw_ref[:, h * HEAD_DIM:(h + 1) * HEAD_DIM]
        partner = jnp.where(first, pltpu.roll(y, HEAD_DIM - HEAD_DIM // 4, 1), pltpu.roll(y, HEAD_DIM // 4, 1))
        o_ref[0, :, h * HEAD_DIM:(h + 1) * HEAD_DIM] = (y * cos + partner * sin).astype(BF16)


def _qk_prep(P, nw, cos_t, sin_t):
    B, Tt, _ = P.shape
    qw, kw = GQA_HEADS * HEAD_DIM, GQA_KV_HEADS * HEAD_DIM
    w = qw + kw
    tr = _pick(Tt, (544, TR))
    return pl.pallas_call(
        _qk_prep_kernel, grid=(B, Tt // tr),
        in_specs=[pl.BlockSpec((1, tr, qw), lambda b, i: (b, i, C_GQ // qw)),
                  pl.BlockSpec((1, tr, kw), lambda b, i: (b, i, C_GK // kw)),
                  pl.BlockSpec((1, w), lambda b, i: (0, 0)),
                  pl.BlockSpec((tr, HEAD_DIM), lambda b, i: (i, 0)),
                  pl.BlockSpec((tr, HEAD_DIM), lambda b, i: (i, 0))],
        out_specs=pl.BlockSpec((1, tr, w), lambda b, i: (b, i, 0)),
        out_shape=jax.ShapeDtypeStruct((B, Tt, w), BF16),
        compiler_params=_params(("parallel", "parallel")),
    )(P, P, nw.reshape(1, w), cos_t, sin_t)


def _flash_kernel(q_ref, k_ref, v_ref, o_ref, m_ref, acc_ref, *, R, Lc, ck, n_lat_tiles):
    tile = pl.program_id(2)

    m_ref[...] = jnp.full(m_ref.shape, NEG_INF, F32)
    acc_ref[...] = jnp.zeros(acc_ref.shape, F32)

    def step(k, v):
        vaug = jnp.concatenate([v, jnp.ones(v.shape, BF16)], axis=1)
        for r in range(R):
            rows = slice(r * TR, (r + 1) * TR)
            s = _dot_nt(q_ref[0, :, r * HEAD_DIM:(r + 1) * HEAD_DIM], k)
            slabs = [s[:, c * LANES:(c + 1) * LANES] for c in range(s.shape[1] // LANES)]
            m_old = m_ref[rows, :]
            m_new = jnp.maximum(m_old, jnp.max(functools.reduce(jnp.maximum, slabs), axis=1, keepdims=True))
            alpha = jnp.exp(m_old - m_new)
            p = jnp.concatenate([jnp.exp(sl - m_new) for sl in slabs], axis=1).astype(BF16)
            acc_ref[rows, :] = jnp.concatenate([alpha, alpha], axis=1) * acc_ref[rows, :] + _dot(p, vaug)
            m_ref[rows, :] = m_new

    kv_rows = k_ref.shape[1]

    @pl.when(tile >= n_lat_tiles)
    def _():
        step(k_ref[0, kv_rows - Lc:, :], v_ref[0, kv_rows - Lc:, :])

    @pl.when(tile < n_lat_tiles)
    def _():
        def body(j, carry):
            r0 = pl.multiple_of(j * ck, ck)
            step(k_ref[0, pl.ds(r0, ck), :], v_ref[0, pl.ds(r0, ck), :])
            return carry
        lax.fori_loop(0, kv_rows // ck, body, 0)

    o = acc_ref[:, :HEAD_DIM] / acc_ref[:, HEAD_DIM:]
    for r in range(R):
        o_ref[0, :, r * HEAD_DIM:(r + 1) * HEAD_DIM] = o[r * TR:(r + 1) * TR].astype(o_ref.dtype)


def _flash(q_arr, q_col, k_arr, k_col, v_arr, v_col, *, G, R, T, Lc):
    B, Tt, _ = q_arr.shape
    qw = R * HEAD_DIM
    ck = max(c for c in range(LANES, 2304 + 1, LANES) if Tt % c == 0)
    return pl.pallas_call(
        functools.partial(_flash_kernel, R=R, Lc=Lc, ck=ck, n_lat_tiles=T // TR),
        grid=(B, G, Tt // TR),
        in_specs=[pl.BlockSpec((1, TR, qw), lambda b, g, i: (b, i, q_col // qw + g)),
                  pl.BlockSpec((1, Tt, HEAD_DIM), lambda b, g, i: (b, 0, k_col // HEAD_DIM + g)),
                  pl.BlockSpec((1, Tt, HEAD_DIM), lambda b, g, i: (b, 0, v_col // HEAD_DIM + g))],
        out_specs=pl.BlockSpec((1, TR, qw), lambda b, g, i: (b, i, g)),
        out_shape=jax.ShapeDtypeStruct((B, Tt, G * qw), BF16),
        scratch_shapes=[pltpu.VMEM((R * TR, LANES), F32), pltpu.VMEM((R * TR, 2 * HEAD_DIM), F32)],
        compiler_params=_params(("parallel", "parallel", "arbitrary"), 48),
    )(q_arr, k_arr, v_arr)


def _nat_geometry(T):
    rows = T // GRID_W
    kh = min(NAT_WIN_ROWS, rows)
    nkr = min(kh + 1, rows)
    qbr = Q_BLOCK // GRID_W
    nbr = rows // qbr
    kb = np.zeros(nbr, np.int32)
    var = np.zeros(nbr, np.int32)
    variants = []
    for j in range(nbr):
        qrow = j * qbr + np.arange(qbr)
        start_r = np.clip(qrow - kh // 2, 0, rows - kh)
        kb[j] = min(start_r[0], rows - nkr)
        sig = (int(kb[j] - j * qbr), tuple(int(s - kb[j]) for s in start_r))
        if sig not in variants:
            variants.append(sig)
        var[j] = variants.index(sig)
    return kh, nkr, qbr, nbr, kb, var, variants


def _nat_bias_kernel(rb_ref, o_ref, *, variants, kh, nkr, qbr):
    h = pl.program_id(0)
    ndr = 2 * NAT_WIN_ROWS - 1
    ndc = 2 * NAT_WIN_COLS - 1
    qc = lax.broadcasted_iota(jnp.int32, (GRID_W, GRID_W), 0)
    kc = lax.broadcasted_iota(jnp.int32, (GRID_W, GRID_W), 1)
    dc = kc - qc + NAT_WIN_COLS - 1
    start_c = jnp.clip(qc - NAT_WIN_COLS // 2, 0, GRID_W - NAT_WIN_COLS)
    col_in = (kc >= start_c) & (kc < start_c + NAT_WIN_COLS)
    masked = jnp.full((GRID_W, GRID_W), NEG_INF, F32)
    toeplitz = []
    for dr in range(ndr):
        t = jnp.zeros((GRID_W, GRID_W), F32)
        for d in range(ndc):
            t = jnp.where(dc == d, rb_ref[(h * ndr + dr) * ndc + d], t)
        toeplitz.append(jnp.where(col_in, t, NEG_INF))
    for v, (delta, srel) in enumerate(variants):
        for qr in range(qbr):
            blocks = []
            for kr in range(nkr):
                if srel[qr] <= kr < srel[qr] + kh:
                    blocks.append(toeplitz[int(np.clip(kr + delta - qr + NAT_WIN_ROWS - 1, 0, ndr - 1))])
                else:
                    blocks.append(masked)
            rows = slice(qr * GRID_W, (qr + 1) * GRID_W)
            for kr in range(0, nkr - 1, 2):
                o_ref[v, 0, rows, kr * GRID_W:(kr + 2) * GRID_W] = jnp.concatenate(blocks[kr:kr + 2], axis=1)
            if nkr % 2:
                o_ref[v, 0, rows, (nkr - 1) * GRID_W:] = blocks[-1]


def _nat_bias_tables(rel_bias, T):
    kh, nkr, qbr, _, _, _, variants = _nat_geometry(T)
    V = len(variants)
    nk = nkr * GRID_W
    return pl.pallas_call(
        functools.partial(_nat_bias_kernel, variants=variants, kh=kh, nkr=nkr, qbr=qbr),
        grid=(NAT_HEADS,),
        in_specs=[pl.BlockSpec(memory_space=pltpu.SMEM)],
        out_specs=pl.BlockSpec((V, 1, Q_BLOCK, nk), lambda h: (0, h, 0, 0)),
        out_shape=jax.ShapeDtypeStruct((V, NAT_HEADS, Q_BLOCK, nk), F32),
        compiler_params=_params(("parallel",)),
    )(rel_bias.reshape(-1))


def _nat_kernel(kb_ref, var_ref, q_ref, k_ref, v_ref, bias_ref, o_ref, *, T, nbr, nk, scale):
    kc = k_ref[0, T:, :]
    vc = v_ref[0, T:, :]

    def body(j, carry):
        q0 = pl.multiple_of(j * Q_BLOCK, Q_BLOCK)
        q = q_ref[0, pl.ds(q0, Q_BLOCK), :]
        r0 = pl.multiple_of(kb_ref[j] * GRID_W, GRID_W)
        s1 = _dot_nt(q, k_ref[0, pl.ds(r0, nk), :]) * scale + bias_ref[var_ref[j], 0]
        s2 = _dot_nt(q, kc) * scale
        m = jnp.maximum(jnp.max(s1, axis=1, keepdims=True), jnp.max(s2, axis=1, keepdims=True))
        p1 = jnp.exp(s1 - m)
        p2 = jnp.exp(s2 - m)
        l = jnp.sum(p1, axis=1, keepdims=True) + jnp.sum(p2, axis=1, keepdims=True)
        o = _dot(p1.astype(BF16), v_ref[0, pl.ds(r0, nk), :]) + _dot(p2.astype(BF16), vc)
        o_ref[0, pl.ds(q0, Q_BLOCK), :] = (o / l).astype(o_ref.dtype)
        return carry

    lax.fori_loop(0, nbr, body, 0, unroll=4)

    s = _dot_nt(q_ref[0, T:, :], kc) * scale
    p = jnp.exp(s - jnp.max(s, axis=1, keepdims=True))
    o_ref[0, T:, :] = (_dot(p.astype(BF16), vc) / jnp.sum(p, axis=1, keepdims=True)).astype(o_ref.dtype)


def _nat(P, bias_tab, T, Lc, scale):
    B, Tt, _ = P.shape
    _, nkr, _, nbr, kb, var, _ = _nat_geometry(T)
    nk = nkr * GRID_W
    V = bias_tab.shape[0]
    grid_spec = pltpu.PrefetchScalarGridSpec(
        num_scalar_prefetch=2, grid=(B, NAT_HEADS),
        in_specs=[pl.BlockSpec((1, Tt, HEAD_DIM), lambda b, h, kb_, vr: (b, 0, C_NQ // HEAD_DIM + h)),
                  pl.BlockSpec((1, Tt, HEAD_DIM), lambda b, h, kb_, vr: (b, 0, C_NK // HEAD_DIM + h)),
                  pl.BlockSpec((1, Tt, HEAD_DIM), lambda b, h, kb_, vr: (b, 0, C_NV // HEAD_DIM + h)),
                  pl.BlockSpec((V, 1, Q_BLOCK, nk), lambda b, h, kb_, vr: (0, h, 0, 0))],
        out_specs=pl.BlockSpec((1, Tt, HEAD_DIM), lambda b, h, kb_, vr: (b, 0, h)))
    return pl.pallas_call(
        functools.partial(_nat_kernel, T=T, nbr=nbr, nk=nk, scale=scale),
        grid_spec=grid_spec,
        out_shape=jax.ShapeDtypeStruct((B, Tt, NAT_HEADS * HEAD_DIM), BF16),
        compiler_params=_params(("parallel", "parallel")),
    )(jnp.asarray(kb), jnp.asarray(var), P, P, P, bias_tab)


def _mlstm_kernel(qf_ref, kf_ref, vf_ref, gf_ref, qb_ref, kb_ref, vb_ref, gb_ref, bias_ref,
                  hf_ref, hb_ref, st_ref, m_ref, *, kscale):
    L = MLSTM_CHUNK
    dv = MLSTM_DV

    @pl.when(pl.program_id(1) == 0)
    def _():
        st_ref[...] = jnp.zeros(st_ref.shape, F32)
        m_ref[...] = jnp.full(m_ref.shape, NEG_INF, F32)

    ti = lax.broadcasted_iota(jnp.int32, (L, L), 0)
    si = lax.broadcasted_iota(jnp.int32, (L, L), 1)
    ones_col = (lax.broadcasted_iota(jnp.int32, (L, LANES), 1) == 0).astype(BF16)

    for dirn, (q_ref, k_ref, v_ref, g_ref, h_ref) in enumerate(
            ((qf_ref, kf_ref, vf_ref, gf_ref, hf_ref), (qb_ref, kb_ref, vb_ref, gb_ref, hb_ref))):
        tri = (si <= ti) if dirn == 0 else (si >= ti)
        G = g_ref[0] + bias_ref[...]
        LF = jnp.minimum(G, 0.0) - jnp.log(1.0 + jnp.exp(-jnp.abs(G)))
        lf_hi = LF.astype(BF16)
        lf_lo = (LF - lf_hi.astype(F32)).astype(BF16)
        trib = tri.astype(BF16)
        Bc = _dot(trib, lf_hi) + _dot(trib, lf_lo)
        BcT = Bc.T
        GT = G.T
        bl_row = Bc[L - 1:L, :] if dirn == 0 else Bc[0:1, :]
        for h in range(MLSTM_HEADS):
            ci = dirn * 2 * MLSTM_HEADS + h
            cf = ci + MLSTM_HEADS
            bcol = Bc[:, cf:cf + 1]
            brow = BcT[cf:cf + 1, :]
            licol = G[:, ci:ci + 1]
            lirow = GT[ci:ci + 1, :]
            bl = bl_row[:, cf:cf + 1]
            m_old = m_ref[dirn, h]
            gcol = bl - bcol + licol
            m_new = jnp.maximum(bl + m_old, jnp.max(gcol, axis=0, keepdims=True))
            decay = jnp.exp(bl + m_old - m_new)
            wk = jnp.exp(gcol - m_new)
            dmat = jnp.where(tri, bcol - brow + lirow, NEG_INF)
            inter = bcol + m_old
            m_t = jnp.maximum(inter, jnp.max(dmat, axis=1, keepdims=True))
            w = jnp.exp(dmat - m_t)
            a = jnp.exp(inter - m_t)
            qh = q_ref[0, :, h * MLSTM_DQK:(h + 1) * MLSTM_DQK]
            kh = k_ref[0, :, h * MLSTM_DQK:(h + 1) * MLSTM_DQK]
            vaug = jnp.concatenate([v_ref[0, :, h * dv:(h + 1) * dv], ones_col], axis=1)
            smat = _dot_nt(qh, kh) * kscale * w
            state = st_ref[dirn, h]
            nd = _dot(smat.astype(BF16), vaug) + a * _dot(qh, state.astype(BF16))
            den = jnp.maximum(jnp.abs(nd[:, dv:dv + 1]), jnp.exp(-m_t))
            h_ref[0, :, h * dv:(h + 1) * dv] = nd[:, :dv] / den
            kT = kh.astype(F32).T.astype(BF16)
            upd = _dot(kT, (wk * vaug.astype(F32)).astype(BF16)) * kscale
            st_ref[dirn, h] = decay * state + upd
            m_ref[dirn, h] = m_new


def _mlstm(P, gates, gate_bias, T, Lc):
    B, Tt, _ = P.shape
    L = MLSTM_CHUNK
    ncl, ncc = T // L, Lc // L
    fwd = lambda i: jnp.where(i < ncc, ncl + i, i - ncc)
    bwd = lambda i: jnp.where(i < ncc, ncl + ncc - 1 - i, ncl - 1 - (i - ncc))
    qw = MLSTM_HEADS * MLSTM_DQK
    vw = MLSTM_HEADS * MLSTM_DV

    def specs(order):
        return [pl.BlockSpec((1, L, qw), lambda b, i: (b, order(i), C_MQ // qw)),
                pl.BlockSpec((1, L, qw), lambda b, i: (b, order(i), C_MK // qw)),
                pl.BlockSpec((1, L, vw), lambda b, i: (b, order(i), C_MV // vw)),
                pl.BlockSpec((1, L, LANES), lambda b, i: (b, order(i), 0))]

    out = jax.ShapeDtypeStruct((B, Tt, vw), F32)
    return pl.pallas_call(
        functools.partial(_mlstm_kernel, kscale=MLSTM_DQK ** -0.5),
        grid=(B, ncl + ncc),
        in_specs=specs(fwd) + specs(bwd) + [pl.BlockSpec((1, LANES), lambda b, i: (0, 0))],
        out_specs=[pl.BlockSpec((1, L, vw), lambda b, i: (b, fwd(i), 0)),
                   pl.BlockSpec((1, L, vw), lambda b, i: (b, bwd(i), 0))],
        out_shape=[out, out],
        scratch_shapes=[pltpu.VMEM((2, MLSTM_HEADS, MLSTM_DQK, MLSTM_DV + LANES), F32),
                        pltpu.VMEM((2, MLSTM_HEADS, 1, 1), F32)],
        compiler_params=_params(("parallel", "arbitrary")),
    )(P, P, P, gates, P, P, P, gates, gate_bias)


def _merge_kernel(hf_ref, hb_ref, mo_ref, yg_ref, yn_ref, g0_ref, g1_ref, g2_ref, wb_ref, o_ref, ym_ref):
    @pl.when(pl.program_id(1) == 0)
    def _():
        h = hf_ref[...] + hb_ref[...]
        for hh in range(MLSTM_HEADS):
            cols = slice(hh * MLSTM_DV, (hh + 1) * MLSTM_DV)
            ym_ref[:, cols] = (_sigmoid(mo_ref[:, cols].astype(F32)) * _rms(h[:, cols])).astype(BF16)

    gate = lambda r: _sigmoid(r[...].astype(F32))
    acc = gate(g0_ref) * _dot(ym_ref[...], wb_ref[0])
    acc += gate(g1_ref) * _dot(yg_ref[...], wb_ref[1])
    acc += gate(g2_ref) * _dot(yn_ref[...], wb_ref[2])
    o_ref[...] = acc.astype(BF16)


def _merge(hf, hb, P2, yg, yn, wb):
    M, W = hf.shape
    D = wb.shape[2]
    tm = _pick(M, (512, 256))
    tn = _pick(D, (512, 256))
    row = lambda c: pl.BlockSpec((tm, W), lambda i, n: (i, c))
    gate = lambda br: pl.BlockSpec((tm, tn), lambda i, n: (i, (C_GATE + br * D) // tn + n))
    return pl.pallas_call(
        _merge_kernel, grid=(M // tm, D // tn),
        in_specs=[row(0), row(0), row(C_MO // W), row(0), row(0), gate(0), gate(1), gate(2),
                  pl.BlockSpec((N_BRANCHES, W, tn), lambda i, n: (0, 0, n))],
        out_specs=pl.BlockSpec((tm, tn), lambda i, n: (i, n)),
        out_shape=jax.ShapeDtypeStruct((M, D), BF16),
        scratch_shapes=[pltpu.VMEM((tm, W), BF16)],
        compiler_params=_params(("parallel", "arbitrary")),
    )(hf, hb, P2, yg, yn, P2, P2, P2, wb)


def _wout_post_kernel(a_ref, w_ref, x_ref, gp_ref, gate_ref, o_ref):
    y = _dot(a_ref[0], w_ref[...])
    o_ref[0] = x_ref[0] + gate_ref[0] * (_rms(y) * gp_ref[...])


def _wout_post(merged, w_out, xc, g_post, mod3, gate_chunk):
    B, Tt, D = xc.shape
    nT = Tt // TR
    tile = pl.BlockSpec((1, TR, D), lambda b, i: (b, i, 0))
    row = lambda b, i: jnp.where(i == nT - 1, B, b)
    return pl.pallas_call(
        _wout_post_kernel, grid=(B, nT),
        in_specs=[tile, pl.BlockSpec((D, D), lambda b, i: (0, 0)), tile,
                  pl.BlockSpec((1, D), lambda b, i: (0, 0)),
                  pl.BlockSpec((1, 1, D), lambda b, i: (row(b, i), 0, gate_chunk))],
        out_specs=tile,
        out_shape=jax.ShapeDtypeStruct((B, Tt, D), F32),
        compiler_params=_params(("parallel", "parallel"), 48),
    )(merged, w_out, xc, g_post.reshape(1, D), mod3)


def _route_segment(arow, acol, k, chunk):
    E, T = arow.shape
    nchunk = T // chunk
    ind = lambda m: jnp.where(m, 1.0, 0.0)
    bits = pltpu.bitcast(arow, jnp.int32)
    thr = jnp.zeros((E, 1), jnp.int32)
    for bit in range(30, -1, -1):
        cand = thr | (1 << bit)
        cnt = jnp.sum(ind(bits >= cand), axis=1, keepdims=True)
        thr = jnp.where(cnt >= k, cand, thr)
    thr_val = pltpu.bitcast(thr, F32)
    gt = arow > thr_val
    eq = arow == thr_val
    need = k - jnp.sum(ind(gt), axis=1, keepdims=True)

    r_i = lax.broadcasted_iota(jnp.int32, (chunk, chunk), 0)
    c_i = lax.broadcasted_iota(jnp.int32, (chunk, chunk), 1)
    before = ind(r_i < c_i).astype(BF16)
    after = ind(c_i < r_i).astype(BF16)

    def prefix_rows(mask):
        carry = jnp.zeros((E, 1), F32)
        parts, carries = [], [carry]
        for c in range(nchunk):
            mc = ind(mask[:, c * chunk:(c + 1) * chunk]).astype(BF16)
            parts.append(_dot(mc, before) + carry)
            carry = carry + jnp.sum(mc.astype(F32), axis=1, keepdims=True)
            carries.append(carry)
        return jnp.concatenate(parts, axis=1), carries

    tie_rank, _ = prefix_rows(eq)
    sel = gt | (eq & (tie_rank < need))
    pos, carries = prefix_rows(sel)
    prow = jnp.where(sel, pos, -1.0)
    lane = lax.broadcasted_iota(jnp.int32, (E, LANES), 1)
    starts = jnp.zeros((E, LANES), F32)
    for c, cv in enumerate(carries):
        starts = jnp.where(lane == c, cv, starts)

    sub = lax.broadcasted_iota(jnp.int32, (E, LANES), 0)
    diag = sub == lane
    thr_row = jnp.sum(jnp.where(diag, thr_val, 0.0), axis=0, keepdims=True)
    need_row = jnp.sum(jnp.where(diag, need, 0.0), axis=0, keepdims=True)
    lane_ok = lax.broadcasted_iota(jnp.int32, (1, LANES), 1) < E
    gt_c = (acol > thr_row) & lane_ok
    eq_c = (acol == thr_row) & lane_ok

    def prefix_cols(mask):
        carry = jnp.zeros((1, LANES), F32)
        parts = []
        for c in range(nchunk):
            mc = ind(mask[c * chunk:(c + 1) * chunk]).astype(BF16)
            parts.append(_dot(after, mc) + carry)
            carry = carry + jnp.sum(mc.astype(F32), axis=0, keepdims=True)
        return jnp.concatenate(parts, axis=0)

    sel_c = gt_c | (eq_c & (prefix_cols(eq_c) < need_row))
    pcol = jnp.where(sel_c, prefix_cols(sel_c), -1.0)
    return prow, pcol, starts


def _route_kernel(arow_ref, acol_ref, prow_ref, pcol_ref, starts_ref, *, T, Lc, cap_l, cap_c):
    prow, pcol, starts = _route_segment(arow_ref[0, :, :T], acol_ref[0, :T], cap_l, TR)
    prow_ref[0, :, :T] = prow
    pcol_ref[0, :T] = pcol
    starts_ref[0] = starts.astype(jnp.int32)
    prow, pcol, _ = _route_segment(arow_ref[0, :, T:], acol_ref[0, T:], cap_c, TR)
    prow_ref[0, :, T:] = prow
    pcol_ref[0, T:] = pcol


def _route(arow, acol, T, Lc, cap_l, cap_c):
    B, E, Tt = arow.shape
    return pl.pallas_call(
        functools.partial(_route_kernel, T=T, Lc=Lc, cap_l=cap_l, cap_c=cap_c),
        grid=(B,),
        in_specs=[pl.BlockSpec((1, E, Tt), lambda b: (b, 0, 0)), pl.BlockSpec((1, Tt, LANES), lambda b: (b, 0, 0))],
        out_specs=[pl.BlockSpec((1, E, Tt), lambda b: (b, 0, 0)), pl.BlockSpec((1, Tt, LANES), lambda b: (b, 0, 0)),
                   pl.BlockSpec((1, E, LANES), lambda b: (b, 0, 0))],
        out_shape=[jax.ShapeDtypeStruct((B, E, Tt), F32), jax.ShapeDtypeStruct((B, Tt, LANES), F32),
                   jax.ShapeDtypeStruct((B, E, LANES), jnp.int32)],
        compiler_params=_params(("parallel",), 48),
    )(arow, acol)


def _gather_kernel(st_ref, p_ref, a_ref, h_ref, o_ref, g_ref, acc_ref, gacc_ref, *, chunk0, nchunk, cap, st):
    b = pl.program_id(0)
    e = pl.program_id(1)
    base = (b * N_EXPERTS + e) * (nchunk + 1)
    for S in range(cap // st):
        acc_ref[...] = jnp.zeros(acc_ref.shape, F32)
        gacc_ref[...] = jnp.zeros(gacc_ref.shape, F32)
        slot = (lax.broadcasted_iota(jnp.int32, (st, TR), 0) + S * st).astype(F32)

        def body(c, carry):
            lo = st_ref[base + c]
            hi = st_ref[base + c + 1]

            @pl.when((lo < (S + 1) * st) & (hi > S * st))
            def _():
                hit = p_ref[0, pl.ds(chunk0 + c, 1), :] == slot
                r0 = pl.multiple_of(c * TR, TR)
                acc_ref[...] += _dot(jnp.where(hit, 1.0, 0.0).astype(BF16), h_ref[0, pl.ds(r0, TR), :])
                gacc_ref[...] += jnp.sum(jnp.where(hit, a_ref[0, pl.ds(chunk0 + c, 1), :], 0.0),
                                         axis=1, keepdims=True)
            return carry

        lax.fori_loop(0, nchunk, body, 0)
        o_ref[0, S * st:(S + 1) * st, :] = acc_ref[...].astype(BF16)
        g_ref[0, S * st:(S + 1) * st, :] = gacc_ref[...]


def _gather(starts, prow3, arow3, hbf, prev, *, S_tot, chunk0, nchunk, cap, slot_blk0):
    B, Tt, D = hbf.shape
    E = N_EXPERTS
    nT = Tt // TR
    st = min(cap, 128)
    rows = nchunk * TR
    pspec = pl.BlockSpec((1, nT, TR), lambda b, e, s: (b * E + e, 0, 0))
    in_specs = [pspec, pspec, pl.BlockSpec((1, rows, D), lambda b, e, s: (b, chunk0 * TR // rows, 0))]
    args = [prow3, arow3, hbf]
    kern = functools.partial(_gather_kernel, chunk0=chunk0, nchunk=nchunk, cap=cap, st=st)
    aliases = {}
    if prev is not None:
        in_specs += [pl.BlockSpec(memory_space=pl.ANY)] * 2
        args += list(prev)
        aliases = {4: 0, 5: 1}
        inner = kern
        kern = lambda s, p, a, h, _x, _g, o, g, acc, gacc: inner(s, p, a, h, o, g, acc, gacc)
    grid_spec = pltpu.PrefetchScalarGridSpec(
        num_scalar_prefetch=1, grid=(B, E), in_specs=in_specs,
        out_specs=[pl.BlockSpec((1, cap, D), lambda b, e, s: (e, slot_blk0 + b, 0)),
                   pl.BlockSpec((1, cap, LANES), lambda b, e, s: (e, slot_blk0 + b, 0))],
        scratch_shapes=[pltpu.VMEM((st, D), F32), pltpu.VMEM((st, LANES), F32)])
    return pl.pallas_call(
        kern, grid_spec=grid_spec,
        out_shape=[jax.ShapeDtypeStruct((E, S_tot, D), BF16), jax.ShapeDtypeStruct((E, S_tot, LANES), F32)],
        input_output_aliases=aliases,
        compiler_params=_params(("parallel", "arbitrary"), 56),
    )(starts, *args)


def _ffn1_kernel(x_ref, wg_ref, wu_ref, o_ref, wgb_ref, wub_ref):
    @pl.when(pl.program_id(2) == 0)
    def _():
        wgb_ref[...] = wg_ref[0].astype(BF16)
        wub_ref[...] = wu_ref[0].astype(BF16)
    x = x_ref[0]
    g = _dot(x, wgb_ref[...])
    u = _dot(x, wub_ref[...])
    o_ref[0] = (g * _sigmoid(g) * u).astype(BF16)


def _ffn2_kernel(x_ref, w_ref, g_ref, o_ref, wb_ref):
    @pl.when(pl.program_id(2) == 0)
    def _():
        wb_ref[...] = w_ref[0].astype(BF16)
    o_ref[0] = (_dot(x_ref[0], wb_ref[...]) * g_ref[0, :, :1]).astype(BF16)


def _expert_matmul(kernel, x, ws, layer, row_scale=None):
    E, S, K = x.shape
    N = ws[0].shape[3]
    ws = [w.reshape(-1, K, N) for w in ws]
    tm = _pick(S, (1088, 1024, 544, 512, 272, 256, 160, 128))
    tn = _pick(N, (512, 256))
    wspec = pl.BlockSpec((1, K, tn), lambda e, n, m: (layer * E + e, 0, n))
    in_specs = [pl.BlockSpec((1, tm, K), lambda e, n, m: (e, m, 0))] + [wspec] * len(ws)
    args = [x, *ws]
    if row_scale is not None:
        in_specs.append(pl.BlockSpec((1, tm, LANES), lambda e, n, m: (e, m, 0)))
        args.append(row_scale)
    return pl.pallas_call(
        kernel, grid=(E, N // tn, S // tm),
        in_specs=in_specs,
        out_specs=pl.BlockSpec((1, tm, tn), lambda e, n, m: (e, m, n)),
        out_shape=jax.ShapeDtypeStruct((E, S, N), BF16),
        scratch_shapes=[pltpu.VMEM((K, tn), BF16)] * len(ws),
        compiler_params=_params(("parallel", "parallel", "arbitrary"), 48),
    )(*args)


SLOT_ALIGN = 16
COMBINE_WINDOW = 64


def _combine_kernel(st_ref, pc_ref, ye_ref, x_ref, gp_ref, gate_ref, o_ref, win_ref, acc_ref, sem,
                    *, nchunk, cap, W, slot0):
    E = N_EXPERTS
    b = pl.program_id(0)
    t = pl.program_id(1)

    def bounds(e):
        base = (b * E + e) * (nchunk + 1) + t
        return (st_ref[base] // SLOT_ALIGN) * SLOT_ALIGN, st_ref[base + 1]

    def window(e, r):
        first, _ = bounds(e)
        want = first + r * W
        return want, jnp.minimum(want, cap - W)

    def copy(e, r):
        _, start = window(e, r)
        row0 = pl.multiple_of(slot0 + b * cap + start, SLOT_ALIGN)
        return pltpu.make_async_copy(ye_ref.at[e, pl.ds(row0, W), :], win_ref.at[pl.ds(e * W, W), :],
                                     sem.at[e])

    for e in range(E):
        copy(e, 0).start()

    rounds = jnp.int32(1)
    for e in range(E):
        first, hi = bounds(e)
        rounds = jnp.maximum(rounds, (hi - first + W - 1) // W)

    pos1 = pc_ref[0] + 1.0
    pos_hi = jnp.floor(pos1 * (1.0 / SLOT_ALIGN))
    pos_lo = pos1 - pos_hi * SLOT_ALIGN
    col = lax.broadcasted_iota(jnp.int32, (LANES, E * W), 1)
    row = lax.broadcasted_iota(jnp.int32, (LANES, E * W), 0)
    expand = jnp.where(col // W == row, 1.0, 0.0).astype(BF16)
    pos_rep = SLOT_ALIGN * _dot(pos_hi.astype(BF16), expand) + _dot(pos_lo.astype(BF16), expand) - 1.0
    lane = lax.broadcasted_iota(jnp.int32, (1, E * W), 1)
    in_win = (lane % W).astype(F32)

    def onehot(r):
        want_row = jnp.zeros((1, E * W), F32)
        start_row = jnp.zeros((1, E * W), F32)
        for e in range(E):
            want, start = window(e, r)
            mine = lane // W == e
            want_row = jnp.where(mine, want.astype(F32), want_row)
            start_row = jnp.where(mine, start.astype(F32), start_row)
        hit = (pos_rep - start_row == in_win) & (pos_rep >= want_row)
        return jnp.where(hit, 1.0, 0.0).astype(BF16)

    lhs = onehot(0)
    for e in range(E):
        copy(e, 0).wait()
    acc_ref[...] = _dot(lhs, win_ref[...])

    def more(r, carry):
        for e in range(E):
            copy(e, r).start()
        lhs_r = onehot(r)
        for e in range(E):
            copy(e, r).wait()
        acc_ref[...] += _dot(lhs_r, win_ref[...])
        return carry

    lax.fori_loop(1, rounds, more, 0)
    o_ref[0] = x_ref[0] + gate_ref[0] * (_rms(acc_ref[...]) * gp_ref[...])


def _combine(starts, pcol, ye, xc, g_post, mod3, gate_chunk, *, tile0, ntile, nchunk, cap, slot0,
             mod_row_ctx, out):
    B, Tt, D = xc.shape
    E = N_EXPERTS
    W = min(cap, COMBINE_WINDOW)
    fresh = out is not None
    if not fresh:
        out = xc
    tok = lambda w: pl.BlockSpec((1, TR, w), lambda b, t, s: (b, tile0 + t, 0))
    mrow = (lambda b: B) if mod_row_ctx else (lambda b: b)
    in_specs = [tok(LANES), pl.BlockSpec(memory_space=pl.ANY), tok(D),
                pl.BlockSpec((1, D), lambda b, t, s: (0, 0)),
                pl.BlockSpec((1, 1, D), lambda b, t, s: (mrow(b), 0, gate_chunk))]
    args = [pcol, ye, xc, g_post.reshape(1, D), mod3]
    kern = functools.partial(_combine_kernel, nchunk=nchunk, cap=cap, W=W, slot0=slot0)
    if fresh:
        out_spec = pl.BlockSpec((1, TR, D), lambda b, t, s: (b, t, 0))
        aliases = {}
    else:
        out_spec = tok(D)
        aliases = {3: 0}
    grid_spec = pltpu.PrefetchScalarGridSpec(
        num_scalar_prefetch=1, grid=(B, ntile), in_specs=in_specs, out_specs=out_spec,
        scratch_shapes=[pltpu.VMEM((E * W, D), BF16), pltpu.VMEM((TR, D), F32),
                        pltpu.SemaphoreType.DMA((E,))])
    return pl.pallas_call(
        kern, grid_spec=grid_spec,
        out_shape=jax.ShapeDtypeStruct(out.shape, F32),
        input_output_aliases=aliases,
        compiler_params=_params(("parallel", "parallel"), 48),
    )(starts, *args)


def _rope_tables(T, Lc):
    nf = HEAD_DIM // 4
    t = np.arange(T)
    inv = ROPE_THETA ** (-jnp.arange(nf, dtype=F32) / nf)
    row = jnp.asarray(t // GRID_W, F32)
    col = jnp.asarray(t % GRID_W, F32)
    ang = jnp.stack([row[:, None] * inv, col[:, None] * inv], axis=1)
    cos = jnp.broadcast_to(jnp.cos(ang)[:, :, None, :], (T, 2, 2, nf)).reshape(T, HEAD_DIM)
    sin = jnp.broadcast_to(jnp.sin(ang)[:, :, None, :], (T, 2, 2, nf)).reshape(T, HEAD_DIM)
    sign = jnp.asarray(np.tile(np.repeat([-1.0, 1.0], nf), 2), F32)
    cos = jnp.concatenate([cos, jnp.ones((Lc, HEAD_DIM), F32)], axis=0)
    sin = jnp.concatenate([sin * sign, jnp.zeros((Lc, HEAD_DIM), F32)], axis=0)
    return cos, sin


def _split_bf16(w):
    hi = w.astype(BF16)
    return hi, (w - hi.astype(F32)).astype(BF16)


def kernel(x, c, ctx, c_ctx, w_mod, b_mod, g_pre_mix, g_post_mix, g_pre_ffn, g_post_ffn, w_in, mlstm_gate_bias, gqa_q_norm, gqa_k_norm, nat_rel_bias, w_branch, w_out, w_router, w_expert_gate, w_expert_up, w_expert_down):
    B, T, D = x.shape
    Lc = ctx.shape[1]
    depth = w_mod.shape[0]
    E = N_EXPERTS
    assert Lc == TR and T % (2 * TR) == 0 and w_router.shape[2] == E
    Tt = T + Lc
    nT = Tt // TR
    M = B * Tt
    cap_l = EC_CAPACITY_FACTOR * T // E
    cap_c = EC_CAPACITY_FACTOR * Lc // E
    scale = HEAD_DIM ** -0.5
    nchunk = T // TR

    xc = jnp.concatenate([x, ctx], axis=1)
    rows_c = -(-(B + 1) // 8) * 8
    cc = jnp.zeros((rows_c, D), F32).at[:B].set(c).at[B].set(c_ctx)
    mod = _modulation(cc, w_mod, b_mod)
    cos_t, sin_t = _rope_tables(T, Lc)

    for l in range(depth):
        last = l == depth - 1
        mod3 = mod[l].reshape(rows_c, 1, N_MOD * D)
        w_l = w_in[l]
        w_main = jnp.concatenate([w_l[:, :W_IN_GATE_COL], w_l[:, W_IN_GATE_COL + N_GATE_COLS:]], axis=1).astype(BF16)
        w_g = jnp.pad(w_l[:, W_IN_GATE_COL:W_IN_GATE_COL + N_GATE_COLS], ((0, 0), (0, LANES - N_GATE_COLS))).astype(BF16)
        gate_bias = jnp.pad(mlstm_gate_bias[l], (0, LANES - N_GATE_COLS)).reshape(1, LANES)
        nw = jnp.concatenate([jnp.tile(gqa_q_norm[l][None] * scale, (GQA_HEADS, 1)),
                              jnp.tile(gqa_k_norm[l][None], (GQA_KV_HEADS, 1))], axis=0)[:, None, :]

        hx = _norm_mod(xc, g_pre_mix[l], mod3, 0, 1)
        hx2 = hx.reshape(M, D)
        P2 = _matmul(hx2, w_main, BF16)
        P = P2.reshape(B, Tt, -1)
        gates = _matmul(hx2, w_g, F32).reshape(B, Tt, LANES)

        hf, hb = _mlstm(P, gates, gate_bias, T, Lc)

        qk = _qk_prep(P, nw, cos_t, sin_t)
        yg = _flash(qk, 0, qk, GQA_HEADS * HEAD_DIM, P, C_GV, G=GQA_KV_HEADS, R=GQA_HEADS // GQA_KV_HEADS,
                    T=T, Lc=Lc)

        bias_tab = _nat_bias_tables(nat_rel_bias[l], T)
        yn = _nat(P, bias_tab, T, Lc, scale)

        merged = _merge(hf.reshape(M, -1), hb.reshape(M, -1), P2, yg.reshape(M, -1), yn.reshape(M, -1),
                        w_branch[l].astype(BF16))
        xc = _wout_post(merged.reshape(B, Tt, D), w_out[l].astype(BF16), xc, g_post_mix[l], mod3, 2)

        wr = jnp.pad(w_router[l], ((0, 0), (0, LANES - E)))
        hx, acol, arow = _norm_mod(xc, g_pre_ffn[l], mod3, 3, 4, router=_split_bf16(wr))
        prow, pcol, starts = _route(arow, acol, T, Lc, cap_l, cap_c)
        starts_flat = starts[:, :, :nchunk + 1].reshape(-1)
        prow3 = prow.reshape(B * E, nT, TR)
        arow3 = arow.reshape(B * E, nT, TR)

        S = B * cap_l + (0 if last else B * cap_c)
        slots = None if last else (jnp.zeros((E, S, D), BF16), jnp.zeros((E, S, LANES), F32))
        slots = _gather(starts_flat, prow3, arow3, hx, slots, S_tot=S, chunk0=0, nchunk=nchunk, cap=cap_l,
                        slot_blk0=0)
        ctx_starts = jnp.tile(jnp.asarray([0, cap_c], jnp.int32), B * E)
        if not last:
            slots = _gather(ctx_starts, prow3, arow3, hx, slots, S_tot=S, chunk0=nchunk, nchunk=1, cap=cap_c,
                            slot_blk0=B * cap_l // cap_c)
        xs, gs = slots
        hid = _expert_matmul(_ffn1_kernel, xs, (w_expert_gate, w_expert_up), l)
        ye = _expert_matmul(_ffn2_kernel, hid, (w_expert_down,), l, row_scale=gs)

        if last:
            return _combine(starts_flat, pcol, ye, xc, g_post_ffn[l], mod3, 5, tile0=0, ntile=nchunk,
                            nchunk=nchunk, cap=cap_l, slot0=0, mod_row_ctx=False,
                            out=jax.ShapeDtypeStruct((B, T, D), F32))
        xc = _combine(starts_flat, pcol, ye, xc, g_post_ffn[l], mod3, 5, tile0=0, ntile=nchunk,
                      nchunk=nchunk, cap=cap_l, slot0=0, mod_row_ctx=False, out=None)
        xc = _combine(ctx_starts, pcol, ye, xc, g_post_ffn[l], mod3, 5, tile0=nT - 1, ntile=1,
                      nchunk=1, cap=cap_c, slot0=B * cap_l, mod_row_ctx=True, out=None)
```

```python
import functools

import numpy as np
import jax
import jax.numpy as jnp
from jax import lax
from jax.experimental import pallas as pl
from jax.experimental.pallas import tpu as pltpu

F32 = jnp.float32
BF16 = jnp.bfloat16

LANES = 128
HEAD_DIM = 128
MLSTM_HEADS = 4
MLSTM_DQK = 128
MLSTM_DV = 256
MLSTM_CHUNK = 256
GQA_HEADS = 8
GQA_KV_HEADS = 2
NAT_HEADS = 8
NAT_WIN_ROWS = 8
NAT_WIN_COLS = 16
GRID_W = 64
Q_BLOCK = 128
N_EXPERTS = 16
EC_CAPACITY_FACTOR = 2
ROPE_THETA = 10000.0
NORM_EPS = 1e-6
NEG_INF = -1e30
N_MOD = 6
N_BRANCHES = 3
BRANCH_WIDTH = 1024
TR = 256
N_GATE_COLS = 4 * MLSTM_HEADS

C_MQ = 0
C_MK = C_MQ + MLSTM_HEADS * MLSTM_DQK
C_MV = C_MK + MLSTM_HEADS * MLSTM_DQK
C_MO = C_MV + MLSTM_HEADS * MLSTM_DV
C_GQ = C_MO + MLSTM_HEADS * MLSTM_DV
C_GK = C_GQ + GQA_HEADS * HEAD_DIM
C_GV = C_GK + GQA_KV_HEADS * HEAD_DIM
C_NQ = C_GV + GQA_KV_HEADS * HEAD_DIM
C_NK = C_NQ + NAT_HEADS * HEAD_DIM
C_NV = C_NK + NAT_HEADS * HEAD_DIM
C_GATE = C_NV + NAT_HEADS * HEAD_DIM
W_IN_GATE_COL = C_GQ


def _params(sem, vmem_mb=None):
    return pltpu.CompilerParams(
        dimension_semantics=sem,
        vmem_limit_bytes=None if vmem_mb is None else vmem_mb * 2 ** 20)


def _pick(n, cands):
    for c in cands:
        if n % c == 0:
            return c
    raise ValueError(f"no tile for {n} in {cands}")


def _dot(a, b):
    return jnp.dot(a, b, preferred_element_type=F32)


def _dot_nt(a, b):
    return lax.dot_general(a, b, (((1,), (1,)), ((), ())), preferred_element_type=F32)


def _sigmoid(x):
    return 0.5 * jnp.tanh(0.5 * x) + 0.5


def _rms(x):
    return x * lax.rsqrt(jnp.mean(x * x, axis=-1, keepdims=True) + NORM_EPS)


def _mod_kernel(c_ref, w_ref, b_ref, o_ref):
    c = c_ref[...]
    a = (c * _sigmoid(c)).astype(BF16)
    o_ref[0] = _dot(a, w_ref[0].astype(BF16)) + b_ref[0]


def _modulation(cc, w_mod, b_mod):
    L, D, N = w_mod.shape
    tn = _pick(N, (768, 512, 256, 128))
    return pl.pallas_call(
        _mod_kernel,
        grid=(L, N // tn),
        in_specs=[pl.BlockSpec((cc.shape[0], D), lambda l, n: (0, 0)),
                  pl.BlockSpec((1, D, tn), lambda l, n: (l, 0, n)),
                  pl.BlockSpec((1, 1, tn), lambda l, n: (l, 0, n))],
        out_specs=pl.BlockSpec((1, cc.shape[0], tn), lambda l, n: (l, 0, n)),
        out_shape=jax.ShapeDtypeStruct((L, cc.shape[0], N), F32),
        compiler_params=_params(("parallel", "parallel")),
    )(cc, w_mod, b_mod.reshape(L, 1, N))


def _norm_mod_router_kernel(x_ref, g_ref, sh_ref, sc_ref, whi_ref, wlo_ref, o_ref, acol_ref, arow_ref):
    y = _rms(x_ref[0]) * g_ref[...]
    h = y * (1.0 + sc_ref[0]) + sh_ref[0]
    hb = h.astype(BF16)
    o_ref[0] = hb
    hlo = (h - hb.astype(F32)).astype(BF16)
    logits = _dot(hb, whi_ref[...]) + _dot(hlo, whi_ref[...]) + _dot(hb, wlo_ref[...])
    lane = lax.broadcasted_iota(jnp.int32, logits.shape, 1)
    valid = lane < N_EXPERTS
    logits = jnp.where(valid, logits, NEG_INF)
    ex = jnp.where(valid, jnp.exp(logits - jnp.max(logits, axis=1, keepdims=True)), 0.0)
    aff = ex / jnp.sum(ex, axis=1, keepdims=True)
    acol_ref[0] = aff
    arow_ref[0] = aff.T[:N_EXPERTS]


def _norm_mod_router(xc, g, mod3, shift_chunk, scale_chunk, whi, wlo):
    B, Tt, D = xc.shape
    nT = Tt // TR
    tile = pl.BlockSpec((1, TR, D), lambda b, i: (b, i, 0))
    row = lambda b, i: jnp.where(i == nT - 1, B, b)
    in_specs = [tile, pl.BlockSpec((1, D), lambda b, i: (0, 0)),
                pl.BlockSpec((1, 1, D), lambda b, i: (row(b, i), 0, shift_chunk)),
                pl.BlockSpec((1, 1, D), lambda b, i: (row(b, i), 0, scale_chunk))]
    args = [xc, g.reshape(1, D), mod3, mod3]
    wspec = pl.BlockSpec((D, LANES), lambda b, i: (0, 0))
    return pl.pallas_call(
        _norm_mod_router_kernel, grid=(B, nT), in_specs=in_specs + [wspec, wspec],
        out_specs=[tile, pl.BlockSpec((1, TR, LANES), lambda b, i: (b, i, 0)),
                   pl.BlockSpec((1, N_EXPERTS, TR), lambda b, i: (b, 0, i))],
        out_shape=[jax.ShapeDtypeStruct((B, Tt, D), BF16),
                   jax.ShapeDtypeStruct((B, Tt, LANES), F32),
                   jax.ShapeDtypeStruct((B, N_EXPERTS, Tt), F32)],
        compiler_params=_params(("parallel", "parallel")),
    )(*args, whi, wlo)


PROJ_ROW_SPLIT = 4


def _proj_kernel(x_ref, g_ref, shx_ref, scx_ref, shc_ref, scc_ref, w_ref, wg_ref, o_ref, og_ref, h_ref, *, T):
    tm = h_ref.shape[0]
    sub = tm // PROJ_ROW_SPLIT

    @pl.when(pl.program_id(2) == 0)
    def _():
        for k in range(PROJ_ROW_SPLIT):
            rows = slice(k * sub, (k + 1) * sub)
            y = _rms(x_ref[0, rows, :]) * g_ref[...]
            tok = pl.program_id(1) * tm + k * sub + lax.broadcasted_iota(jnp.int32, (sub, 1), 0)
            is_ctx = tok >= T
            sc = jnp.where(is_ctx, scc_ref[0], scx_ref[0])
            sh = jnp.where(is_ctx, shc_ref[0], shx_ref[0])
            h = (y * (1.0 + sc) + sh).astype(BF16)
            h_ref[rows, :] = h
            og_ref[0, rows, :] = _dot(h, wg_ref[...])

    o_ref[0] = _dot(h_ref[...], w_ref[...]).astype(o_ref.dtype)


def _project(xc, g, mod3, w_main, w_g, T):
    B, Tt, D = xc.shape
    N = w_main.shape[1]
    tm = Tt // PROJ_ROW_SPLIT
    tn = _pick(N, (512, 256, 128))
    mrow = lambda row, chunk: pl.BlockSpec((1, 1, D), lambda b, i, n: (row(b), 0, chunk))
    own, ctx = (lambda b: b), (lambda b: B)
    return pl.pallas_call(
        functools.partial(_proj_kernel, T=T), grid=(B, PROJ_ROW_SPLIT, N // tn),
        in_specs=[pl.BlockSpec((1, tm, D), lambda b, i, n: (b, i, 0)),
                  pl.BlockSpec((1, D), lambda b, i, n: (0, 0)),
                  mrow(own, 0), mrow(own, 1), mrow(ctx, 0), mrow(ctx, 1),
                  pl.BlockSpec((D, tn), lambda b, i, n: (0, n)),
                  pl.BlockSpec((D, LANES), lambda b, i, n: (0, 0))],
        out_specs=[pl.BlockSpec((1, tm, tn), lambda b, i, n: (b, i, n)),
                   pl.BlockSpec((1, tm, LANES), lambda b, i, n: (b, i, 0))],
        out_shape=[jax.ShapeDtypeStruct((B, Tt, N), BF16), jax.ShapeDtypeStruct((B, Tt, LANES), F32)],
        scratch_shapes=[pltpu.VMEM((tm, D), BF16)],
        compiler_params=_params(("parallel", "parallel", "arbitrary"), 48),
    )(xc, g.reshape(1, D), mod3, mod3, mod3, mod3, w_main, w_g)


def _qk_prep_kernel(q_ref, k_ref, nw_ref, cos_ref, sin_ref, o_ref):
    lane = lax.broadcasted_iota(jnp.int32, (q_ref.shape[1], HEAD_DIM), 1)
    first = (lane % (HEAD_DIM // 2)) < HEAD_DIM // 4
    cos = cos_ref[...]
    sin = sin_ref[...]
    for h in range(o_ref.shape[2] // HEAD_DIM):
        src, hh = (q_ref, h) if h < GQA_HEADS else (k_ref, h - GQA_HEADS)
        y = _rms(src[0, :, hh * HEAD_DIM:(hh + 1) * HEAD_DIM].astype(F32)) * nw_ref[:, h * HEAD_DIM:(h + 1) * HEAD_DIM]
        partner = jnp.where(first, pltpu.roll(y, HEAD_DIM - HEAD_DIM // 4, 1), pltpu.roll(y, HEAD_DIM // 4, 1))
        o_ref[0, :, h * HEAD_DIM:(h + 1) * HEAD_DIM] = (y * cos + partner * sin).astype(BF16)


def _qk_prep(P, nw, cos_t, sin_t):
    B, Tt, _ = P.shape
    qw, kw = GQA_HEADS * HEAD_DIM, GQA_KV_HEADS * HEAD_DIM
    w = qw + kw
    tr = _pick(Tt, (544, TR))
    return pl.pallas_call(
        _qk_prep_kernel, grid=(B, Tt // tr),
        in_specs=[pl.BlockSpec((1, tr, qw), lambda b, i: (b, i, C_GQ // qw)),
                  pl.BlockSpec((1, tr, kw), lambda b, i: (b, i, C_GK // kw)),
                  pl.BlockSpec((1, w), lambda b, i: (0, 0)),
                  pl.BlockSpec((tr, HEAD_DIM), lambda b, i: (i, 0)),
                  pl.BlockSpec((tr, HEAD_DIM), lambda b, i: (i, 0))],
        out_specs=pl.BlockSpec((1, tr, w), lambda b, i: (b, i, 0)),
        out_shape=jax.ShapeDtypeStruct((B, Tt, w), BF16),
        compiler_params=_params(("parallel", "parallel")),
    )(P, P, nw.reshape(1, w), cos_t, sin_t)


def _flash_kernel(q_ref, k_ref, v_ref, o_ref, m_ref, acc_ref, *, R, Lc, ck, n_lat_tiles):
    tile = pl.program_id(2)

    m_ref[...] = jnp.full(m_ref.shape, NEG_INF, F32)
    acc_ref[...] = jnp.zeros(acc_ref.shape, F32)

    def step(k, v):
        vaug = jnp.concatenate([v, jnp.ones(v.shape, BF16)], axis=1)
        for r in range(R):
            rows = slice(r * TR, (r + 1) * TR)
            s = _dot_nt(q_ref[0, :, r * HEAD_DIM:(r + 1) * HEAD_DIM], k)
            slabs = [s[:, c * LANES:(c + 1) * LANES] for c in range(s.shape[1] // LANES)]
            m_old = m_ref[rows, :]
            m_new = jnp.maximum(m_old, jnp.max(functools.reduce(jnp.maximum, slabs), axis=1, keepdims=True))
            alpha = jnp.exp(m_old - m_new)
            p = jnp.concatenate([jnp.exp(sl - m_new) for sl in slabs], axis=1).astype(BF16)
            acc_ref[rows, :] = jnp.concatenate([alpha, alpha], axis=1) * acc_ref[rows, :] + _dot(p, vaug)
            m_ref[rows, :] = m_new

    kv_rows = k_ref.shape[1]

    @pl.when(tile >= n_lat_tiles)
    def _():
        step(k_ref[0, kv_rows - Lc:, :], v_ref[0, kv_rows - Lc:, :])

    @pl.when(tile < n_lat_tiles)
    def _():
        def body(j, carry):
            r0 = pl.multiple_of(j * ck, ck)
            step(k_ref[0, pl.ds(r0, ck), :], v_ref[0, pl.ds(r0, ck), :])
            return carry
        lax.fori_loop(0, kv_rows // ck, body, 0)

    o = acc_ref[:, :HEAD_DIM] / acc_ref[:, HEAD_DIM:]
    for r in range(R):
        o_ref[0, :, r * HEAD_DIM:(r + 1) * HEAD_DIM] = o[r * TR:(r + 1) * TR].astype(o_ref.dtype)


def _flash(q_arr, q_col, k_arr, k_col, v_arr, v_col, *, G, R, T, Lc):
    B, Tt, _ = q_arr.shape
    qw = R * HEAD_DIM
    ck = max(c for c in range(LANES, 2304 + 1, LANES) if Tt % c == 0)
    return pl.pallas_call(
        functools.partial(_flash_kernel, R=R, Lc=Lc, ck=ck, n_lat_tiles=T // TR),
        grid=(B, G, Tt // TR),
        in_specs=[pl.BlockSpec((1, TR, qw), lambda b, g, i: (b, i, q_col // qw + g)),
                  pl.BlockSpec((1, Tt, HEAD_DIM), lambda b, g, i: (b, 0, k_col // HEAD_DIM + g)),
                  pl.BlockSpec((1, Tt, HEAD_DIM), lambda b, g, i: (b, 0, v_col // HEAD_DIM + g))],
        out_specs=pl.BlockSpec((1, TR, qw), lambda b, g, i: (b, i, g)),
        out_shape=jax.ShapeDtypeStruct((B, Tt, G * qw), BF16),
        scratch_shapes=[pltpu.VMEM((R * TR, LANES), F32), pltpu.VMEM((R * TR, 2 * HEAD_DIM), F32)],
        compiler_params=_params(("parallel", "parallel", "arbitrary"), 48),
    )(q_arr, k_arr, v_arr)


def _nat_geometry(T):
    rows = T // GRID_W
    kh = min(NAT_WIN_ROWS, rows)
    nkr = min(kh + 1, rows)
    qbr = Q_BLOCK // GRID_W
    nbr = rows // qbr
    kb = np.zeros(nbr, np.int32)
    var = np.zeros(nbr, np.int32)
    variants = []
    for j in range(nbr):
        qrow = j * qbr + np.arange(qbr)
        start_r = np.clip(qrow - kh // 2, 0, rows - kh)
        kb[j] = min(start_r[0], rows - nkr)
        sig = (int(kb[j] - j * qbr), tuple(int(s - kb[j]) for s in start_r))
        if sig not in variants:
            variants.append(sig)
        var[j] = variants.index(sig)
    return kh, nkr, qbr, nbr, kb, var, variants


def _nat_bias_kernel(rb_ref, o_ref, *, variants, kh, nkr, qbr):
    h = pl.program_id(0)
    ndr = 2 * NAT_WIN_ROWS - 1
    ndc = 2 * NAT_WIN_COLS - 1
    qc = lax.broadcasted_iota(jnp.int32, (GRID_W, GRID_W), 0)
    kc = lax.broadcasted_iota(jnp.int32, (GRID_W, GRID_W), 1)
    dc = kc - qc + NAT_WIN_COLS - 1
    start_c = jnp.clip(qc - NAT_WIN_COLS // 2, 0, GRID_W - NAT_WIN_COLS)
    col_in = (kc >= start_c) & (kc < start_c + NAT_WIN_COLS)
    masked = jnp.full((GRID_W, GRID_W), NEG_INF, F32)
    toeplitz = []
    for dr in range(ndr):
        t = jnp.zeros((GRID_W, GRID_W), F32)
        for d in range(ndc):
            t = jnp.where(dc == d, rb_ref[(h * ndr + dr) * ndc + d], t)
        toeplitz.append(jnp.where(col_in, t, NEG_INF))
    for v, (delta, srel) in enumerate(variants):
        for qr in range(qbr):
            blocks = []
            for kr in range(nkr):
                if srel[qr] <= kr < srel[qr] + kh:
                    blocks.append(toeplitz[int(np.clip(kr + delta - qr + NAT_WIN_ROWS - 1, 0, ndr - 1))])
                else:
                    blocks.append(masked)
            rows = slice(qr * GRID_W, (qr + 1) * GRID_W)
            for kr in range(0, nkr - 1, 2):
                o_ref[v, 0, rows, kr * GRID_W:(kr + 2) * GRID_W] = jnp.concatenate(blocks[kr:kr + 2], axis=1)
            if nkr % 2:
                o_ref[v, 0, rows, (nkr - 1) * GRID_W:] = blocks[-1]


def _nat_bias_tables(rel_bias, T):
    kh, nkr, qbr, _, _, _, variants = _nat_geometry(T)
    V = len(variants)
    nk = nkr * GRID_W
    return pl.pallas_call(
        functools.partial(_nat_bias_kernel, variants=variants, kh=kh, nkr=nkr, qbr=qbr),
        grid=(NAT_HEADS,),
        in_specs=[pl.BlockSpec(memory_space=pltpu.SMEM)],
        out_specs=pl.BlockSpec((V, 1, Q_BLOCK, nk), lambda h: (0, h, 0, 0)),
        out_shape=jax.ShapeDtypeStruct((V, NAT_HEADS, Q_BLOCK, nk), F32),
        compiler_params=_params(("parallel",)),
    )(rel_bias.reshape(-1))


def _nat_kernel(kb_ref, var_ref, q_ref, k_ref, v_ref, bias_ref, o_ref, *, T, nbr, nk, scale):
    kc = k_ref[0, T:, :]
    vc = v_ref[0, T:, :]

    def body(j, carry):
        q0 = pl.multiple_of(j * Q_BLOCK, Q_BLOCK)
        q = q_ref[0, pl.ds(q0, Q_BLOCK), :]
        r0 = pl.multiple_of(kb_ref[j] * GRID_W, GRID_W)
        s1 = _dot_nt(q, k_ref[0, pl.ds(r0, nk), :]) * scale + bias_ref[var_ref[j], 0]
        s2 = _dot_nt(q, kc) * scale
        m = jnp.maximum(jnp.max(s1, axis=1, keepdims=True), jnp.max(s2, axis=1, keepdims=True))
        p1 = jnp.exp(s1 - m)
        p2 = jnp.exp(s2 - m)
        l = jnp.sum(p1, axis=1, keepdims=True) + jnp.sum(p2, axis=1, keepdims=True)
        o = _dot(p1.astype(BF16), v_ref[0, pl.ds(r0, nk), :]) + _dot(p2.astype(BF16), vc)
        o_ref[0, pl.ds(q0, Q_BLOCK), :] = (o / l).astype(o_ref.dtype)
        return carry

    lax.fori_loop(0, nbr, body, 0, unroll=4)

    s = _dot_nt(q_ref[0, T:, :], kc) * scale
    p = jnp.exp(s - jnp.max(s, axis=1, keepdims=True))
    o_ref[0, T:, :] = (_dot(p.astype(BF16), vc) / jnp.sum(p, axis=1, keepdims=True)).astype(o_ref.dtype)


def _nat(P, bias_tab, T, Lc, scale):
    B, Tt, _ = P.shape
    _, nkr, _, nbr, kb, var, _ = _nat_geometry(T)
    nk = nkr * GRID_W
    V = bias_tab.shape[0]
    grid_spec = pltpu.PrefetchScalarGridSpec(
        num_scalar_prefetch=2, grid=(B, NAT_HEADS),
        in_specs=[pl.BlockSpec((1, Tt, HEAD_DIM), lambda b, h, kb_, vr: (b, 0, C_NQ // HEAD_DIM + h)),
                  pl.BlockSpec((1, Tt, HEAD_DIM), lambda b, h, kb_, vr: (b, 0, C_NK // HEAD_DIM + h)),
                  pl.BlockSpec((1, Tt, HEAD_DIM), lambda b, h, kb_, vr: (b, 0, C_NV // HEAD_DIM + h)),
                  pl.BlockSpec((V, 1, Q_BLOCK, nk), lambda b, h, kb_, vr: (0, h, 0, 0))],
        out_specs=pl.BlockSpec((1, Tt, HEAD_DIM), lambda b, h, kb_, vr: (b, 0, h)))
    return pl.pallas_call(
        functools.partial(_nat_kernel, T=T, nbr=nbr, nk=nk, scale=scale),
        grid_spec=grid_spec,
        out_shape=jax.ShapeDtypeStruct((B, Tt, NAT_HEADS * HEAD_DIM), BF16),
        compiler_params=_params(("parallel", "parallel")),
    )(jnp.asarray(kb), jnp.asarray(var), P, P, P, bias_tab)


def _mlstm_kernel(qf_ref, kf_ref, vf_ref, gf_ref, qb_ref, kb_ref, vb_ref, gb_ref, bias_ref,
                  hf_ref, hb_ref, st_ref, m_ref, *, kscale):
    L = MLSTM_CHUNK
    dv = MLSTM_DV

    @pl.when(pl.program_id(1) == 0)
    def _():
        st_ref[...] = jnp.zeros(st_ref.shape, F32)
        m_ref[...] = jnp.full(m_ref.shape, NEG_INF, F32)

    ti = lax.broadcasted_iota(jnp.int32, (L, L), 0)
    si = lax.broadcasted_iota(jnp.int32, (L, L), 1)
    ones_col = (lax.broadcasted_iota(jnp.int32, (L, LANES), 1) == 0).astype(BF16)

    for dirn, (q_ref, k_ref, v_ref, g_ref, h_ref) in enumerate(
            ((qf_ref, kf_ref, vf_ref, gf_ref, hf_ref), (qb_ref, kb_ref, vb_ref, gb_ref, hb_ref))):
        tri = (si <= ti) if dirn == 0 else (si >= ti)
        G = g_ref[0] + bias_ref[...]
        LF = jnp.minimum(G, 0.0) - jnp.log(1.0 + jnp.exp(-jnp.abs(G)))
        lf_hi = LF.astype(BF16)
        lf_lo = (LF - lf_hi.astype(F32)).astype(BF16)
        trib = tri.astype(BF16)
        Bc = _dot(trib, lf_hi) + _dot(trib, lf_lo)
        BcT = Bc.T
        GT = G.T
        bl_row = Bc[L - 1:L, :] if dirn == 0 else Bc[0:1, :]
        for h in range(MLSTM_HEADS):
            ci = dirn * 2 * MLSTM_HEADS + h
            cf = ci + MLSTM_HEADS
            bcol = Bc[:, cf:cf + 1]
            brow = BcT[cf:cf + 1, :]
            licol = G[:, ci:ci + 1]
            lirow = GT[ci:ci + 1, :]
            bl = bl_row[:, cf:cf + 1]
            m_old = m_ref[dirn, h]
            gcol = bl - bcol + licol
            m_new = jnp.maximum(bl + m_old, jnp.max(gcol, axis=0, keepdims=True))
            decay = jnp.exp(bl + m_old - m_new)
            wk = jnp.exp(gcol - m_new)
            dmat = jnp.where(tri, bcol - brow + lirow, NEG_INF)
            inter = bcol + m_old
            m_t = jnp.maximum(inter, jnp.max(dmat, axis=1, keepdims=True))
            w = jnp.exp(dmat - m_t)
            a = jnp.exp(inter - m_t)
            qh = q_ref[0, :, h * MLSTM_DQK:(h + 1) * MLSTM_DQK]
            kh = k_ref[0, :, h * MLSTM_DQK:(h + 1) * MLSTM_DQK]
            vaug = jnp.concatenate([v_ref[0, :, h * dv:(h + 1) * dv], ones_col], axis=1)
            smat = _dot_nt(qh, kh) * kscale * w
            state = st_ref[dirn, h]
            nd = _dot(smat.astype(BF16), vaug) + a * _dot(qh, state.astype(BF16))
            den = jnp.maximum(jnp.abs(nd[:, dv:dv + 1]), jnp.exp(-m_t))
            h_ref[0, :, h * dv:(h + 1) * dv] = nd[:, :dv] / den
            kT = kh.astype(F32).T.astype(BF16)
            upd = _dot(kT, (wk * vaug.astype(F32)).astype(BF16)) * kscale
            st_ref[dirn, h] = decay * state + upd
            m_ref[dirn, h] = m_new


def _mlstm(P, gates, gate_bias, T, Lc):
    B, Tt, _ = P.shape
    L = MLSTM_CHUNK
    ncl, ncc = T // L, Lc // L
    fwd = lambda i: jnp.where(i < ncc, ncl + i, i - ncc)
    bwd = lambda i: jnp.where(i < ncc, ncl + ncc - 1 - i, ncl - 1 - (i - ncc))
    qw = MLSTM_HEADS * MLSTM_DQK
    vw = MLSTM_HEADS * MLSTM_DV

    def specs(order):
        return [pl.BlockSpec((1, L, qw), lambda b, i: (b, order(i), C_MQ // qw)),
                pl.BlockSpec((1, L, qw), lambda b, i: (b, order(i), C_MK // qw)),
                pl.BlockSpec((1, L, vw), lambda b, i: (b, order(i), C_MV // vw)),
                pl.BlockSpec((1, L, LANES), lambda b, i: (b, order(i), 0))]

    out = jax.ShapeDtypeStruct((B, Tt, vw), F32)
    return pl.pallas_call(
        functools.partial(_mlstm_kernel, kscale=MLSTM_DQK ** -0.5),
        grid=(B, ncl + ncc),
        in_specs=specs(fwd) + specs(bwd) + [pl.BlockSpec((1, LANES), lambda b, i: (0, 0))],
        out_specs=[pl.BlockSpec((1, L, vw), lambda b, i: (b, fwd(i), 0)),
                   pl.BlockSpec((1, L, vw), lambda b, i: (b, bwd(i), 0))],
        out_shape=[out, out],
        scratch_shapes=[pltpu.VMEM((2, MLSTM_HEADS, MLSTM_DQK, MLSTM_DV + LANES), F32),
                        pltpu.VMEM((2, MLSTM_HEADS, 1, 1), F32)],
        compiler_params=_params(("parallel", "arbitrary")),
    )(P, P, P, gates, P, P, P, gates, gate_bias)


def _merge_kernel(hf_ref, hb_ref, mo_ref, yg_ref, yn_ref, g0_ref, g1_ref, g2_ref, wb_ref, o_ref, ym_ref):
    @pl.when(pl.program_id(1) == 0)
    def _():
        h = hf_ref[...] + hb_ref[...]
        for hh in range(MLSTM_HEADS):
            cols = slice(hh * MLSTM_DV, (hh + 1) * MLSTM_DV)
            ym_ref[:, cols] = (_sigmoid(mo_ref[:, cols].astype(F32)) * _rms(h[:, cols])).astype(BF16)

    gate = lambda r: _sigmoid(r[...].astype(F32))
    acc = gate(g0_ref) * _dot(ym_ref[...], wb_ref[0])
    acc += gate(g1_ref) * _dot(yg_ref[...], wb_ref[1])
    acc += gate(g2_ref) * _dot(yn_ref[...], wb_ref[2])
    o_ref[...] = acc.astype(BF16)


def _merge(hf, hb, P2, yg, yn, wb):
    M, W = hf.shape
    D = wb.shape[2]
    tm = _pick(M, (512, 256))
    tn = _pick(D, (512, 256))
    row = lambda c: pl.BlockSpec((tm, W), lambda i, n: (i, c))
    gate = lambda br: pl.BlockSpec((tm, tn), lambda i, n: (i, (C_GATE + br * D) // tn + n))
    return pl.pallas_call(
        _merge_kernel, grid=(M // tm, D // tn),
        in_specs=[row(0), row(0), row(C_MO // W), row(0), row(0), gate(0), gate(1), gate(2),
                  pl.BlockSpec((N_BRANCHES, W, tn), lambda i, n: (0, 0, n))],
        out_specs=pl.BlockSpec((tm, tn), lambda i, n: (i, n)),
        out_shape=jax.ShapeDtypeStruct((M, D), BF16),
        scratch_shapes=[pltpu.VMEM((tm, W), BF16)],
        compiler_params=_params(("parallel", "arbitrary")),
    )(hf, hb, P2, yg, yn, P2, P2, P2, wb)


def _wout_post_kernel(a_ref, w_ref, x_ref, gp_ref, gate_ref, o_ref):
    y = _dot(a_ref[0], w_ref[...])
    o_ref[0] = x_ref[0] + gate_ref[0] * (_rms(y) * gp_ref[...])


def _wout_post(merged, w_out, xc, g_post, mod3, gate_chunk):
    B, Tt, D = xc.shape
    nT = Tt // TR
    tile = pl.BlockSpec((1, TR, D), lambda b, i: (b, i, 0))
    row = lambda b, i: jnp.where(i == nT - 1, B, b)
    return pl.pallas_call(
        _wout_post_kernel, grid=(B, nT),
        in_specs=[tile, pl.BlockSpec((D, D), lambda b, i: (0, 0)), tile,
                  pl.BlockSpec((1, D), lambda b, i: (0, 0)),
                  pl.BlockSpec((1, 1, D), lambda b, i: (row(b, i), 0, gate_chunk))],
        out_specs=tile,
        out_shape=jax.ShapeDtypeStruct((B, Tt, D), F32),
        compiler_params=_params(("parallel", "parallel"), 48),
    )(merged, w_out, xc, g_post.reshape(1, D), mod3)


def _route_segment(arow, acol, k, chunk):
    E, T = arow.shape
    nchunk = T // chunk
    ind = lambda m: jnp.where(m, 1.0, 0.0)
    bits = pltpu.bitcast(arow, jnp.int32)
    thr = jnp.zeros((E, 1), jnp.int32)
    for bit in range(30, -1, -1):
        cand = thr | (1 << bit)
        cnt = jnp.sum(ind(bits >= cand), axis=1, keepdims=True)
        thr = jnp.where(cnt >= k, cand, thr)
    thr_val = pltpu.bitcast(thr, F32)
    gt = arow > thr_val
    eq = arow == thr_val
    need = k - jnp.sum(ind(gt), axis=1, keepdims=True)

    r_i = lax.broadcasted_iota(jnp.int32, (chunk, chunk), 0)
    c_i = lax.broadcasted_iota(jnp.int32, (chunk, chunk), 1)
    before = ind(r_i < c_i).astype(BF16)
    after = ind(c_i < r_i).astype(BF16)

    def prefix_rows(mask):
        carry = jnp.zeros((E, 1), F32)
        parts, carries = [], [carry]
        for c in range(nchunk):
            mc = ind(mask[:, c * chunk:(c + 1) * chunk]).astype(BF16)
            parts.append(_dot(mc, before) + carry)
            carry = carry + jnp.sum(mc.astype(F32), axis=1, keepdims=True)
            carries.append(carry)
        return jnp.concatenate(parts, axis=1), carries

    tie_rank, _ = prefix_rows(eq)
    sel = gt | (eq & (tie_rank < need))
    pos, carries = prefix_rows(sel)
    prow = jnp.where(sel, pos, -1.0)
    lane = lax.broadcasted_iota(jnp.int32, (E, LANES), 1)
    starts = jnp.zeros((E, LANES), F32)
    for c, cv in enumerate(carries):
        starts = jnp.where(lane == c, cv, starts)

    sub = lax.broadcasted_iota(jnp.int32, (E, LANES), 0)
    diag = sub == lane
    thr_row = jnp.sum(jnp.where(diag, thr_val, 0.0), axis=0, keepdims=True)
    need_row = jnp.sum(jnp.where(diag, need, 0.0), axis=0, keepdims=True)
    lane_ok = lax.broadcasted_iota(jnp.int32, (1, LANES), 1) < E
    gt_c = (acol > thr_row) & lane_ok
    eq_c = (acol == thr_row) & lane_ok

    def prefix_cols(mask):
        carry = jnp.zeros((1, LANES), F32)
        parts = []
        for c in range(nchunk):
            mc = ind(mask[c * chunk:(c + 1) * chunk]).astype(BF16)
            parts.append(_dot(after, mc) + carry)
            carry = carry + jnp.sum(mc.astype(F32), axis=0, keepdims=True)
        return jnp.concatenate(parts, axis=0)

    sel_c = gt_c | (eq_c & (prefix_cols(eq_c) < need_row))
    pcol = jnp.where(sel_c, prefix_cols(sel_c), -1.0)
    return prow, pcol, starts


def _route_kernel(arow_ref, acol_ref, prow_ref, pcol_ref, starts_ref, *, T, Lc, cap_l, cap_c):
    prow, pcol, starts = _route_segment(arow_ref[0, :, :T], acol_ref[0, :T], cap_l, TR)
    prow_ref[0, :, :T] = prow
    pcol_ref[0, :T] = pcol
    starts_ref[0] = starts.astype(jnp.int32)
    prow, pcol, _ = _route_segment(arow_ref[0, :, T:], acol_ref[0, T:], cap_c, TR)
    prow_ref[0, :, T:] = prow
    pcol_ref[0, T:] = pcol


def _route(arow, acol, T, Lc, cap_l, cap_c):
    B, E, Tt = arow.shape
    return pl.pallas_call(
        functools.partial(_route_kernel, T=T, Lc=Lc, cap_l=cap_l, cap_c=cap_c),
        grid=(B,),
        in_specs=[pl.BlockSpec((1, E, Tt), lambda b: (b, 0, 0)), pl.BlockSpec((1, Tt, LANES), lambda b: (b, 0, 0))],
        out_specs=[pl.BlockSpec((1, E, Tt), lambda b: (b, 0, 0)), pl.BlockSpec((1, Tt, LANES), lambda b: (b, 0, 0)),
                   pl.BlockSpec((1, E, LANES), lambda b: (b, 0, 0))],
        out_shape=[jax.ShapeDtypeStruct((B, E, Tt), F32), jax.ShapeDtypeStruct((B, Tt, LANES), F32),
                   jax.ShapeDtypeStruct((B, E, LANES), jnp.int32)],
        compiler_params=_params(("parallel",), 48),
    )(arow, acol)


def _gather_kernel(st_ref, p_ref, a_ref, h_ref, o_ref, g_ref, acc_ref, gacc_ref, *, chunk0, nchunk, cap, st):
    b = pl.program_id(0)
    e = pl.program_id(1)
    base = (b * N_EXPERTS + e) * (nchunk + 1)
    for S in range(cap // st):
        acc_ref[...] = jnp.zeros(acc_ref.shape, F32)
        gacc_ref[...] = jnp.zeros(gacc_ref.shape, F32)
        slot = (lax.broadcasted_iota(jnp.int32, (st, TR), 0) + S * st).astype(F32)

        def body(c, carry):
            lo = st_ref[base + c]
            hi = st_ref[base + c + 1]

            @pl.when((lo < (S + 1) * st) & (hi > S * st))
            def _():
                hit = p_ref[0, pl.ds(chunk0 + c, 1), :] == slot
                r0 = pl.multiple_of(c * TR, TR)
                acc_ref[...] += _dot(jnp.where(hit, 1.0, 0.0).astype(BF16), h_ref[0, pl.ds(r0, TR), :])
                gacc_ref[...] += jnp.sum(jnp.where(hit, a_ref[0, pl.ds(chunk0 + c, 1), :], 0.0),
                                         axis=1, keepdims=True)
            return carry

        lax.fori_loop(0, nchunk, body, 0)
        o_ref[0, S * st:(S + 1) * st, :] = acc_ref[...].astype(BF16)
        g_ref[0, S * st:(S + 1) * st, :] = gacc_ref[...]


def _gather(starts, prow3, arow3, hbf, prev, *, S_tot, chunk0, nchunk, cap, slot_blk0):
    B, Tt, D = hbf.shape
    E = N_EXPERTS
    nT = Tt // TR
    st = min(cap, 128)
    rows = nchunk * TR
    pspec = pl.BlockSpec((1, nT, TR), lambda b, e, s: (b * E + e, 0, 0))
    in_specs = [pspec, pspec, pl.BlockSpec((1, rows, D), lambda b, e, s: (b, chunk0 * TR // rows, 0))]
    args = [prow3, arow3, hbf]
    kern = functools.partial(_gather_kernel, chunk0=chunk0, nchunk=nchunk, cap=cap, st=st)
    aliases = {}
    if prev is not None:
        in_specs += [pl.BlockSpec(memory_space=pl.ANY)] * 2
        args += list(prev)
        aliases = {4: 0, 5: 1}
        inner = kern
        kern = lambda s, p, a, h, _x, _g, o, g, acc, gacc: inner(s, p, a, h, o, g, acc, gacc)
    grid_spec = pltpu.PrefetchScalarGridSpec(
        num_scalar_prefetch=1, grid=(B, E), in_specs=in_specs,
        out_specs=[pl.BlockSpec((1, cap, D), lambda b, e, s: (e, slot_blk0 + b, 0)),
                   pl.BlockSpec((1, cap, LANES), lambda b, e, s: (e, slot_blk0 + b, 0))],
        scratch_shapes=[pltpu.VMEM((st, D), F32), pltpu.VMEM((st, LANES), F32)])
    return pl.pallas_call(
        kern, grid_spec=grid_spec,
        out_shape=[jax.ShapeDtypeStruct((E, S_tot, D), BF16), jax.ShapeDtypeStruct((E, S_tot, LANES), F32)],
        input_output_aliases=aliases,
        compiler_params=_params(("parallel", "arbitrary"), 56),
    )(starts, *args)


def _ffn1_kernel(x_ref, wg_ref, wu_ref, o_ref, wgb_ref, wub_ref):
    @pl.when(pl.program_id(2) == 0)
    def _():
        wgb_ref[...] = wg_ref[0].astype(BF16)
        wub_ref[...] = wu_ref[0].astype(BF16)
    x = x_ref[0]
    g = _dot(x, wgb_ref[...])
    u = _dot(x, wub_ref[...])
    o_ref[0] = (g * _sigmoid(g) * u).astype(BF16)


def _ffn2_kernel(x_ref, w_ref, g_ref, o_ref, wb_ref):
    @pl.when(pl.program_id(2) == 0)
    def _():
        wb_ref[...] = w_ref[0].astype(BF16)
    o_ref[0] = (_dot(x_ref[0], wb_ref[...]) * g_ref[0, :, :1]).astype(BF16)


def _expert_matmul(kernel, x, ws, layer, row_scale=None):
    E, S, K = x.shape
    N = ws[0].shape[3]
    ws = [w.reshape(-1, K, N) for w in ws]
    tm = _pick(S, (1088, 1024, 544, 512, 272, 256, 160, 128))
    tn = _pick(N, (512, 256))
    wspec = pl.BlockSpec((1, K, tn), lambda e, n, m: (layer * E + e, 0, n))
    in_specs = [pl.BlockSpec((1, tm, K), lambda e, n, m: (e, m, 0))] + [wspec] * len(ws)
    args = [x, *ws]
    if row_scale is not None:
        in_specs.append(pl.BlockSpec((1, tm, LANES), lambda e, n, m: (e, m, 0)))
        args.append(row_scale)
    return pl.pallas_call(
        kernel, grid=(E, N // tn, S // tm),
        in_specs=in_specs,
        out_specs=pl.BlockSpec((1, tm, tn), lambda e, n, m: (e, m, n)),
        out_shape=jax.ShapeDtypeStruct((E, S, N), BF16),
        scratch_shapes=[pltpu.VMEM((K, tn), BF16)] * len(ws),
        compiler_params=_params(("parallel", "parallel", "arbitrary"), 48),
    )(*args)


SLOT_ALIGN = 16
COMBINE_WINDOW = 64


def _combine_kernel(st_ref, pc_ref, ye_ref, x_ref, gp_ref, gate_ref, o_ref, win_ref, acc_ref, sem,
                    *, nchunk, cap, W, slot0):
    E = N_EXPERTS
    b = pl.program_id(0)
    t = pl.program_id(1)
    nb, nt = pl.num_programs(0), pl.num_programs(1)
    step = b * nt + t
    buf = step % 2

    def bounds(bb, tt, e):
        base = (bb * E + e) * (nchunk + 1) + tt
        return (st_ref[base] // SLOT_ALIGN) * SLOT_ALIGN, st_ref[base + 1]

    def window(bb, tt, e, r):
        first, _ = bounds(bb, tt, e)
        want = first + r * W
        return want, jnp.minimum(want, cap - W)

    def copy(bb, tt, e, r, into):
        _, start = window(bb, tt, e, r)
        row0 = pl.multiple_of(slot0 + bb * cap + start, SLOT_ALIGN)
        return pltpu.make_async_copy(ye_ref.at[e, pl.ds(row0, W), :],
                                     win_ref.at[into, pl.ds(e * W, W), :], sem.at[into, e])

    @pl.when(step == 0)
    def _():
        for e in range(E):
            copy(b, t, e, 0, buf).start()

    @pl.when(step + 1 < nb * nt)
    def _():
        last_t = t + 1 == nt
        b_next = jnp.where(last_t, b + 1, b)
        t_next = jnp.where(last_t, 0, t + 1)
        for e in range(E):
            copy(b_next, t_next, e, 0, 1 - buf).start()

    rounds = jnp.int32(1)
    for e in range(E):
        first, hi = bounds(b, t, e)
        rounds = jnp.maximum(rounds, (hi - first + W - 1) // W)

    pos1 = pc_ref[0] + 1.0
    pos_hi = jnp.floor(pos1 * (1.0 / SLOT_ALIGN))
    pos_lo = pos1 - pos_hi * SLOT_ALIGN
    col = lax.broadcasted_iota(jnp.int32, (LANES, E * W), 1)
    row = lax.broadcasted_iota(jnp.int32, (LANES, E * W), 0)
    expand = jnp.where(col // W == row, 1.0, 0.0).astype(BF16)
    pos_rep = SLOT_ALIGN * _dot(pos_hi.astype(BF16), expand) + _dot(pos_lo.astype(BF16), expand) - 1.0
    lane = lax.broadcasted_iota(jnp.int32, (1, E * W), 1)
    in_win = (lane % W).astype(F32)

    def onehot(r):
        want_row = jnp.zeros((1, E * W), F32)
        start_row = jnp.zeros((1, E * W), F32)
        for e in range(E):
            want, start = window(b, t, e, r)
            mine = lane // W == e
            want_row = jnp.where(mine, want.astype(F32), want_row)
            start_row = jnp.where(mine, start.astype(F32), start_row)
        hit = (pos_rep - start_row == in_win) & (pos_rep >= want_row)
        return jnp.where(hit, 1.0, 0.0).astype(BF16)

    lhs = onehot(0)
    for e in range(E):
        copy(b, t, e, 0, buf).wait()
    acc_ref[...] = _dot(lhs, win_ref[buf])

    def more(r, carry):
        for e in range(E):
            copy(b, t, e, r, buf).start()
        lhs_r = onehot(r)
        for e in range(E):
            copy(b, t, e, r, buf).wait()
        acc_ref[...] += _dot(lhs_r, win_ref[buf])
        return carry

    lax.fori_loop(1, rounds, more, 0)
    o_ref[0] = x_ref[0] + gate_ref[0] * (_rms(acc_ref[...]) * gp_ref[...])


def _combine(starts, pcol, ye, xc, g_post, mod3, gate_chunk, *, tile0, ntile, nchunk, cap, slot0,
             mod_row_ctx, out):
    B, Tt, D = xc.shape
    E = N_EXPERTS
    W = min(cap, COMBINE_WINDOW)
    fresh = out is not None
    if not fresh:
        out = xc
    tok = lambda w: pl.BlockSpec((1, TR, w), lambda b, t, s: (b, tile0 + t, 0))
    mrow = (lambda b: B) if mod_row_ctx else (lambda b: b)
    in_specs = [tok(LANES), pl.BlockSpec(memory_space=pl.ANY), tok(D),
                pl.BlockSpec((1, D), lambda b, t, s: (0, 0)),
                pl.BlockSpec((1, 1, D), lambda b, t, s: (mrow(b), 0, gate_chunk))]
    args = [pcol, ye, xc, g_post.reshape(1, D), mod3]
    kern = functools.partial(_combine_kernel, nchunk=nchunk, cap=cap, W=W, slot0=slot0)
    if fresh:
        out_spec = pl.BlockSpec((1, TR, D), lambda b, t, s: (b, t, 0))
        aliases = {}
    else:
        out_spec = tok(D)
        aliases = {3: 0}
    grid_spec = pltpu.PrefetchScalarGridSpec(
        num_scalar_prefetch=1, grid=(B, ntile), in_specs=in_specs, out_specs=out_spec,
        scratch_shapes=[pltpu.VMEM((2, E * W, D), BF16), pltpu.VMEM((TR, D), F32),
                        pltpu.SemaphoreType.DMA((2, E))])
    return pl.pallas_call(
        kern, grid_spec=grid_spec,
        out_shape=jax.ShapeDtypeStruct(out.shape, F32),
        input_output_aliases=aliases,
        compiler_params=_params(("arbitrary", "arbitrary"), 48),
    )(starts, *args)


def _rope_tables(T, Lc):
    nf = HEAD_DIM // 4
    t = np.arange(T)
    inv = ROPE_THETA ** (-jnp.arange(nf, dtype=F32) / nf)
    row = jnp.asarray(t // GRID_W, F32)
    col = jnp.asarray(t % GRID_W, F32)
    ang = jnp.stack([row[:, None] * inv, col[:, None] * inv], axis=1)
    cos = jnp.broadcast_to(jnp.cos(ang)[:, :, None, :], (T, 2, 2, nf)).reshape(T, HEAD_DIM)
    sin = jnp.broadcast_to(jnp.sin(ang)[:, :, None, :], (T, 2, 2, nf)).reshape(T, HEAD_DIM)
    sign = jnp.asarray(np.tile(np.repeat([-1.0, 1.0], nf), 2), F32)
    cos = jnp.concatenate([cos, jnp.ones((Lc, HEAD_DIM), F32)], axis=0)
    sin = jnp.concatenate([sin * sign, jnp.zeros((Lc, HEAD_DIM), F32)], axis=0)
    return cos, sin


def _split_bf16(w):
    hi = w.astype(BF16)
    return hi, (w - hi.astype(F32)).astype(BF16)


def kernel(x, c, ctx, c_ctx, w_mod, b_mod, g_pre_mix, g_post_mix, g_pre_ffn, g_post_ffn, w_in, mlstm_gate_bias, gqa_q_norm, gqa_k_norm, nat_rel_bias, w_branch, w_out, w_router, w_expert_gate, w_expert_up, w_expert_down):
    B, T, D = x.shape
    Lc = ctx.shape[1]
    depth = w_mod.shape[0]
    E = N_EXPERTS
    assert Lc == TR and T % (2 * TR) == 0 and w_router.shape[2] == E
    Tt = T + Lc
    nT = Tt // TR
    M = B * Tt
    cap_l = EC_CAPACITY_FACTOR * T // E
    cap_c = EC_CAPACITY_FACTOR * Lc // E
    scale = HEAD_DIM ** -0.5
    nchunk = T // TR

    xc = jnp.concatenate([x, ctx], axis=1)
    rows_c = -(-(B + 1) // 8) * 8
    cc = jnp.zeros((rows_c, D), F32).at[:B].set(c).at[B].set(c_ctx)
    mod = _modulation(cc, w_mod, b_mod)
    cos_t, sin_t = _rope_tables(T, Lc)

    for l in range(depth):
        last = l == depth - 1
        mod3 = mod[l].reshape(rows_c, 1, N_MOD * D)
        w_l = w_in[l]
        w_main = jnp.concatenate([w_l[:, :W_IN_GATE_COL], w_l[:, W_IN_GATE_COL + N_GATE_COLS:]], axis=1).astype(BF16)
        w_g = jnp.pad(w_l[:, W_IN_GATE_COL:W_IN_GATE_COL + N_GATE_COLS], ((0, 0), (0, LANES - N_GATE_COLS))).astype(BF16)
        gate_bias = jnp.pad(mlstm_gate_bias[l], (0, LANES - N_GATE_COLS)).reshape(1, LANES)
        nw = jnp.concatenate([jnp.tile(gqa_q_norm[l][None] * scale, (GQA_HEADS, 1)),
                              jnp.tile(gqa_k_norm[l][None], (GQA_KV_HEADS, 1))], axis=0)[:, None, :]

        P, gates = _project(xc, g_pre_mix[l], mod3, w_main, w_g, T)
        P2 = P.reshape(M, -1)

        hf, hb = _mlstm(P, gates, gate_bias, T, Lc)

        qk = _qk_prep(P, nw, cos_t, sin_t)
        yg = _flash(qk, 0, qk, GQA_HEADS * HEAD_DIM, P, C_GV, G=GQA_KV_HEADS, R=GQA_HEADS // GQA_KV_HEADS,
                    T=T, Lc=Lc)

        bias_tab = _nat_bias_tables(nat_rel_bias[l], T)
        yn = _nat(P, bias_tab, T, Lc, scale)

        merged = _merge(hf.reshape(M, -1), hb.reshape(M, -1), P2, yg.reshape(M, -1), yn.reshape(M, -1),
                        w_branch[l].astype(BF16))
        xc = _wout_post(merged.reshape(B, Tt, D), w_out[l].astype(BF16), xc, g_post_mix[l], mod3, 2)

        wr = jnp.pad(w_router[l], ((0, 0), (0, LANES - E)))
        hx, acol, arow = _norm_mod_router(xc, g_pre_ffn[l], mod3, 3, 4, *_split_bf16(wr))
        prow, pcol, starts = _route(arow, acol, T, Lc, cap_l, cap_c)
        starts_flat = starts[:, :, :nchunk + 1].reshape(-1)
        prow3 = prow.reshape(B * E, nT, TR)
        arow3 = arow.reshape(B * E, nT, TR)

        S = B * cap_l + (0 if last else B * cap_c)
        slots = None if last else (jnp.zeros((E, S, D), BF16), jnp.zeros((E, S, LANES), F32))
        slots = _gather(starts_flat, prow3, arow3, hx, slots, S_tot=S, chunk0=0, nchunk=nchunk, cap=cap_l,
                        slot_blk0=0)
        ctx_starts = jnp.tile(jnp.asarray([0, cap_c], jnp.int32), B * E)
        if not last:
            slots = _gather(ctx_starts, prow3, arow3, hx, slots, S_tot=S, chunk0=nchunk, nchunk=1, cap=cap_c,
                            slot_blk0=B * cap_l // cap_c)
        xs, gs = slots
        hid = _expert_matmul(_ffn1_kernel, xs, (w_expert_gate, w_expert_up), l)
        ye = _expert_matmul(_ffn2_kernel, hid, (w_expert_down,), l, row_scale=gs)

        if last:
            return _combine(starts_flat, pcol, ye, xc, g_post_ffn[l], mod3, 5, tile0=0, ntile=nchunk,
                            nchunk=nchunk, cap=cap_l, slot0=0, mod_row_ctx=False,
                            out=jax.ShapeDtypeStruct((B, T, D), F32))
        xc = _combine(starts_flat, pcol, ye, xc, g_post_ffn[l], mod3, 5, tile0=0, ntile=nchunk,
                      nchunk=nchunk, cap=cap_l, slot0=0, mod_row_ctx=False, out=None)
        xc = _combine(ctx_starts, pcol, ye, xc, g_post_ffn[l], mod3, 5, tile0=nT - 1, ntile=1,
                      nchunk=1, cap=cap_c, slot0=B * cap_l, mod_row_ctx=True, out=None)
```

```python
import functools
import math

import numpy as np
import jax
import jax.numpy as jnp
from jax import lax
from jax.experimental import pallas as pl
from jax.experimental.pallas import tpu as pltpu

F32 = jnp.float32
BF16 = jnp.bfloat16

LANES = 128
HEAD_DIM = 128
MLSTM_HEADS = 4
MLSTM_DQK = 128
MLSTM_DV = 256
MLSTM_CHUNK = 256
GQA_HEADS = 8
GQA_KV_HEADS = 2
NAT_HEADS = 8
NAT_WIN_ROWS = 8
NAT_WIN_COLS = 16
GRID_W = 64
Q_BLOCK = 128
N_EXPERTS = 16
EC_CAPACITY_FACTOR = 2
ROPE_THETA = 10000.0
NORM_EPS = 1e-6
NEG_INF = -1e30
N_MOD = 6
N_BRANCHES = 3
BRANCH_WIDTH = 1024
TR = 256
N_GATE_COLS = 4 * MLSTM_HEADS

C_MQ = 0
C_MK = C_MQ + MLSTM_HEADS * MLSTM_DQK
C_MV = C_MK + MLSTM_HEADS * MLSTM_DQK
C_MO = C_MV + MLSTM_HEADS * MLSTM_DV
C_GQ = C_MO + MLSTM_HEADS * MLSTM_DV
C_GK = C_GQ + GQA_HEADS * HEAD_DIM
C_GV = C_GK + GQA_KV_HEADS * HEAD_DIM
C_NQ = C_GV + GQA_KV_HEADS * HEAD_DIM
C_NK = C_NQ + NAT_HEADS * HEAD_DIM
C_NV = C_NK + NAT_HEADS * HEAD_DIM
C_GATE = C_NV + NAT_HEADS * HEAD_DIM
W_IN_GATE_COL = C_GQ


def _params(sem, vmem_mb=None):
    return pltpu.CompilerParams(
        dimension_semantics=sem,
        vmem_limit_bytes=None if vmem_mb is None else vmem_mb * 2 ** 20)


def _pick(n, cands):
    for c in cands:
        if n % c == 0:
            return c
    raise ValueError(f"no tile for {n} in {cands}")


def _dot(a, b):
    return jnp.dot(a, b, preferred_element_type=F32)


def _dot_nt(a, b):
    return lax.dot_general(a, b, (((1,), (1,)), ((), ())), preferred_element_type=F32)


def _sigmoid(x):
    return 0.5 * jnp.tanh(0.5 * x) + 0.5


def _rms(x):
    return x * lax.rsqrt(jnp.mean(x * x, axis=-1, keepdims=True) + NORM_EPS)


def _mod_kernel(c_ref, w_ref, b_ref, o_ref):
    c = c_ref[...]
    a = (c * _sigmoid(c)).astype(BF16)
    o_ref[0] = _dot(a, w_ref[0].astype(BF16)) + b_ref[0]


def _modulation(cc, w_mod, b_mod):
    L, D, N = w_mod.shape
    tn = _pick(N, (768, 512, 256, 128))
    return pl.pallas_call(
        _mod_kernel,
        grid=(L, N // tn),
        in_specs=[pl.BlockSpec((cc.shape[0], D), lambda l, n: (0, 0)),
                  pl.BlockSpec((1, D, tn), lambda l, n: (l, 0, n)),
                  pl.BlockSpec((1, 1, tn), lambda l, n: (l, 0, n))],
        out_specs=pl.BlockSpec((1, cc.shape[0], tn), lambda l, n: (l, 0, n)),
        out_shape=jax.ShapeDtypeStruct((L, cc.shape[0], N), F32),
        compiler_params=_params(("parallel", "parallel")),
    )(cc, w_mod, b_mod.reshape(L, 1, N))


def _norm_mod_router_kernel(x_ref, g_ref, sh_ref, sc_ref, whi_ref, wlo_ref, o_ref, acol_ref, arow_ref):
    y = _rms(x_ref[0]) * g_ref[...]
    h = y * (1.0 + sc_ref[0]) + sh_ref[0]
    hb = h.astype(BF16)
    o_ref[0] = hb
    hlo = (h - hb.astype(F32)).astype(BF16)
    logits = _dot(hb, whi_ref[...]) + _dot(hlo, whi_ref[...]) + _dot(hb, wlo_ref[...])
    lane = lax.broadcasted_iota(jnp.int32, logits.shape, 1)
    valid = lane < N_EXPERTS
    logits = jnp.where(valid, logits, NEG_INF)
    ex = jnp.where(valid, jnp.exp(logits - jnp.max(logits, axis=1, keepdims=True)), 0.0)
    aff = ex / jnp.sum(ex, axis=1, keepdims=True)
    acol_ref[0] = aff
    arow_ref[0] = aff.T[:N_EXPERTS]


def _norm_mod_router(xc, g, mod3, shift_chunk, scale_chunk, whi, wlo):
    B, Tt, D = xc.shape
    nT = Tt // TR
    tile = pl.BlockSpec((1, TR, D), lambda b, i: (b, i, 0))
    row = lambda b, i: jnp.where(i == nT - 1, B, b)
    in_specs = [tile, pl.BlockSpec((1, D), lambda b, i: (0, 0)),
                pl.BlockSpec((1, 1, D), lambda b, i: (row(b, i), 0, shift_chunk)),
                pl.BlockSpec((1, 1, D), lambda b, i: (row(b, i), 0, scale_chunk))]
    args = [xc, g.reshape(1, D), mod3, mod3]
    wspec = pl.BlockSpec((D, LANES), lambda b, i: (0, 0))
    return pl.pallas_call(
        _norm_mod_router_kernel, grid=(B, nT), in_specs=in_specs + [wspec, wspec],
        out_specs=[tile, pl.BlockSpec((1, TR, LANES), lambda b, i: (b, i, 0)),
                   pl.BlockSpec((1, N_EXPERTS, TR), lambda b, i: (b, 0, i))],
        out_shape=[jax.ShapeDtypeStruct((B, Tt, D), BF16),
                   jax.ShapeDtypeStruct((B, Tt, LANES), F32),
                   jax.ShapeDtypeStruct((B, N_EXPERTS, Tt), F32)],
        compiler_params=_params(("parallel", "parallel")),
    )(*args, whi, wlo)


PROJ_ROW_SPLIT = 4


def _proj_kernel(x_ref, g_ref, shx_ref, scx_ref, shc_ref, scc_ref, w_ref, wg_ref, o_ref, og_ref, h_ref, *, T):
    tm = h_ref.shape[0]
    sub = tm // PROJ_ROW_SPLIT

    @pl.when(pl.program_id(2) == 0)
    def _():
        for k in range(PROJ_ROW_SPLIT):
            rows = slice(k * sub, (k + 1) * sub)
            y = _rms(x_ref[0, rows, :]) * g_ref[...]
            tok = pl.program_id(1) * tm + k * sub + lax.broadcasted_iota(jnp.int32, (sub, 1), 0)
            is_ctx = tok >= T
            sc = jnp.where(is_ctx, scc_ref[0], scx_ref[0])
            sh = jnp.where(is_ctx, shc_ref[0], shx_ref[0])
            h = (y * (1.0 + sc) + sh).astype(BF16)
            h_ref[rows, :] = h
            og_ref[0, rows, :] = _dot(h, wg_ref[...])

    acc = _dot(h_ref[...], w_ref[...])
    is_gate = pl.program_id(2) >= C_GATE // o_ref.shape[2]
    o_ref[0] = jnp.where(is_gate, _sigmoid(acc), acc).astype(o_ref.dtype)


def _project(xc, g, mod3, w_main, w_g, T):
    B, Tt, D = xc.shape
    N = w_main.shape[1]
    tm = Tt // PROJ_ROW_SPLIT
    tn = _pick(math.gcd(N, C_GATE), (768, 512, 256, 128))
    mrow = lambda row, chunk: pl.BlockSpec((1, 1, D), lambda b, i, n: (row(b), 0, chunk))
    own, ctx = (lambda b: b), (lambda b: B)
    return pl.pallas_call(
        functools.partial(_proj_kernel, T=T), grid=(B, PROJ_ROW_SPLIT, N // tn),
        in_specs=[pl.BlockSpec((1, tm, D), lambda b, i, n: (b, i, 0)),
                  pl.BlockSpec((1, D), lambda b, i, n: (0, 0)),
                  mrow(own, 0), mrow(own, 1), mrow(ctx, 0), mrow(ctx, 1),
                  pl.BlockSpec((D, tn), lambda b, i, n: (0, n)),
                  pl.BlockSpec((D, LANES), lambda b, i, n: (0, 0))],
        out_specs=[pl.BlockSpec((1, tm, tn), lambda b, i, n: (b, i, n)),
                   pl.BlockSpec((1, tm, LANES), lambda b, i, n: (b, i, 0))],
        out_shape=[jax.ShapeDtypeStruct((B, Tt, N), BF16), jax.ShapeDtypeStruct((B, Tt, LANES), F32)],
        scratch_shapes=[pltpu.VMEM((tm, D), BF16)],
        compiler_params=_params(("parallel", "parallel", "arbitrary"), 48),
    )(xc, g.reshape(1, D), mod3, mod3, mod3, mod3, w_main, w_g)


def _qk_prep_kernel(q_ref, k_ref, nw_ref, cos_ref, sin_ref, o_ref):
    lane = lax.broadcasted_iota(jnp.int32, (q_ref.shape[1], HEAD_DIM), 1)
    first = (lane % (HEAD_DIM // 2)) < HEAD_DIM // 4
    cos = cos_ref[...]
    sin = sin_ref[...]
    for h in range(o_ref.shape[2] // HEAD_DIM):
        src, hh = (q_ref, h) if h < GQA_HEADS else (k_ref, h - GQA_HEADS)
        y = _rms(src[0, :, hh * HEAD_DIM:(hh + 1) * HEAD_DIM].astype(F32)) * nw_ref[:, h * HEAD_DIM:(h + 1) * HEAD_DIM]
        partner = jnp.where(first, pltpu.roll(y, HEAD_DIM - HEAD_DIM // 4, 1), pltpu.roll(y, HEAD_DIM // 4, 1))
        o_ref[0, :, h * HEAD_DIM:(h + 1) * HEAD_DIM] = (y * cos + partner * sin).astype(BF16)


def _qk_prep(P, nw, cos_t, sin_t):
    B, Tt, _ = P.shape
    qw, kw = GQA_HEADS * HEAD_DIM, GQA_KV_HEADS * HEAD_DIM
    w = qw + kw
    tr = _pick(Tt, (544, TR))
    return pl.pallas_call(
        _qk_prep_kernel, grid=(B, Tt // tr),
        in_specs=[pl.BlockSpec((1, tr, qw), lambda b, i: (b, i, C_GQ // qw)),
                  pl.BlockSpec((1, tr, kw), lambda b, i: (b, i, C_GK // kw)),
                  pl.BlockSpec((1, w), lambda b, i: (0, 0)),
                  pl.BlockSpec((tr, HEAD_DIM), lambda b, i: (i, 0)),
                  pl.BlockSpec((tr, HEAD_DIM), lambda b, i: (i, 0))],
        out_specs=pl.BlockSpec((1, tr, w), lambda b, i: (b, i, 0)),
        out_shape=jax.ShapeDtypeStruct((B, Tt, w), BF16),
        compiler_params=_params(("parallel", "parallel")),
    )(P, P, nw.reshape(1, w), cos_t, sin_t)


def _flash_kernel(q_ref, k_ref, v_ref, o_ref, m_ref, acc_ref, *, R, Lc, ck, n_lat_tiles):
    tile = pl.program_id(2)

    m_ref[...] = jnp.full(m_ref.shape, NEG_INF, F32)
    acc_ref[...] = jnp.zeros(acc_ref.shape, F32)

    def step(k, v):
        vaug = jnp.concatenate([v, jnp.ones(v.shape, BF16)], axis=1)
        for r in range(R):
            rows = slice(r * TR, (r + 1) * TR)
            s = _dot_nt(q_ref[0, :, r * HEAD_DIM:(r + 1) * HEAD_DIM], k)
            slabs = [s[:, c * LANES:(c + 1) * LANES] for c in range(s.shape[1] // LANES)]
            m_old = m_ref[rows, :]
            m_new = jnp.maximum(m_old, jnp.max(functools.reduce(jnp.maximum, slabs), axis=1, keepdims=True))
            alpha = jnp.exp(m_old - m_new)
            p = jnp.concatenate([jnp.exp(sl - m_new) for sl in slabs], axis=1).astype(BF16)
            acc_ref[rows, :] = jnp.concatenate([alpha, alpha], axis=1) * acc_ref[rows, :] + _dot(p, vaug)
            m_ref[rows, :] = m_new

    kv_rows = k_ref.shape[1]

    @pl.when(tile >= n_lat_tiles)
    def _():
        step(k_ref[0, kv_rows - Lc:, :], v_ref[0, kv_rows - Lc:, :])

    @pl.when(tile < n_lat_tiles)
    def _():
        def body(j, carry):
            r0 = pl.multiple_of(j * ck, ck)
            step(k_ref[0, pl.ds(r0, ck), :], v_ref[0, pl.ds(r0, ck), :])
            return carry
        lax.fori_loop(0, kv_rows // ck, body, 0)

    o = acc_ref[:, :HEAD_DIM] / acc_ref[:, HEAD_DIM:]
    for r in range(R):
        o_ref[0, :, r * HEAD_DIM:(r + 1) * HEAD_DIM] = o[r * TR:(r + 1) * TR].astype(o_ref.dtype)


def _flash(q_arr, q_col, k_arr, k_col, v_arr, v_col, *, G, R, T, Lc):
    B, Tt, _ = q_arr.shape
    qw = R * HEAD_DIM
    ck = max(c for c in range(LANES, 2304 + 1, LANES) if Tt % c == 0)
    return pl.pallas_call(
        functools.partial(_flash_kernel, R=R, Lc=Lc, ck=ck, n_lat_tiles=T // TR),
        grid=(B, G, Tt // TR),
        in_specs=[pl.BlockSpec((1, TR, qw), lambda b, g, i: (b, i, q_col // qw + g)),
                  pl.BlockSpec((1, Tt, HEAD_DIM), lambda b, g, i: (b, 0, k_col // HEAD_DIM + g)),
                  pl.BlockSpec((1, Tt, HEAD_DIM), lambda b, g, i: (b, 0, v_col // HEAD_DIM + g))],
        out_specs=pl.BlockSpec((1, TR, qw), lambda b, g, i: (b, i, g)),
        out_shape=jax.ShapeDtypeStruct((B, Tt, G * qw), BF16),
        scratch_shapes=[pltpu.VMEM((R * TR, LANES), F32), pltpu.VMEM((R * TR, 2 * HEAD_DIM), F32)],
        compiler_params=_params(("parallel", "parallel", "arbitrary"), 48),
    )(q_arr, k_arr, v_arr)


def _nat_geometry(T):
    rows = T // GRID_W
    kh = min(NAT_WIN_ROWS, rows)
    nkr = min(kh + 1, rows)
    qbr = Q_BLOCK // GRID_W
    nbr = rows // qbr
    kb = np.zeros(nbr, np.int32)
    var = np.zeros(nbr, np.int32)
    variants = []
    for j in range(nbr):
        qrow = j * qbr + np.arange(qbr)
        start_r = np.clip(qrow - kh // 2, 0, rows - kh)
        kb[j] = min(start_r[0], rows - nkr)
        sig = (int(kb[j] - j * qbr), tuple(int(s - kb[j]) for s in start_r))
        if sig not in variants:
            variants.append(sig)
        var[j] = variants.index(sig)
    return kh, nkr, qbr, nbr, kb, var, variants


def _nat_bias_kernel(rb_ref, o_ref, *, variants, kh, nkr, qbr):
    h = pl.program_id(0)
    ndr = 2 * NAT_WIN_ROWS - 1
    ndc = 2 * NAT_WIN_COLS - 1
    qc = lax.broadcasted_iota(jnp.int32, (GRID_W, GRID_W), 0)
    kc = lax.broadcasted_iota(jnp.int32, (GRID_W, GRID_W), 1)
    dc = kc - qc + NAT_WIN_COLS - 1
    start_c = jnp.clip(qc - NAT_WIN_COLS // 2, 0, GRID_W - NAT_WIN_COLS)
    col_in = (kc >= start_c) & (kc < start_c + NAT_WIN_COLS)
    masked = jnp.full((GRID_W, GRID_W), NEG_INF, F32)
    toeplitz = []
    for dr in range(ndr):
        t = jnp.zeros((GRID_W, GRID_W), F32)
        for d in range(ndc):
            t = jnp.where(dc == d, rb_ref[(h * ndr + dr) * ndc + d], t)
        toeplitz.append(jnp.where(col_in, t, NEG_INF))
    for v, (delta, srel) in enumerate(variants):
        for qr in range(qbr):
            blocks = []
            for kr in range(nkr):
                if srel[qr] <= kr < srel[qr] + kh:
                    blocks.append(toeplitz[int(np.clip(kr + delta - qr + NAT_WIN_ROWS - 1, 0, ndr - 1))])
                else:
                    blocks.append(masked)
            rows = slice(qr * GRID_W, (qr + 1) * GRID_W)
            for kr in range(0, nkr - 1, 2):
                o_ref[v, 0, rows, kr * GRID_W:(kr + 2) * GRID_W] = jnp.concatenate(blocks[kr:kr + 2], axis=1)
            if nkr % 2:
                o_ref[v, 0, rows, (nkr - 1) * GRID_W:] = blocks[-1]


def _nat_bias_tables(rel_bias, T):
    kh, nkr, qbr, _, _, _, variants = _nat_geometry(T)
    V = len(variants)
    nk = nkr * GRID_W
    return pl.pallas_call(
        functools.partial(_nat_bias_kernel, variants=variants, kh=kh, nkr=nkr, qbr=qbr),
        grid=(NAT_HEADS,),
        in_specs=[pl.BlockSpec(memory_space=pltpu.SMEM)],
        out_specs=pl.BlockSpec((V, 1, Q_BLOCK, nk), lambda h: (0, h, 0, 0)),
        out_shape=jax.ShapeDtypeStruct((V, NAT_HEADS, Q_BLOCK, nk), F32),
        compiler_params=_params(("parallel",)),
    )(rel_bias.reshape(-1))


def _nat_kernel(kb_ref, var_ref, q_ref, k_ref, v_ref, bias_ref, o_ref, s1_ref, s2_ref, *, T, nbr, nk, scale):
    kc = k_ref[0, T:, :]
    vc = v_ref[0, T:, :]

    def scores(j, slot):
        q = q_ref[0, pl.ds(pl.multiple_of(j * Q_BLOCK, Q_BLOCK), Q_BLOCK), :]
        r0 = pl.multiple_of(kb_ref[j] * GRID_W, GRID_W)
        s1_ref[slot] = _dot_nt(q, k_ref[0, pl.ds(r0, nk), :]) * scale + bias_ref[var_ref[j], 0]
        s2_ref[slot] = _dot_nt(q, kc) * scale

    def finish(j, slot):
        s1 = s1_ref[slot]
        s2 = s2_ref[slot]
        r0 = pl.multiple_of(kb_ref[j] * GRID_W, GRID_W)
        m = jnp.maximum(jnp.max(s1, axis=1, keepdims=True), jnp.max(s2, axis=1, keepdims=True))
        p1 = jnp.exp(s1 - m)
        p2 = jnp.exp(s2 - m)
        l = jnp.sum(p1, axis=1, keepdims=True) + jnp.sum(p2, axis=1, keepdims=True)
        o = _dot(p1.astype(BF16), v_ref[0, pl.ds(r0, nk), :]) + _dot(p2.astype(BF16), vc)
        o_ref[0, pl.ds(pl.multiple_of(j * Q_BLOCK, Q_BLOCK), Q_BLOCK), :] = (o / l).astype(o_ref.dtype)

    scores(0, 0)

    def body(i, carry):
        j = 2 * i
        scores(j + 1, 1)
        finish(j, 0)
        scores(jnp.minimum(j + 2, nbr - 1), 0)
        finish(j + 1, 1)
        return carry

    lax.fori_loop(0, nbr // 2, body, 0)

    s = _dot_nt(q_ref[0, T:, :], kc) * scale
    p = jnp.exp(s - jnp.max(s, axis=1, keepdims=True))
    o_ref[0, T:, :] = (_dot(p.astype(BF16), vc) / jnp.sum(p, axis=1, keepdims=True)).astype(o_ref.dtype)


def _nat(P, bias_tab, T, Lc, scale):
    B, Tt, _ = P.shape
    _, nkr, _, nbr, kb, var, _ = _nat_geometry(T)
    nk = nkr * GRID_W
    V = bias_tab.shape[0]
    grid_spec = pltpu.PrefetchScalarGridSpec(
        num_scalar_prefetch=2, grid=(B, NAT_HEADS),
        in_specs=[pl.BlockSpec((1, Tt, HEAD_DIM), lambda b, h, kb_, vr: (b, 0, C_NQ // HEAD_DIM + h)),
                  pl.BlockSpec((1, Tt, HEAD_DIM), lambda b, h, kb_, vr: (b, 0, C_NK // HEAD_DIM + h)),
                  pl.BlockSpec((1, Tt, HEAD_DIM), lambda b, h, kb_, vr: (b, 0, C_NV // HEAD_DIM + h)),
                  pl.BlockSpec((V, 1, Q_BLOCK, nk), lambda b, h, kb_, vr: (0, h, 0, 0))],
        out_specs=pl.BlockSpec((1, Tt, HEAD_DIM), lambda b, h, kb_, vr: (b, 0, h)),
        scratch_shapes=[pltpu.VMEM((2, Q_BLOCK, nk), F32), pltpu.VMEM((2, Q_BLOCK, Lc), F32)])
    assert nbr % 2 == 0
    return pl.pallas_call(
        functools.partial(_nat_kernel, T=T, nbr=nbr, nk=nk, scale=scale),
        grid_spec=grid_spec,
        out_shape=jax.ShapeDtypeStruct((B, Tt, NAT_HEADS * HEAD_DIM), BF16),
        compiler_params=_params(("parallel", "parallel")),
    )(jnp.asarray(kb), jnp.asarray(var), P, P, P, bias_tab)


def _mlstm_kernel(qf_ref, kf_ref, vf_ref, gf_ref, qb_ref, kb_ref, vb_ref, gb_ref, bias_ref,
                  hf_ref, hb_ref, st_ref, m_ref, *, kscale):
    L = MLSTM_CHUNK
    dv = MLSTM_DV

    @pl.when(pl.program_id(1) == 0)
    def _():
        st_ref[...] = jnp.zeros(st_ref.shape, F32)
        m_ref[...] = jnp.full(m_ref.shape, NEG_INF, F32)

    ti = lax.broadcasted_iota(jnp.int32, (L, L), 0)
    si = lax.broadcasted_iota(jnp.int32, (L, L), 1)
    ones_col = (lax.broadcasted_iota(jnp.int32, (L, LANES), 1) == 0).astype(BF16)

    for dirn, (q_ref, k_ref, v_ref, g_ref, h_ref) in enumerate(
            ((qf_ref, kf_ref, vf_ref, gf_ref, hf_ref), (qb_ref, kb_ref, vb_ref, gb_ref, hb_ref))):
        tri = (si <= ti) if dirn == 0 else (si >= ti)
        G = g_ref[0] + bias_ref[...]
        LF = jnp.minimum(G, 0.0) - jnp.log(1.0 + jnp.exp(-jnp.abs(G)))
        lf_hi = LF.astype(BF16)
        lf_lo = (LF - lf_hi.astype(F32)).astype(BF16)
        trib = tri.astype(BF16)
        Bc = _dot(trib, lf_hi) + _dot(trib, lf_lo)
        BcT = Bc.T
        GT = G.T
        bl_row = Bc[L - 1:L, :] if dirn == 0 else Bc[0:1, :]
        for h in range(MLSTM_HEADS):
            ci = dirn * 2 * MLSTM_HEADS + h
            cf = ci + MLSTM_HEADS
            bcol = Bc[:, cf:cf + 1]
            brow = BcT[cf:cf + 1, :]
            licol = G[:, ci:ci + 1]
            lirow = GT[ci:ci + 1, :]
            bl = bl_row[:, cf:cf + 1]
            m_old = m_ref[dirn, h]
            gcol = bl - bcol + licol
            m_new = jnp.maximum(bl + m_old, jnp.max(gcol, axis=0, keepdims=True))
            decay = jnp.exp(bl + m_old - m_new)
            wk = jnp.exp(gcol - m_new)
            dmat = jnp.where(tri, bcol - brow + lirow, NEG_INF)
            inter = bcol + m_old
            m_t = jnp.maximum(inter, jnp.max(dmat, axis=1, keepdims=True))
            w = jnp.exp(dmat - m_t)
            a = jnp.exp(inter - m_t)
            qh = q_ref[0, :, h * MLSTM_DQK:(h + 1) * MLSTM_DQK]
            kh = k_ref[0, :, h * MLSTM_DQK:(h + 1) * MLSTM_DQK]
            vaug = jnp.concatenate([v_ref[0, :, h * dv:(h + 1) * dv], ones_col], axis=1)
            smat = _dot_nt(qh, kh) * kscale * w
            state = st_ref[dirn, h]
            nd = _dot(smat.astype(BF16), vaug) + a * _dot(qh, state.astype(BF16))
            den = jnp.maximum(jnp.abs(nd[:, dv:dv + 1]), jnp.exp(-m_t))
            h_ref[0, :, h * dv:(h + 1) * dv] = (nd[:, :dv] / den).astype(h_ref.dtype)
            kT = kh.astype(F32).T.astype(BF16)
            upd = _dot(kT, (wk * vaug.astype(F32)).astype(BF16)) * kscale
            st_ref[dirn, h] = decay * state + upd
            m_ref[dirn, h] = m_new


def _mlstm(P, gates, gate_bias, T, Lc):
    B, Tt, _ = P.shape
    L = MLSTM_CHUNK
    ncl, ncc = T // L, Lc // L
    fwd = lambda i: jnp.where(i < ncc, ncl + i, i - ncc)
    bwd = lambda i: jnp.where(i < ncc, ncl + ncc - 1 - i, ncl - 1 - (i - ncc))
    qw = MLSTM_HEADS * MLSTM_DQK
    vw = MLSTM_HEADS * MLSTM_DV

    def specs(order):
        return [pl.BlockSpec((1, L, qw), lambda b, i: (b, order(i), C_MQ // qw)),
                pl.BlockSpec((1, L, qw), lambda b, i: (b, order(i), C_MK // qw)),
                pl.BlockSpec((1, L, vw), lambda b, i: (b, order(i), C_MV // vw)),
                pl.BlockSpec((1, L, LANES), lambda b, i: (b, order(i), 0))]

    out = jax.ShapeDtypeStruct((B, Tt, vw), BF16)
    return pl.pallas_call(
        functools.partial(_mlstm_kernel, kscale=MLSTM_DQK ** -0.5),
        grid=(B, ncl + ncc),
        in_specs=specs(fwd) + specs(bwd) + [pl.BlockSpec((1, LANES), lambda b, i: (0, 0))],
        out_specs=[pl.BlockSpec((1, L, vw), lambda b, i: (b, fwd(i), 0)),
                   pl.BlockSpec((1, L, vw), lambda b, i: (b, bwd(i), 0))],
        out_shape=[out, out],
        scratch_shapes=[pltpu.VMEM((2, MLSTM_HEADS, MLSTM_DQK, MLSTM_DV + LANES), F32),
                        pltpu.VMEM((2, MLSTM_HEADS, 1, 1), F32)],
        compiler_params=_params(("parallel", "arbitrary")),
    )(P, P, P, gates, P, P, P, gates, gate_bias)


def _merge_kernel(hf_ref, hb_ref, mo_ref, yg_ref, yn_ref, g0_ref, g1_ref, g2_ref, wb_ref, o_ref, ym_ref):
    @pl.when(pl.program_id(1) == 0)
    def _():
        h = hf_ref[...].astype(F32) + hb_ref[...].astype(F32)
        for hh in range(MLSTM_HEADS):
            cols = slice(hh * MLSTM_DV, (hh + 1) * MLSTM_DV)
            ym_ref[:, cols] = (_sigmoid(mo_ref[:, cols].astype(F32)) * _rms(h[:, cols])).astype(BF16)

    gate = lambda r: r[...].astype(F32)
    acc = gate(g0_ref) * _dot(ym_ref[...], wb_ref[0])
    acc += gate(g1_ref) * _dot(yg_ref[...], wb_ref[1])
    acc += gate(g2_ref) * _dot(yn_ref[...], wb_ref[2])
    o_ref[...] = acc.astype(BF16)


def _merge(hf, hb, P2, yg, yn, wb):
    M, W = hf.shape
    D = wb.shape[2]
    tm = _pick(M, (1024, 512, 256))
    tn = _pick(D, (512, 256))
    row = lambda c: pl.BlockSpec((tm, W), lambda i, n: (i, c))
    gate = lambda br: pl.BlockSpec((tm, tn), lambda i, n: (i, (C_GATE + br * D) // tn + n))
    return pl.pallas_call(
        _merge_kernel, grid=(M // tm, D // tn),
        in_specs=[row(0), row(0), row(C_MO // W), row(0), row(0), gate(0), gate(1), gate(2),
                  pl.BlockSpec((N_BRANCHES, W, tn), lambda i, n: (0, 0, n))],
        out_specs=pl.BlockSpec((tm, tn), lambda i, n: (i, n)),
        out_shape=jax.ShapeDtypeStruct((M, D), BF16),
        scratch_shapes=[pltpu.VMEM((tm, W), BF16)],
        compiler_params=_params(("parallel", "arbitrary"), 56),
    )(hf, hb, P2, yg, yn, P2, P2, P2, wb)


def _wout_post_kernel(a_ref, w_ref, x_ref, gp_ref, gate_ref, o_ref):
    y = _dot(a_ref[0], w_ref[...])
    o_ref[0] = x_ref[0] + gate_ref[0] * (_rms(y) * gp_ref[...])


def _wout_post(merged, w_out, xc, g_post, mod3, gate_chunk):
    B, Tt, D = xc.shape
    nT = Tt // TR
    tile = pl.BlockSpec((1, TR, D), lambda b, i: (b, i, 0))
    row = lambda b, i: jnp.where(i == nT - 1, B, b)
    return pl.pallas_call(
        _wout_post_kernel, grid=(B, nT),
        in_specs=[tile, pl.BlockSpec((D, D), lambda b, i: (0, 0)), tile,
                  pl.BlockSpec((1, D), lambda b, i: (0, 0)),
                  pl.BlockSpec((1, 1, D), lambda b, i: (row(b, i), 0, gate_chunk))],
        out_specs=tile,
        out_shape=jax.ShapeDtypeStruct((B, Tt, D), F32),
        compiler_params=_params(("parallel", "parallel"), 48),
    )(merged, w_out, xc, g_post.reshape(1, D), mod3)


def _route_segment(arow, acol, k, chunk):
    E, T = arow.shape
    nchunk = T // chunk
    ind = lambda m: jnp.where(m, 1.0, 0.0)
    bits = pltpu.bitcast(arow, jnp.int32)
    thr = jnp.zeros((E, 1), jnp.int32)
    for bit in range(30, -1, -1):
        cand = thr | (1 << bit)
        cnt = jnp.sum(ind(bits >= cand), axis=1, keepdims=True)
        thr = jnp.where(cnt >= k, cand, thr)
    thr_val = pltpu.bitcast(thr, F32)
    gt = arow > thr_val
    eq = arow == thr_val
    need = k - jnp.sum(ind(gt), axis=1, keepdims=True)

    r_i = lax.broadcasted_iota(jnp.int32, (chunk, chunk), 0)
    c_i = lax.broadcasted_iota(jnp.int32, (chunk, chunk), 1)
    before = ind(r_i < c_i).astype(BF16)
    after = ind(c_i < r_i).astype(BF16)

    def prefix_rows(mask):
        carry = jnp.zeros((E, 1), F32)
        parts, carries = [], [carry]
        for c in range(nchunk):
            mc = ind(mask[:, c * chunk:(c + 1) * chunk]).astype(BF16)
            parts.append(_dot(mc, before) + carry)
            carry = carry + jnp.sum(mc.astype(F32), axis=1, keepdims=True)
            carries.append(carry)
        return jnp.concatenate(parts, axis=1), carries

    tie_rank, _ = prefix_rows(eq)
    sel = gt | (eq & (tie_rank < need))
    pos, carries = prefix_rows(sel)
    prow = jnp.where(sel, pos, -1.0)
    lane = lax.broadcasted_iota(jnp.int32, (E, LANES), 1)
    starts = jnp.zeros((E, LANES), F32)
    for c, cv in enumerate(carries):
        starts = jnp.where(lane == c, cv, starts)

    sub = lax.broadcasted_iota(jnp.int32, (E, LANES), 0)
    diag = sub == lane
    thr_row = jnp.sum(jnp.where(diag, thr_val, 0.0), axis=0, keepdims=True)
    need_row = jnp.sum(jnp.where(diag, need, 0.0), axis=0, keepdims=True)
    lane_ok = lax.broadcasted_iota(jnp.int32, (1, LANES), 1) < E
    gt_c = (acol > thr_row) & lane_ok
    eq_c = (acol == thr_row) & lane_ok

    def prefix_cols(mask):
        carry = jnp.zeros((1, LANES), F32)
        parts = []
        for c in range(nchunk):
            mc = ind(mask[c * chunk:(c + 1) * chunk]).astype(BF16)
            parts.append(_dot(after, mc) + carry)
            carry = carry + jnp.sum(mc.astype(F32), axis=0, keepdims=True)
        return jnp.concatenate(parts, axis=0)

    sel_c = gt_c | (eq_c & (prefix_cols(eq_c) < need_row))
    pcol = jnp.where(sel_c, prefix_cols(sel_c), -1.0)
    return prow, pcol, starts


def _route_kernel(arow_ref, acol_ref, prow_ref, pcol_ref, starts_ref, *, T, Lc, cap_l, cap_c):
    prow, pcol, starts = _route_segment(arow_ref[0, :, :T], acol_ref[0, :T], cap_l, TR)
    prow_ref[0, :, :T] = prow
    pcol_ref[0, :T] = pcol
    starts_ref[0] = starts.astype(jnp.int32)
    prow, pcol, _ = _route_segment(arow_ref[0, :, T:], acol_ref[0, T:], cap_c, TR)
    prow_ref[0, :, T:] = prow
    pcol_ref[0, T:] = pcol


def _route(arow, acol, T, Lc, cap_l, cap_c):
    B, E, Tt = arow.shape
    return pl.pallas_call(
        functools.partial(_route_kernel, T=T, Lc=Lc, cap_l=cap_l, cap_c=cap_c),
        grid=(B,),
        in_specs=[pl.BlockSpec((1, E, Tt), lambda b: (b, 0, 0)), pl.BlockSpec((1, Tt, LANES), lambda b: (b, 0, 0))],
        out_specs=[pl.BlockSpec((1, E, Tt), lambda b: (b, 0, 0)), pl.BlockSpec((1, Tt, LANES), lambda b: (b, 0, 0)),
                   pl.BlockSpec((1, E, LANES), lambda b: (b, 0, 0))],
        out_shape=[jax.ShapeDtypeStruct((B, E, Tt), F32), jax.ShapeDtypeStruct((B, Tt, LANES), F32),
                   jax.ShapeDtypeStruct((B, E, LANES), jnp.int32)],
        compiler_params=_params(("parallel",), 48),
    )(arow, acol)


def _gather_kernel(st_ref, p_ref, a_ref, h_ref, o_ref, g_ref, acc_ref, gacc_ref, *, chunk0, nchunk, cap, st):
    b = pl.program_id(0)
    e = pl.program_id(1)
    base = (b * N_EXPERTS + e) * (nchunk + 1)
    for S in range(cap // st):
        acc_ref[...] = jnp.zeros(acc_ref.shape, F32)
        gacc_ref[...] = jnp.zeros(gacc_ref.shape, F32)
        slot = (lax.broadcasted_iota(jnp.int32, (st, TR), 0) + S * st).astype(F32)

        def body(c, carry):
            lo = st_ref[base + c]
            hi = st_ref[base + c + 1]

            @pl.when((lo < (S + 1) * st) & (hi > S * st))
            def _():
                hit = p_ref[0, pl.ds(chunk0 + c, 1), :] == slot
                r0 = pl.multiple_of(c * TR, TR)
                acc_ref[...] += _dot(jnp.where(hit, 1.0, 0.0).astype(BF16), h_ref[0, pl.ds(r0, TR), :])
                gacc_ref[...] += jnp.sum(jnp.where(hit, a_ref[0, pl.ds(chunk0 + c, 1), :], 0.0),
                                         axis=1, keepdims=True)
            return carry

        lax.fori_loop(0, nchunk, body, 0)
        o_ref[0, S * st:(S + 1) * st, :] = acc_ref[...].astype(BF16)
        g_ref[0, S * st:(S + 1) * st, :] = gacc_ref[...]


def _gather(starts, prow3, arow3, hbf, prev, *, S_tot, chunk0, nchunk, cap, slot_blk0):
    B, Tt, D = hbf.shape
    E = N_EXPERTS
    nT = Tt // TR
    st = min(cap, 128)
    rows = nchunk * TR
    pspec = pl.BlockSpec((1, nT, TR), lambda b, e, s: (b * E + e, 0, 0))
    in_specs = [pspec, pspec, pl.BlockSpec((1, rows, D), lambda b, e, s: (b, chunk0 * TR // rows, 0))]
    args = [prow3, arow3, hbf]
    kern = functools.partial(_gather_kernel, chunk0=chunk0, nchunk=nchunk, cap=cap, st=st)
    aliases = {}
    if prev is not None:
        in_specs += [pl.BlockSpec(memory_space=pl.ANY)] * 2
        args += list(prev)
        aliases = {4: 0, 5: 1}
        inner = kern
        kern = lambda s, p, a, h, _x, _g, o, g, acc, gacc: inner(s, p, a, h, o, g, acc, gacc)
    grid_spec = pltpu.PrefetchScalarGridSpec(
        num_scalar_prefetch=1, grid=(B, E), in_specs=in_specs,
        out_specs=[pl.BlockSpec((1, cap, D), lambda b, e, s: (e, slot_blk0 + b, 0)),
                   pl.BlockSpec((1, cap, LANES), lambda b, e, s: (e, slot_blk0 + b, 0))],
        scratch_shapes=[pltpu.VMEM((st, D), F32), pltpu.VMEM((st, LANES), F32)])
    return pl.pallas_call(
        kern, grid_spec=grid_spec,
        out_shape=[jax.ShapeDtypeStruct((E, S_tot, D), BF16), jax.ShapeDtypeStruct((E, S_tot, LANES), F32)],
        input_output_aliases=aliases,
        compiler_params=_params(("parallel", "arbitrary"), 56),
    )(starts, *args)


def _ffn1_kernel(x_ref, wg_ref, wu_ref, o_ref, wgb_ref, wub_ref):
    @pl.when(pl.program_id(2) == 0)
    def _():
        wgb_ref[...] = wg_ref[0].astype(BF16)
        wub_ref[...] = wu_ref[0].astype(BF16)
    x = x_ref[0]
    g = _dot(x, wgb_ref[...])
    u = _dot(x, wub_ref[...])
    o_ref[0] = (g * _sigmoid(g) * u).astype(BF16)


def _ffn2_kernel(x_ref, w_ref, g_ref, o_ref, wb_ref):
    @pl.when(pl.program_id(2) == 0)
    def _():
        wb_ref[...] = w_ref[0].astype(BF16)
    o_ref[0] = (_dot(x_ref[0], wb_ref[...]) * g_ref[0, :, :1]).astype(BF16)


def _expert_matmul(kernel, x, ws, layer, row_scale=None):
    E, S, K = x.shape
    N = ws[0].shape[3]
    ws = [w.reshape(-1, K, N) for w in ws]
    tm = _pick(S, (1088, 1024, 544, 512, 272, 256, 160, 128))
    tn = _pick(N, (512, 256))
    wspec = pl.BlockSpec((1, K, tn), lambda e, n, m: (layer * E + e, 0, n))
    in_specs = [pl.BlockSpec((1, tm, K), lambda e, n, m: (e, m, 0))] + [wspec] * len(ws)
    args = [x, *ws]
    if row_scale is not None:
        in_specs.append(pl.BlockSpec((1, tm, LANES), lambda e, n, m: (e, m, 0)))
        args.append(row_scale)
    return pl.pallas_call(
        kernel, grid=(E, N // tn, S // tm),
        in_specs=in_specs,
        out_specs=pl.BlockSpec((1, tm, tn), lambda e, n, m: (e, m, n)),
        out_shape=jax.ShapeDtypeStruct((E, S, N), BF16),
        scratch_shapes=[pltpu.VMEM((K, tn), BF16)] * len(ws),
        compiler_params=_params(("parallel", "parallel", "arbitrary"), 48),
    )(*args)


SLOT_ALIGN = 16
COMBINE_WINDOW = 64


def _combine_kernel(st_ref, pc_ref, ye_ref, x_ref, gp_ref, gate_ref, o_ref, win_ref, acc_ref, sem,
                    *, nchunk, cap, W, slot0):
    E = N_EXPERTS
    b = pl.program_id(0)
    t = pl.program_id(1)
    nb, nt = pl.num_programs(0), pl.num_programs(1)
    step = b * nt + t
    buf = step % 2

    def bounds(bb, tt, e):
        base = (bb * E + e) * (nchunk + 1) + tt
        return (st_ref[base] // SLOT_ALIGN) * SLOT_ALIGN, st_ref[base + 1]

    def window(bb, tt, e, r):
        first, _ = bounds(bb, tt, e)
        want = first + r * W
        return want, jnp.minimum(want, cap - W)

    def copy(bb, tt, e, r, into):
        _, start = window(bb, tt, e, r)
        row0 = pl.multiple_of(slot0 + bb * cap + start, SLOT_ALIGN)
        return pltpu.make_async_copy(ye_ref.at[e, pl.ds(row0, W), :],
                                     win_ref.at[into, pl.ds(e * W, W), :], sem.at[into, e])

    @pl.when(step == 0)
    def _():
        for e in range(E):
            copy(b, t, e, 0, buf).start()

    @pl.when(step + 1 < nb * nt)
    def _():
        last_t = t + 1 == nt
        b_next = jnp.where(last_t, b + 1, b)
        t_next = jnp.where(last_t, 0, t + 1)
        for e in range(E):
            copy(b_next, t_next, e, 0, 1 - buf).start()

    rounds = jnp.int32(1)
    for e in range(E):
        first, hi = bounds(b, t, e)
        rounds = jnp.maximum(rounds, (hi - first + W - 1) // W)

    pos1 = pc_ref[0] + 1.0
    pos_hi = jnp.floor(pos1 * (1.0 / SLOT_ALIGN))
    pos_lo = pos1 - pos_hi * SLOT_ALIGN
    col = lax.broadcasted_iota(jnp.int32, (LANES, E * W), 1)
    row = lax.broadcasted_iota(jnp.int32, (LANES, E * W), 0)
    expand = jnp.where(col // W == row, 1.0, 0.0).astype(BF16)
    pos_rep = SLOT_ALIGN * _dot(pos_hi.astype(BF16), expand) + _dot(pos_lo.astype(BF16), expand) - 1.0
    lane = lax.broadcasted_iota(jnp.int32, (1, E * W), 1)
    in_win = (lane % W).astype(F32)

    def onehot(r):
        want_row = jnp.zeros((1, E * W), F32)
        start_row = jnp.zeros((1, E * W), F32)
        for e in range(E):
            want, start = window(b, t, e, r)
            mine = lane // W == e
            want_row = jnp.where(mine, want.astype(F32), want_row)
            start_row = jnp.where(mine, start.astype(F32), start_row)
        hit = (pos_rep - start_row == in_win) & (pos_rep >= want_row)
        return jnp.where(hit, 1.0, 0.0).astype(BF16)

    lhs = onehot(0)
    for e in range(E):
        copy(b, t, e, 0, buf).wait()
    acc_ref[...] = _dot(lhs, win_ref[buf])

    def more(r, carry):
        for e in range(E):
            copy(b, t, e, r, buf).start()
        lhs_r = onehot(r)
        for e in range(E):
            copy(b, t, e, r, buf).wait()
        acc_ref[...] += _dot(lhs_r, win_ref[buf])
        return carry

    lax.fori_loop(1, rounds, more, 0)
    o_ref[0] = x_ref[0] + gate_ref[0] * (_rms(acc_ref[...]) * gp_ref[...])


def _combine(starts, pcol, ye, xc, g_post, mod3, gate_chunk, *, tile0, ntile, nchunk, cap, slot0,
             mod_row_ctx, out):
    B, Tt, D = xc.shape
    E = N_EXPERTS
    W = min(cap, COMBINE_WINDOW)
    fresh = out is not None
    if not fresh:
        out = xc
    tok = lambda w: pl.BlockSpec((1, TR, w), lambda b, t, s: (b, tile0 + t, 0))
    mrow = (lambda b: B) if mod_row_ctx else (lambda b: b)
    in_specs = [tok(LANES), pl.BlockSpec(memory_space=pl.ANY), tok(D),
                pl.BlockSpec((1, D), lambda b, t, s: (0, 0)),
                pl.BlockSpec((1, 1, D), lambda b, t, s: (mrow(b), 0, gate_chunk))]
    args = [pcol, ye, xc, g_post.reshape(1, D), mod3]
    kern = functools.partial(_combine_kernel, nchunk=nchunk, cap=cap, W=W, slot0=slot0)
    if fresh:
        out_spec = pl.BlockSpec((1, TR, D), lambda b, t, s: (b, t, 0))
        aliases = {}
    else:
        out_spec = tok(D)
        aliases = {3: 0}
    grid_spec = pltpu.PrefetchScalarGridSpec(
        num_scalar_prefetch=1, grid=(B, ntile), in_specs=in_specs, out_specs=out_spec,
        scratch_shapes=[pltpu.VMEM((2, E * W, D), BF16), pltpu.VMEM((TR, D), F32),
                        pltpu.SemaphoreType.DMA((2, E))])
    return pl.pallas_call(
        kern, grid_spec=grid_spec,
        out_shape=jax.ShapeDtypeStruct(out.shape, F32),
        input_output_aliases=aliases,
        compiler_params=_params(("arbitrary", "arbitrary"), 48),
    )(starts, *args)


def _rope_tables(T, Lc):
    nf = HEAD_DIM // 4
    t = np.arange(T)
    inv = ROPE_THETA ** (-jnp.arange(nf, dtype=F32) / nf)
    row = jnp.asarray(t // GRID_W, F32)
    col = jnp.asarray(t % GRID_W, F32)
    ang = jnp.stack([row[:, None] * inv, col[:, None] * inv], axis=1)
    cos = jnp.broadcast_to(jnp.cos(ang)[:, :, None, :], (T, 2, 2, nf)).reshape(T, HEAD_DIM)
    sin = jnp.broadcast_to(jnp.sin(ang)[:, :, None, :], (T, 2, 2, nf)).reshape(T, HEAD_DIM)
    sign = jnp.asarray(np.tile(np.repeat([-1.0, 1.0], nf), 2), F32)
    cos = jnp.concatenate([cos, jnp.ones((Lc, HEAD_DIM), F32)], axis=0)
    sin = jnp.concatenate([sin * sign, jnp.zeros((Lc, HEAD_DIM), F32)], axis=0)
    return cos, sin


def _split_bf16(w):
    hi = w.astype(BF16)
    return hi, (w - hi.astype(F32)).astype(BF16)


def kernel(x, c, ctx, c_ctx, w_mod, b_mod, g_pre_mix, g_post_mix, g_pre_ffn, g_post_ffn, w_in, mlstm_gate_bias, gqa_q_norm, gqa_k_norm, nat_rel_bias, w_branch, w_out, w_router, w_expert_gate, w_expert_up, w_expert_down):
    B, T, D = x.shape
    Lc = ctx.shape[1]
    depth = w_mod.shape[0]
    E = N_EXPERTS
    assert Lc == TR and T % (2 * TR) == 0 and w_router.shape[2] == E
    Tt = T + Lc
    nT = Tt // TR
    M = B * Tt
    cap_l = EC_CAPACITY_FACTOR * T // E
    cap_c = EC_CAPACITY_FACTOR * Lc // E
    scale = HEAD_DIM ** -0.5
    nchunk = T // TR

    xc = jnp.concatenate([x, ctx], axis=1)
    rows_c = -(-(B + 1) // 8) * 8
    cc = jnp.zeros((rows_c, D), F32).at[:B].set(c).at[B].set(c_ctx)
    mod = _modulation(cc, w_mod, b_mod)
    cos_t, sin_t = _rope_tables(T, Lc)

    for l in range(depth):
        last = l == depth - 1
        mod3 = mod[l].reshape(rows_c, 1, N_MOD * D)
        w_l = w_in[l]
        w_main = jnp.concatenate([w_l[:, :W_IN_GATE_COL], w_l[:, W_IN_GATE_COL + N_GATE_COLS:]], axis=1).astype(BF16)
        w_g = jnp.pad(w_l[:, W_IN_GATE_COL:W_IN_GATE_COL + N_GATE_COLS], ((0, 0), (0, LANES - N_GATE_COLS))).astype(BF16)
        gate_bias = jnp.pad(mlstm_gate_bias[l], (0, LANES - N_GATE_COLS)).reshape(1, LANES)
        nw = jnp.concatenate([jnp.tile(gqa_q_norm[l][None] * scale, (GQA_HEADS, 1)),
                              jnp.tile(gqa_k_norm[l][None], (GQA_KV_HEADS, 1))], axis=0)[:, None, :]

        P, gates = _project(xc, g_pre_mix[l], mod3, w_main, w_g, T)
        P2 = P.reshape(M, -1)

        hf, hb = _mlstm(P, gates, gate_bias, T, Lc)

        qk = _qk_prep(P, nw, cos_t, sin_t)
        yg = _flash(qk, 0, qk, GQA_HEADS * HEAD_DIM, P, C_GV, G=GQA_KV_HEADS, R=GQA_HEADS // GQA_KV_HEADS,
                    T=T, Lc=Lc)

        bias_tab = _nat_bias_tables(nat_rel_bias[l], T)
        yn = _nat(P, bias_tab, T, Lc, scale)

        merged = _merge(hf.reshape(M, -1), hb.reshape(M, -1), P2, yg.reshape(M, -1), yn.reshape(M, -1),
                        w_branch[l].astype(BF16))
        xc = _wout_post(merged.reshape(B, Tt, D), w_out[l].astype(BF16), xc, g_post_mix[l], mod3, 2)

        wr = jnp.pad(w_router[l], ((0, 0), (0, LANES - E)))
        hx, acol, arow = _norm_mod_router(xc, g_pre_ffn[l], mod3, 3, 4, *_split_bf16(wr))
        prow, pcol, starts = _route(arow, acol, T, Lc, cap_l, cap_c)
        starts_flat = starts[:, :, :nchunk + 1].reshape(-1)
        prow3 = prow.reshape(B * E, nT, TR)
        arow3 = arow.reshape(B * E, nT, TR)

        S = B * cap_l + (0 if last else B * cap_c)
        slots = None if last else (jnp.zeros((E, S, D), BF16), jnp.zeros((E, S, LANES), F32))
        slots = _gather(starts_flat, prow3, arow3, hx, slots, S_tot=S, chunk0=0, nchunk=nchunk, cap=cap_l,
                        slot_blk0=0)
        ctx_starts = jnp.tile(jnp.asarray([0, cap_c], jnp.int32), B * E)
        if not last:
            slots = _gather(ctx_starts, prow3, arow3, hx, slots, S_tot=S, chunk0=nchunk, nchunk=1, cap=cap_c,
                            slot_blk0=B * cap_l // cap_c)
        xs, gs = slots
        hid = _expert_matmul(_ffn1_kernel, xs, (w_expert_gate, w_expert_up), l)
        ye = _expert_matmul(_ffn2_kernel, hid, (w_expert_down,), l, row_scale=gs)

        if last:
            return _combine(starts_flat, pcol, ye, xc, g_post_ffn[l], mod3, 5, tile0=0, ntile=nchunk,
                            nchunk=nchunk, cap=cap_l, slot0=0, mod_row_ctx=False,
                            out=jax.ShapeDtypeStruct((B, T, D), F32))
        xc = _combine(starts_flat, pcol, ye, xc, g_post_ffn[l], mod3, 5, tile0=0, ntile=nchunk,
                      nchunk=nchunk, cap=cap_l, slot0=0, mod_row_ctx=False, out=None)
        xc = _combine(ctx_starts, pcol, ye, xc, g_post_ffn[l], mod3, 5, tile0=nT - 1, ntile=1,
                      nchunk=1, cap=cap_c, slot0=B * cap_l, mod_row_ctx=True, out=None)
```

```python
import functools
import math

import numpy as np
import jax
import jax.numpy as jnp
from jax import lax
from jax.experimental import pallas as pl
from jax.experimental.pallas import tpu as pltpu

F32 = jnp.float32
BF16 = jnp.bfloat16

LANES = 128
HEAD_DIM = 128
MLSTM_HEADS = 4
MLSTM_DQK = 128
MLSTM_DV = 256
MLSTM_CHUNK = 256
GQA_HEADS = 8
GQA_KV_HEADS = 2
NAT_HEADS = 8
NAT_WIN_ROWS = 8
NAT_WIN_COLS = 16
GRID_W = 64
Q_BLOCK = 128
N_EXPERTS = 16
EC_CAPACITY_FACTOR = 2
ROPE_THETA = 10000.0
NORM_EPS = 1e-6
NEG_INF = -1e30
N_MOD = 6
N_BRANCHES = 3
BRANCH_WIDTH = 1024
TR = 256
N_GATE_COLS = 4 * MLSTM_HEADS

C_MQ = 0
C_MK = C_MQ + MLSTM_HEADS * MLSTM_DQK
C_MV = C_MK + MLSTM_HEADS * MLSTM_DQK
C_MO = C_MV + MLSTM_HEADS * MLSTM_DV
C_GQ = C_MO + MLSTM_HEADS * MLSTM_DV
C_GK = C_GQ + GQA_HEADS * HEAD_DIM
C_GV = C_GK + GQA_KV_HEADS * HEAD_DIM
C_NQ = C_GV + GQA_KV_HEADS * HEAD_DIM
C_NK = C_NQ + NAT_HEADS * HEAD_DIM
C_NV = C_NK + NAT_HEADS * HEAD_DIM
C_GATE = C_NV + NAT_HEADS * HEAD_DIM
W_IN_GATE_COL = C_GQ


def _params(sem, vmem_mb=None):
    return pltpu.CompilerParams(
        dimension_semantics=sem,
        vmem_limit_bytes=None if vmem_mb is None else vmem_mb * 2 ** 20)


def _pick(n, cands):
    for c in cands:
        if n % c == 0:
            return c
    raise ValueError(f"no tile for {n} in {cands}")


def _dot(a, b):
    return jnp.dot(a, b, preferred_element_type=F32)


def _dot_nt(a, b):
    return lax.dot_general(a, b, (((1,), (1,)), ((), ())), preferred_element_type=F32)


def _sigmoid(x):
    return 0.5 * jnp.tanh(0.5 * x) + 0.5


def _rms(x):
    return x * lax.rsqrt(jnp.mean(x * x, axis=-1, keepdims=True) + NORM_EPS)


def _mod_kernel(c_ref, w_ref, b_ref, o_ref):
    c = c_ref[...]
    a = (c * _sigmoid(c)).astype(BF16)
    o_ref[0] = _dot(a, w_ref[0].astype(BF16)) + b_ref[0]


def _modulation(cc, w_mod, b_mod):
    L, D, N = w_mod.shape
    tn = _pick(N, (768, 512, 256, 128))
    return pl.pallas_call(
        _mod_kernel,
        grid=(L, N // tn),
        in_specs=[pl.BlockSpec((cc.shape[0], D), lambda l, n: (0, 0)),
                  pl.BlockSpec((1, D, tn), lambda l, n: (l, 0, n)),
                  pl.BlockSpec((1, 1, tn), lambda l, n: (l, 0, n))],
        out_specs=pl.BlockSpec((1, cc.shape[0], tn), lambda l, n: (l, 0, n)),
        out_shape=jax.ShapeDtypeStruct((L, cc.shape[0], N), F32),
        compiler_params=_params(("parallel", "parallel")),
    )(cc, w_mod, b_mod.reshape(L, 1, N))


def _norm_mod_router_kernel(x_ref, g_ref, sh_ref, sc_ref, whi_ref, wlo_ref, o_ref, acol_ref, arow_ref):
    y = _rms(x_ref[0]) * g_ref[...]
    h = y * (1.0 + sc_ref[0]) + sh_ref[0]
    hb = h.astype(BF16)
    o_ref[0] = hb
    hlo = (h - hb.astype(F32)).astype(BF16)
    logits = _dot(hb, whi_ref[...]) + _dot(hlo, whi_ref[...]) + _dot(hb, wlo_ref[...])
    lane = lax.broadcasted_iota(jnp.int32, logits.shape, 1)
    valid = lane < N_EXPERTS
    logits = jnp.where(valid, logits, NEG_INF)
    ex = jnp.where(valid, jnp.exp(logits - jnp.max(logits, axis=1, keepdims=True)), 0.0)
    aff = ex / jnp.sum(ex, axis=1, keepdims=True)
    acol_ref[0] = aff
    arow_ref[0] = aff.T[:N_EXPERTS]


def _norm_mod_router(xc, g, mod3, shift_chunk, scale_chunk, whi, wlo):
    B, Tt, D = xc.shape
    nT = Tt // TR
    tile = pl.BlockSpec((1, TR, D), lambda b, i: (b, i, 0))
    row = lambda b, i: jnp.where(i == nT - 1, B, b)
    in_specs = [tile, pl.BlockSpec((1, D), lambda b, i: (0, 0)),
                pl.BlockSpec((1, 1, D), lambda b, i: (row(b, i), 0, shift_chunk)),
                pl.BlockSpec((1, 1, D), lambda b, i: (row(b, i), 0, scale_chunk))]
    args = [xc, g.reshape(1, D), mod3, mod3]
    wspec = pl.BlockSpec((D, LANES), lambda b, i: (0, 0))
    return pl.pallas_call(
        _norm_mod_router_kernel, grid=(B, nT), in_specs=in_specs + [wspec, wspec],
        out_specs=[tile, pl.BlockSpec((1, TR, LANES), lambda b, i: (b, i, 0)),
                   pl.BlockSpec((1, N_EXPERTS, TR), lambda b, i: (b, 0, i))],
        out_shape=[jax.ShapeDtypeStruct((B, Tt, D), BF16),
                   jax.ShapeDtypeStruct((B, Tt, LANES), F32),
                   jax.ShapeDtypeStruct((B, N_EXPERTS, Tt), F32)],
        compiler_params=_params(("parallel", "parallel")),
    )(*args, whi, wlo)


PROJ_ROW_SPLIT = 4


def _proj_kernel(x_ref, g_ref, shx_ref, scx_ref, shc_ref, scc_ref, w_ref, wg_ref, nw_ref, cos_ref, sin_ref,
                 o_ref, og_ref, h_ref, *, T):
    tm = h_ref.shape[0]
    tn = o_ref.shape[2]
    sub = tm // PROJ_ROW_SPLIT
    n = pl.program_id(2)

    @pl.when(n == 0)
    def _():
        for k in range(PROJ_ROW_SPLIT):
            rows = slice(k * sub, (k + 1) * sub)
            y = _rms(x_ref[0, rows, :]) * g_ref[...]
            tok = pl.program_id(1) * tm + k * sub + lax.broadcasted_iota(jnp.int32, (sub, 1), 0)
            is_ctx = tok >= T
            sc = jnp.where(is_ctx, scc_ref[0], scx_ref[0])
            sh = jnp.where(is_ctx, shc_ref[0], shx_ref[0])
            h = (y * (1.0 + sc) + sh).astype(BF16)
            h_ref[rows, :] = h
            og_ref[0, rows, :] = _dot(h, wg_ref[...])

    acc = _dot(h_ref[...], w_ref[...])
    n_qk = GQA_HEADS + GQA_KV_HEADS
    qk_tiles = range(C_GQ // tn, (C_GQ + n_qk * HEAD_DIM - 1) // tn + 1)

    @pl.when((n < qk_tiles[0]) | (n > qk_tiles[-1]))
    def _():
        o_ref[0] = jnp.where(n >= C_GATE // tn, _sigmoid(acc), acc).astype(o_ref.dtype)

    lane = lax.broadcasted_iota(jnp.int32, (tm, HEAD_DIM), 1)
    first = (lane % (HEAD_DIM // 2)) < HEAD_DIM // 4
    for tile in qk_tiles:
        @pl.when(n == tile)
        def _():
            for c in range(tn // HEAD_DIM):
                cols = slice(c * HEAD_DIM, (c + 1) * HEAD_DIM)
                head = (tile * tn + c * HEAD_DIM - C_GQ) // HEAD_DIM
                blk = acc[:, cols]
                if 0 <= head < n_qk:
                    y = _rms(blk) * nw_ref[:, head * HEAD_DIM:(head + 1) * HEAD_DIM]
                    partner = jnp.where(first, pltpu.roll(y, HEAD_DIM - HEAD_DIM // 4, 1),
                                        pltpu.roll(y, HEAD_DIM // 4, 1))
                    blk = y * cos_ref[...] + partner * sin_ref[...]
                o_ref[0, :, cols] = blk.astype(o_ref.dtype)


def _project(xc, g, mod3, w_main, w_g, nw, cos_t, sin_t, T):
    B, Tt, D = xc.shape
    N = w_main.shape[1]
    tm = Tt // PROJ_ROW_SPLIT
    tn = _pick(math.gcd(N, C_GATE), (768, 512, 256, 128))
    assert C_GQ % tn == 0
    mrow = lambda row, chunk: pl.BlockSpec((1, 1, D), lambda b, i, n: (row(b), 0, chunk))
    own, ctx = (lambda b: b), (lambda b: B)
    rope = pl.BlockSpec((tm, HEAD_DIM), lambda b, i, n: (i, 0))
    return pl.pallas_call(
        functools.partial(_proj_kernel, T=T), grid=(B, PROJ_ROW_SPLIT, N // tn),
        in_specs=[pl.BlockSpec((1, tm, D), lambda b, i, n: (b, i, 0)),
                  pl.BlockSpec((1, D), lambda b, i, n: (0, 0)),
                  mrow(own, 0), mrow(own, 1), mrow(ctx, 0), mrow(ctx, 1),
                  pl.BlockSpec((D, tn), lambda b, i, n: (0, n)),
                  pl.BlockSpec((D, LANES), lambda b, i, n: (0, 0)),
                  pl.BlockSpec((1, nw.size), lambda b, i, n: (0, 0)), rope, rope],
        out_specs=[pl.BlockSpec((1, tm, tn), lambda b, i, n: (b, i, n)),
                   pl.BlockSpec((1, tm, LANES), lambda b, i, n: (b, i, 0))],
        out_shape=[jax.ShapeDtypeStruct((B, Tt, N), BF16), jax.ShapeDtypeStruct((B, Tt, LANES), F32)],
        scratch_shapes=[pltpu.VMEM((tm, D), BF16)],
        compiler_params=_params(("parallel", "parallel", "arbitrary"), 48),
    )(xc, g.reshape(1, D), mod3, mod3, mod3, mod3, w_main, w_g, nw.reshape(1, -1), cos_t, sin_t)


def _flash_kernel(q_ref, k_ref, v_ref, o_ref, m_ref, acc_ref, vaug_ref, *, R, Lc, ck, n_lat_tiles):
    tile = pl.program_id(2)

    m_ref[...] = jnp.full(m_ref.shape, NEG_INF, F32)
    acc_ref[...] = jnp.zeros(acc_ref.shape, F32)

    @pl.when(tile == 0)
    def _():
        vaug_ref[:, :HEAD_DIM] = v_ref[0]
        vaug_ref[:, HEAD_DIM:] = jnp.ones((vaug_ref.shape[0], HEAD_DIM), BF16)

    def step(r0, size):
        k = k_ref[0, pl.ds(r0, size), :]
        vaug = vaug_ref[pl.ds(r0, size), :]
        for r in range(R):
            rows = slice(r * TR, (r + 1) * TR)
            s = _dot_nt(q_ref[0, :, r * HEAD_DIM:(r + 1) * HEAD_DIM], k)
            slabs = [s[:, c * LANES:(c + 1) * LANES] for c in range(s.shape[1] // LANES)]
            m_old = m_ref[rows, :]
            m_new = jnp.maximum(m_old, jnp.max(functools.reduce(jnp.maximum, slabs), axis=1, keepdims=True))
            alpha = jnp.exp(m_old - m_new)
            p = jnp.concatenate([jnp.exp(sl - m_new) for sl in slabs], axis=1).astype(BF16)
            acc_ref[rows, :] = jnp.concatenate([alpha, alpha], axis=1) * acc_ref[rows, :] + _dot(p, vaug)
            m_ref[rows, :] = m_new

    kv_rows = k_ref.shape[1]

    @pl.when(tile >= n_lat_tiles)
    def _():
        step(kv_rows - Lc, Lc)

    @pl.when(tile < n_lat_tiles)
    def _():
        def body(j, carry):
            step(pl.multiple_of(j * ck, ck), ck)
            return carry
        lax.fori_loop(0, kv_rows // ck, body, 0)

    o = acc_ref[:, :HEAD_DIM] / acc_ref[:, HEAD_DIM:]
    for r in range(R):
        o_ref[0, :, r * HEAD_DIM:(r + 1) * HEAD_DIM] = o[r * TR:(r + 1) * TR].astype(o_ref.dtype)


def _flash(q_arr, q_col, k_arr, k_col, v_arr, v_col, *, G, R, T, Lc):
    B, Tt, _ = q_arr.shape
    qw = R * HEAD_DIM
    ck = max(c for c in range(LANES, 2304 + 1, LANES) if Tt % c == 0)
    return pl.pallas_call(
        functools.partial(_flash_kernel, R=R, Lc=Lc, ck=ck, n_lat_tiles=T // TR),
        grid=(B, G, Tt // TR),
        in_specs=[pl.BlockSpec((1, TR, qw), lambda b, g, i: (b, i, q_col // qw + g)),
                  pl.BlockSpec((1, Tt, HEAD_DIM), lambda b, g, i: (b, 0, k_col // HEAD_DIM + g)),
                  pl.BlockSpec((1, Tt, HEAD_DIM), lambda b, g, i: (b, 0, v_col // HEAD_DIM + g))],
        out_specs=pl.BlockSpec((1, TR, qw), lambda b, g, i: (b, i, g)),
        out_shape=jax.ShapeDtypeStruct((B, Tt, G * qw), BF16),
        scratch_shapes=[pltpu.VMEM((R * TR, LANES), F32), pltpu.VMEM((R * TR, 2 * HEAD_DIM), F32),
                        pltpu.VMEM((Tt, 2 * HEAD_DIM), BF16)],
        compiler_params=_params(("parallel", "parallel", "arbitrary"), 48),
    )(q_arr, k_arr, v_arr)


def _nat_geometry(T):
    rows = T // GRID_W
    kh = min(NAT_WIN_ROWS, rows)
    nkr = min(kh + 1, rows)
    qbr = Q_BLOCK // GRID_W
    nbr = rows // qbr
    kb = np.zeros(nbr, np.int32)
    var = np.zeros(nbr, np.int32)
    variants = []
    for j in range(nbr):
        qrow = j * qbr + np.arange(qbr)
        start_r = np.clip(qrow - kh // 2, 0, rows - kh)
        kb[j] = min(start_r[0], rows - nkr)
        sig = (int(kb[j] - j * qbr), tuple(int(s - kb[j]) for s in start_r))
        if sig not in variants:
            variants.append(sig)
        var[j] = variants.index(sig)
    return kh, nkr, qbr, nbr, kb, var, variants


def _nat_bias_kernel(rb_ref, o_ref, *, variants, kh, nkr, qbr):
    h = pl.program_id(0)
    ndr = 2 * NAT_WIN_ROWS - 1
    ndc = 2 * NAT_WIN_COLS - 1
    qc = lax.broadcasted_iota(jnp.int32, (GRID_W, GRID_W), 0)
    kc = lax.broadcasted_iota(jnp.int32, (GRID_W, GRID_W), 1)
    dc = kc - qc + NAT_WIN_COLS - 1
    start_c = jnp.clip(qc - NAT_WIN_COLS // 2, 0, GRID_W - NAT_WIN_COLS)
    col_in = (kc >= start_c) & (kc < start_c + NAT_WIN_COLS)
    masked = jnp.full((GRID_W, GRID_W), NEG_INF, F32)
    toeplitz = []
    for dr in range(ndr):
        t = jnp.zeros((GRID_W, GRID_W), F32)
        for d in range(ndc):
            t = jnp.where(dc == d, rb_ref[(h * ndr + dr) * ndc + d], t)
        toeplitz.append(jnp.where(col_in, t, NEG_INF))
    for v, (delta, srel) in enumerate(variants):
        for qr in range(qbr):
            blocks = []
            for kr in range(nkr):
                if srel[qr] <= kr < srel[qr] + kh:
                    blocks.append(toeplitz[int(np.clip(kr + delta - qr + NAT_WIN_ROWS - 1, 0, ndr - 1))])
                else:
                    blocks.append(masked)
            rows = slice(qr * GRID_W, (qr + 1) * GRID_W)
            for kr in range(0, nkr - 1, 2):
                o_ref[v, 0, rows, kr * GRID_W:(kr + 2) * GRID_W] = jnp.concatenate(blocks[kr:kr + 2], axis=1)
            if nkr % 2:
                o_ref[v, 0, rows, (nkr - 1) * GRID_W:] = blocks[-1]


def _nat_bias_tables(rel_bias, T):
    kh, nkr, qbr, _, _, _, variants = _nat_geometry(T)
    V = len(variants)
    nk = nkr * GRID_W
    return pl.pallas_call(
        functools.partial(_nat_bias_kernel, variants=variants, kh=kh, nkr=nkr, qbr=qbr),
        grid=(NAT_HEADS,),
        in_specs=[pl.BlockSpec(memory_space=pltpu.SMEM)],
        out_specs=pl.BlockSpec((V, 1, Q_BLOCK, nk), lambda h: (0, h, 0, 0)),
        out_shape=jax.ShapeDtypeStruct((V, NAT_HEADS, Q_BLOCK, nk), F32),
        compiler_params=_params(("parallel",)),
    )(rel_bias.reshape(-1))


def _nat_kernel(kb_ref, var_ref, q_ref, k_ref, v_ref, bias_ref, o_ref, s1_ref, s2_ref, *, T, nbr, nk, scale):
    kc = k_ref[0, T:, :]
    vc = v_ref[0, T:, :]

    def scores(j, slot):
        q = q_ref[0, pl.ds(pl.multiple_of(j * Q_BLOCK, Q_BLOCK), Q_BLOCK), :]
        r0 = pl.multiple_of(kb_ref[j] * GRID_W, GRID_W)
        s1_ref[slot] = _dot_nt(q, k_ref[0, pl.ds(r0, nk), :]) * scale + bias_ref[var_ref[j], 0]
        s2_ref[slot] = _dot_nt(q, kc) * scale

    def finish(j, slot):
        s1 = s1_ref[slot]
        s2 = s2_ref[slot]
        r0 = pl.multiple_of(kb_ref[j] * GRID_W, GRID_W)
        m = jnp.maximum(jnp.max(s1, axis=1, keepdims=True), jnp.max(s2, axis=1, keepdims=True))
        p1 = jnp.exp(s1 - m)
        p2 = jnp.exp(s2 - m)
        l = jnp.sum(p1, axis=1, keepdims=True) + jnp.sum(p2, axis=1, keepdims=True)
        o = _dot(p1.astype(BF16), v_ref[0, pl.ds(r0, nk), :]) + _dot(p2.astype(BF16), vc)
        o_ref[0, pl.ds(pl.multiple_of(j * Q_BLOCK, Q_BLOCK), Q_BLOCK), :] = (o / l).astype(o_ref.dtype)

    scores(0, 0)

    def body(i, carry):
        j = 2 * i
        scores(j + 1, 1)
        finish(j, 0)
        scores(jnp.minimum(j + 2, nbr - 1), 0)
        finish(j + 1, 1)
        return carry

    lax.fori_loop(0, nbr // 2, body, 0)

    s = _dot_nt(q_ref[0, T:, :], kc) * scale
    p = jnp.exp(s - jnp.max(s, axis=1, keepdims=True))
    o_ref[0, T:, :] = (_dot(p.astype(BF16), vc) / jnp.sum(p, axis=1, keepdims=True)).astype(o_ref.dtype)


def _nat(P, bias_tab, T, Lc, scale):
    B, Tt, _ = P.shape
    _, nkr, _, nbr, kb, var, _ = _nat_geometry(T)
    nk = nkr * GRID_W
    V = bias_tab.shape[0]
    grid_spec = pltpu.PrefetchScalarGridSpec(
        num_scalar_prefetch=2, grid=(B, NAT_HEADS),
        in_specs=[pl.BlockSpec((1, Tt, HEAD_DIM), lambda b, h, kb_, vr: (b, 0, C_NQ // HEAD_DIM + h)),
                  pl.BlockSpec((1, Tt, HEAD_DIM), lambda b, h, kb_, vr: (b, 0, C_NK // HEAD_DIM + h)),
                  pl.BlockSpec((1, Tt, HEAD_DIM), lambda b, h, kb_, vr: (b, 0, C_NV // HEAD_DIM + h)),
                  pl.BlockSpec((V, 1, Q_BLOCK, nk), lambda b, h, kb_, vr: (0, h, 0, 0))],
        out_specs=pl.BlockSpec((1, Tt, HEAD_DIM), lambda b, h, kb_, vr: (b, 0, h)),
        scratch_shapes=[pltpu.VMEM((2, Q_BLOCK, nk), F32), pltpu.VMEM((2, Q_BLOCK, Lc), F32)])
    assert nbr % 2 == 0
    return pl.pallas_call(
        functools.partial(_nat_kernel, T=T, nbr=nbr, nk=nk, scale=scale),
        grid_spec=grid_spec,
        out_shape=jax.ShapeDtypeStruct((B, Tt, NAT_HEADS * HEAD_DIM), BF16),
        compiler_params=_params(("parallel", "parallel")),
    )(jnp.asarray(kb), jnp.asarray(var), P, P, P, bias_tab)


def _mlstm_kernel(qf_ref, kf_ref, vf_ref, gf_ref, ktf_ref, qb_ref, kb_ref, vb_ref, gb_ref, ktb_ref, bias_ref,
                  hf_ref, hb_ref, st_ref, m_ref, *, kscale):
    L = MLSTM_CHUNK
    dv = MLSTM_DV

    @pl.when(pl.program_id(1) == 0)
    def _():
        st_ref[...] = jnp.zeros(st_ref.shape, F32)
        m_ref[...] = jnp.full(m_ref.shape, NEG_INF, F32)

    ti = lax.broadcasted_iota(jnp.int32, (L, L), 0)
    si = lax.broadcasted_iota(jnp.int32, (L, L), 1)
    ones_col = (lax.broadcasted_iota(jnp.int32, (L, LANES), 1) == 0).astype(BF16)

    for dirn, (q_ref, k_ref, v_ref, g_ref, kt_ref, h_ref) in enumerate(
            ((qf_ref, kf_ref, vf_ref, gf_ref, ktf_ref, hf_ref), (qb_ref, kb_ref, vb_ref, gb_ref, ktb_ref, hb_ref))):
        tri = (si <= ti) if dirn == 0 else (si >= ti)
        G = g_ref[0] + bias_ref[...]
        LF = jnp.minimum(G, 0.0) - jnp.log(1.0 + jnp.exp(-jnp.abs(G)))
        lf_hi = LF.astype(BF16)
        lf_lo = (LF - lf_hi.astype(F32)).astype(BF16)
        trib = tri.astype(BF16)
        Bc = _dot(trib, lf_hi) + _dot(trib, lf_lo)
        BcT = Bc.T
        GT = G.T
        bl_row = Bc[L - 1:L, :] if dirn == 0 else Bc[0:1, :]
        for h in range(MLSTM_HEADS):
            ci = dirn * 2 * MLSTM_HEADS + h
            cf = ci + MLSTM_HEADS
            bcol = Bc[:, cf:cf + 1]
            brow = BcT[cf:cf + 1, :]
            licol = G[:, ci:ci + 1]
            lirow = GT[ci:ci + 1, :]
            bl = bl_row[:, cf:cf + 1]
            m_old = m_ref[dirn, h]
            gcol = bl - bcol + licol
            m_new = jnp.maximum(bl + m_old, jnp.max(gcol, axis=0, keepdims=True))
            decay = jnp.exp(bl + m_old - m_new)
            wk = jnp.exp(gcol - m_new)
            dmat = jnp.where(tri, bcol - brow + lirow, NEG_INF)
            inter = bcol + m_old
            m_t = jnp.maximum(inter, jnp.max(dmat, axis=1, keepdims=True))
            w = jnp.exp(dmat - m_t)
            a = jnp.exp(inter - m_t)
            qh = q_ref[0, :, h * MLSTM_DQK:(h + 1) * MLSTM_DQK]
            kh = k_ref[0, :, h * MLSTM_DQK:(h + 1) * MLSTM_DQK]
            vaug = jnp.concatenate([v_ref[0, :, h * dv:(h + 1) * dv], ones_col], axis=1)
            smat = _dot_nt(qh, kh) * kscale * w
            state = st_ref[dirn, h]
            nd = _dot(smat.astype(BF16), vaug) + a * _dot(qh, state.astype(BF16))
            den = jnp.maximum(jnp.abs(nd[:, dv:dv + 1]), jnp.exp(-m_t))
            h_ref[0, :, h * dv:(h + 1) * dv] = (nd[:, :dv] / den).astype(h_ref.dtype)
            kT = kt_ref[0, h * MLSTM_DQK:(h + 1) * MLSTM_DQK, :]
            upd = _dot(kT, (wk * vaug.astype(F32)).astype(BF16)) * kscale
            st_ref[dirn, h] = decay * state + upd
            m_ref[dirn, h] = m_new


def _mlstm(P, gates, gate_bias, T, Lc):
    B, Tt, _ = P.shape
    L = MLSTM_CHUNK
    kt = jnp.swapaxes(P[:, :, C_MK:C_MK + MLSTM_HEADS * MLSTM_DQK], 1, 2)
    ncl, ncc = T // L, Lc // L
    fwd = lambda i: jnp.where(i < ncc, ncl + i, i - ncc)
    bwd = lambda i: jnp.where(i < ncc, ncl + ncc - 1 - i, ncl - 1 - (i - ncc))
    qw = MLSTM_HEADS * MLSTM_DQK
    vw = MLSTM_HEADS * MLSTM_DV

    def specs(order):
        return [pl.BlockSpec((1, L, qw), lambda b, i: (b, order(i), C_MQ // qw)),
                pl.BlockSpec((1, L, qw), lambda b, i: (b, order(i), C_MK // qw)),
                pl.BlockSpec((1, L, vw), lambda b, i: (b, order(i), C_MV // vw)),
                pl.BlockSpec((1, L, LANES), lambda b, i: (b, order(i), 0)),
                pl.BlockSpec((1, qw, L), lambda b, i: (b, 0, order(i)))]

    out = jax.ShapeDtypeStruct((B, Tt, vw), BF16)
    return pl.pallas_call(
        functools.partial(_mlstm_kernel, kscale=MLSTM_DQK ** -0.5),
        grid=(B, ncl + ncc),
        in_specs=specs(fwd) + specs(bwd) + [pl.BlockSpec((1, LANES), lambda b, i: (0, 0))],
        out_specs=[pl.BlockSpec((1, L, vw), lambda b, i: (b, fwd(i), 0)),
                   pl.BlockSpec((1, L, vw), lambda b, i: (b, bwd(i), 0))],
        out_shape=[out, out],
        scratch_shapes=[pltpu.VMEM((2, MLSTM_HEADS, MLSTM_DQK, MLSTM_DV + LANES), F32),
                        pltpu.VMEM((2, MLSTM_HEADS, 1, 1), F32)],
        compiler_params=_params(("parallel", "arbitrary")),
    )(P, P, P, gates, kt, P, P, P, gates, kt, gate_bias)


def _merge_kernel(hf_ref, hb_ref, mo_ref, yg_ref, yn_ref, g0_ref, g1_ref, g2_ref, wb_ref, o_ref, ym_ref):
    @pl.when(pl.program_id(1) == 0)
    def _():
        h = hf_ref[...].astype(F32) + hb_ref[...].astype(F32)
        for hh in range(MLSTM_HEADS):
            cols = slice(hh * MLSTM_DV, (hh + 1) * MLSTM_DV)
            ym_ref[:, cols] = (_sigmoid(mo_ref[:, cols].astype(F32)) * _rms(h[:, cols])).astype(BF16)

    gate = lambda r: r[...].astype(F32)
    acc = gate(g0_ref) * _dot(ym_ref[...], wb_ref[0])
    acc += gate(g1_ref) * _dot(yg_ref[...], wb_ref[1])
    acc += gate(g2_ref) * _dot(yn_ref[...], wb_ref[2])
    o_ref[...] = acc.astype(BF16)


def _merge(hf, hb, P2, yg, yn, wb):
    M, W = hf.shape
    D = wb.shape[2]
    tm = _pick(M, (1024, 512, 256))
    tn = _pick(D, (512, 256))
    row = lambda c: pl.BlockSpec((tm, W), lambda i, n: (i, c))
    gate = lambda br: pl.BlockSpec((tm, tn), lambda i, n: (i, (C_GATE + br * D) // tn + n))
    return pl.pallas_call(
        _merge_kernel, grid=(M // tm, D // tn),
        in_specs=[row(0), row(0), row(C_MO // W), row(0), row(0), gate(0), gate(1), gate(2),
                  pl.BlockSpec((N_BRANCHES, W, tn), lambda i, n: (0, 0, n))],
        out_specs=pl.BlockSpec((tm, tn), lambda i, n: (i, n)),
        out_shape=jax.ShapeDtypeStruct((M, D), BF16),
        scratch_shapes=[pltpu.VMEM((tm, W), BF16)],
        compiler_params=_params(("parallel", "arbitrary"), 56),
    )(hf, hb, P2, yg, yn, P2, P2, P2, wb)


def _wout_post_kernel(a_ref, w_ref, x_ref, gp_ref, gate_ref, o_ref):
    y = _dot(a_ref[0], w_ref[...])
    o_ref[0] = x_ref[0] + gate_ref[0] * (_rms(y) * gp_ref[...])


def _wout_post(merged, w_out, xc, g_post, mod3, gate_chunk):
    B, Tt, D = xc.shape
    nT = Tt // TR
    tile = pl.BlockSpec((1, TR, D), lambda b, i: (b, i, 0))
    row = lambda b, i: jnp.where(i == nT - 1, B, b)
    return pl.pallas_call(
        _wout_post_kernel, grid=(B, nT),
        in_specs=[tile, pl.BlockSpec((D, D), lambda b, i: (0, 0)), tile,
                  pl.BlockSpec((1, D), lambda b, i: (0, 0)),
                  pl.BlockSpec((1, 1, D), lambda b, i: (row(b, i), 0, gate_chunk))],
        out_specs=tile,
        out_shape=jax.ShapeDtypeStruct((B, Tt, D), F32),
        compiler_params=_params(("parallel", "parallel"), 48),
    )(merged, w_out, xc, g_post.reshape(1, D), mod3)


def _route_segment(arow, acol, k, chunk):
    E, T = arow.shape
    nchunk = T // chunk
    ind = lambda m: jnp.where(m, 1.0, 0.0)
    bits = pltpu.bitcast(arow, jnp.int32)
    thr = jnp.zeros((E, 1), jnp.int32)
    for bit in range(30, -1, -1):
        cand = thr | (1 << bit)
        cnt = jnp.sum(ind(bits >= cand), axis=1, keepdims=True)
        thr = jnp.where(cnt >= k, cand, thr)
    thr_val = pltpu.bitcast(thr, F32)
    gt = arow > thr_val
    eq = arow == thr_val
    need = k - jnp.sum(ind(gt), axis=1, keepdims=True)

    r_i = lax.broadcasted_iota(jnp.int32, (chunk, chunk), 0)
    c_i = lax.broadcasted_iota(jnp.int32, (chunk, chunk), 1)
    before = ind(r_i < c_i).astype(BF16)
    after = ind(c_i < r_i).astype(BF16)

    def prefix_rows(mask):
        carry = jnp.zeros((E, 1), F32)
        parts, carries = [], [carry]
        for c in range(nchunk):
            mc = ind(mask[:, c * chunk:(c + 1) * chunk]).astype(BF16)
            parts.append(_dot(mc, before) + carry)
            carry = carry + jnp.sum(mc.astype(F32), axis=1, keepdims=True)
            carries.append(carry)
        return jnp.concatenate(parts, axis=1), carries

    tie_rank, _ = prefix_rows(eq)
    sel = gt | (eq & (tie_rank < need))
    pos, carries = prefix_rows(sel)
    prow = jnp.where(sel, pos, -1.0)
    lane = lax.broadcasted_iota(jnp.int32, (E, LANES), 1)
    starts = jnp.zeros((E, LANES), F32)
    for c, cv in enumerate(carries):
        starts = jnp.where(lane == c, cv, starts)

    sub = lax.broadcasted_iota(jnp.int32, (E, LANES), 0)
    diag = sub == lane
    thr_row = jnp.sum(jnp.where(diag, thr_val, 0.0), axis=0, keepdims=True)
    need_row = jnp.sum(jnp.where(diag, need, 0.0), axis=0, keepdims=True)
    lane_ok = lax.broadcasted_iota(jnp.int32, (1, LANES), 1) < E
    gt_c = (acol > thr_row) & lane_ok
    eq_c = (acol == thr_row) & lane_ok

    def prefix_cols(mask):
        carry = jnp.zeros((1, LANES), F32)
        parts = []
        for c in range(nchunk):
            mc = ind(mask[c * chunk:(c + 1) * chunk]).astype(BF16)
            parts.append(_dot(after, mc) + carry)
            carry = carry + jnp.sum(mc.astype(F32), axis=0, keepdims=True)
        return jnp.concatenate(parts, axis=0)

    sel_c = gt_c | (eq_c & (prefix_cols(eq_c) < need_row))
    pcol = jnp.where(sel_c, prefix_cols(sel_c), -1.0)
    return prow, pcol, starts


def _route_kernel(arow_ref, acol_ref, prow_ref, pcol_ref, starts_ref, *, T, Lc, cap_l, cap_c):
    prow, pcol, starts = _route_segment(arow_ref[0, :, :T], acol_ref[0, :T], cap_l, TR)
    prow_ref[0, :, :T] = prow
    pcol_ref[0, :T] = pcol
    starts_ref[0] = starts.astype(jnp.int32)
    prow, pcol, _ = _route_segment(arow_ref[0, :, T:], acol_ref[0, T:], cap_c, TR)
    prow_ref[0, :, T:] = prow
    pcol_ref[0, T:] = pcol


def _route(arow, acol, T, Lc, cap_l, cap_c):
    B, E, Tt = arow.shape
    return pl.pallas_call(
        functools.partial(_route_kernel, T=T, Lc=Lc, cap_l=cap_l, cap_c=cap_c),
        grid=(B,),
        in_specs=[pl.BlockSpec((1, E, Tt), lambda b: (b, 0, 0)), pl.BlockSpec((1, Tt, LANES), lambda b: (b, 0, 0))],
        out_specs=[pl.BlockSpec((1, E, Tt), lambda b: (b, 0, 0)), pl.BlockSpec((1, Tt, LANES), lambda b: (b, 0, 0)),
                   pl.BlockSpec((1, E, LANES), lambda b: (b, 0, 0))],
        out_shape=[jax.ShapeDtypeStruct((B, E, Tt), F32), jax.ShapeDtypeStruct((B, Tt, LANES), F32),
                   jax.ShapeDtypeStruct((B, E, LANES), jnp.int32)],
        compiler_params=_params(("parallel",), 48),
    )(arow, acol)


def _gather_kernel(st_ref, p_ref, a_ref, h_ref, o_ref, g_ref, acc_ref, gacc_ref, *, chunk0, nchunk, cap, st):
    b = pl.program_id(0)
    e = pl.program_id(1)
    base = (b * N_EXPERTS + e) * (nchunk + 1)
    for S in range(cap // st):
        acc_ref[...] = jnp.zeros(acc_ref.shape, F32)
        gacc_ref[...] = jnp.zeros(gacc_ref.shape, F32)
        slot = (lax.broadcasted_iota(jnp.int32, (st, TR), 0) + S * st).astype(F32)

        def body(c, carry):
            lo = st_ref[base + c]
            hi = st_ref[base + c + 1]

            @pl.when((lo < (S + 1) * st) & (hi > S * st))
            def _():
                hit = p_ref[0, pl.ds(chunk0 + c, 1), :] == slot
                r0 = pl.multiple_of(c * TR, TR)
                acc_ref[...] += _dot(jnp.where(hit, 1.0, 0.0).astype(BF16), h_ref[0, pl.ds(r0, TR), :])
                gacc_ref[...] += jnp.sum(jnp.where(hit, a_ref[0, pl.ds(chunk0 + c, 1), :], 0.0),
                                         axis=1, keepdims=True)
            return carry

        lax.fori_loop(0, nchunk, body, 0)
        o_ref[0, S * st:(S + 1) * st, :] = acc_ref[...].astype(BF16)
        g_ref[0, S * st:(S + 1) * st, :] = gacc_ref[...]


def _gather(starts, prow3, arow3, hbf, prev, *, S_tot, chunk0, nchunk, cap, slot_blk0):
    B, Tt, D = hbf.shape
    E = N_EXPERTS
    nT = Tt // TR
    st = min(cap, 128)
    rows = nchunk * TR
    pspec = pl.BlockSpec((1, nT, TR), lambda b, e, s: (b * E + e, 0, 0))
    in_specs = [pspec, pspec, pl.BlockSpec((1, rows, D), lambda b, e, s: (b, chunk0 * TR // rows, 0))]
    args = [prow3, arow3, hbf]
    kern = functools.partial(_gather_kernel, chunk0=chunk0, nchunk=nchunk, cap=cap, st=st)
    aliases = {}
    if prev is not None:
        in_specs += [pl.BlockSpec(memory_space=pl.ANY)] * 2
        args += list(prev)
        aliases = {4: 0, 5: 1}
        inner = kern
        kern = lambda s, p, a, h, _x, _g, o, g, acc, gacc: inner(s, p, a, h, o, g, acc, gacc)
    grid_spec = pltpu.PrefetchScalarGridSpec(
        num_scalar_prefetch=1, grid=(B, E), in_specs=in_specs,
        out_specs=[pl.BlockSpec((1, cap, D), lambda b, e, s: (e, slot_blk0 + b, 0)),
                   pl.BlockSpec((1, cap, LANES), lambda b, e, s: (e, slot_blk0 + b, 0))],
        scratch_shapes=[pltpu.VMEM((st, D), F32), pltpu.VMEM((st, LANES), F32)])
    return pl.pallas_call(
        kern, grid_spec=grid_spec,
        out_shape=[jax.ShapeDtypeStruct((E, S_tot, D), BF16), jax.ShapeDtypeStruct((E, S_tot, LANES), F32)],
        input_output_aliases=aliases,
        compiler_params=_params(("parallel", "arbitrary"), 56),
    )(starts, *args)


def _ffn1_kernel(x_ref, wg_ref, wu_ref, o_ref, wgb_ref, wub_ref):
    @pl.when(pl.program_id(2) == 0)
    def _():
        wgb_ref[...] = wg_ref[0].astype(BF16)
        wub_ref[...] = wu_ref[0].astype(BF16)
    x = x_ref[0]
    g = _dot(x, wgb_ref[...])
    u = _dot(x, wub_ref[...])
    o_ref[0] = (g * _sigmoid(g) * u).astype(BF16)


def _ffn2_kernel(x_ref, w_ref, g_ref, o_ref, wb_ref):
    @pl.when(pl.program_id(2) == 0)
    def _():
        wb_ref[...] = w_ref[0].astype(BF16)
    o_ref[0] = (_dot(x_ref[0], wb_ref[...]) * g_ref[0, :, :1]).astype(BF16)


def _expert_matmul(kernel, x, ws, layer, row_scale=None):
    E, S, K = x.shape
    N = ws[0].shape[3]
    ws = [w.reshape(-1, K, N) for w in ws]
    tm = _pick(S, (1088, 1024, 544, 512, 272, 256, 160, 128))
    tn = _pick(N, (512, 256))
    wspec = pl.BlockSpec((1, K, tn), lambda e, n, m: (layer * E + e, 0, n))
    in_specs = [pl.BlockSpec((1, tm, K), lambda e, n, m: (e, m, 0))] + [wspec] * len(ws)
    args = [x, *ws]
    if row_scale is not None:
        in_specs.append(pl.BlockSpec((1, tm, LANES), lambda e, n, m: (e, m, 0)))
        args.append(row_scale)
    return pl.pallas_call(
        kernel, grid=(E, N // tn, S // tm),
        in_specs=in_specs,
        out_specs=pl.BlockSpec((1, tm, tn), lambda e, n, m: (e, m, n)),
        out_shape=jax.ShapeDtypeStruct((E, S, N), BF16),
        scratch_shapes=[pltpu.VMEM((K, tn), BF16)] * len(ws),
        compiler_params=_params(("parallel", "parallel", "arbitrary"), 48),
    )(*args)


SLOT_ALIGN = 16
COMBINE_WINDOW = 64


def _combine_kernel(st_ref, pc_ref, ye_ref, x_ref, gp_ref, gate_ref, o_ref, win_ref, acc_ref, sem,
                    *, nchunk, cap, W, slot0):
    E = N_EXPERTS
    b = pl.program_id(0)
    t = pl.program_id(1)
    nb, nt = pl.num_programs(0), pl.num_programs(1)
    step = b * nt + t
    buf = step % 2

    def bounds(bb, tt, e):
        base = (bb * E + e) * (nchunk + 1) + tt
        return (st_ref[base] // SLOT_ALIGN) * SLOT_ALIGN, st_ref[base + 1]

    def window(bb, tt, e, r):
        first, _ = bounds(bb, tt, e)
        want = first + r * W
        return want, jnp.minimum(want, cap - W)

    def copy(bb, tt, e, r, into):
        _, start = window(bb, tt, e, r)
        row0 = pl.multiple_of(slot0 + bb * cap + start, SLOT_ALIGN)
        return pltpu.make_async_copy(ye_ref.at[e, pl.ds(row0, W), :],
                                     win_ref.at[into, pl.ds(e * W, W), :], sem.at[into, e])

    @pl.when(step == 0)
    def _():
        for e in range(E):
            copy(b, t, e, 0, buf).start()

    @pl.when(step + 1 < nb * nt)
    def _():
        last_t = t + 1 == nt
        b_next = jnp.where(last_t, b + 1, b)
        t_next = jnp.where(last_t, 0, t + 1)
        for e in range(E):
            copy(b_next, t_next, e, 0, 1 - buf).start()

    rounds = jnp.int32(1)
    for e in range(E):
        first, hi = bounds(b, t, e)
        rounds = jnp.maximum(rounds, (hi - first + W - 1) // W)

    pos1 = pc_ref[0] + 1.0
    pos_hi = jnp.floor(pos1 * (1.0 / SLOT_ALIGN))
    pos_lo = pos1 - pos_hi * SLOT_ALIGN
    col = lax.broadcasted_iota(jnp.int32, (LANES, E * W), 1)
    row = lax.broadcasted_iota(jnp.int32, (LANES, E * W), 0)
    expand = jnp.where(col // W == row, 1.0, 0.0).astype(BF16)
    pos_rep = SLOT_ALIGN * _dot(pos_hi.astype(BF16), expand) + _dot(pos_lo.astype(BF16), expand) - 1.0
    lane = lax.broadcasted_iota(jnp.int32, (1, E * W), 1)
    in_win = (lane % W).astype(F32)

    def onehot(r):
        want_row = jnp.zeros((1, E * W), F32)
        start_row = jnp.zeros((1, E * W), F32)
        for e in range(E):
            want, start = window(b, t, e, r)
            mine = lane // W == e
            want_row = jnp.where(mine, want.astype(F32), want_row)
            start_row = jnp.where(mine, start.astype(F32), start_row)
        hit = (pos_rep - start_row == in_win) & (pos_rep >= want_row)
        return jnp.where(hit, 1.0, 0.0).astype(BF16)

    lhs = onehot(0)
    for e in range(E):
        copy(b, t, e, 0, buf).wait()
    acc_ref[...] = _dot(lhs, win_ref[buf])

    def more(r, carry):
        for e in range(E):
            copy(b, t, e, r, buf).start()
        lhs_r = onehot(r)
        for e in range(E):
            copy(b, t, e, r, buf).wait()
        acc_ref[...] += _dot(lhs_r, win_ref[buf])
        return carry

    lax.fori_loop(1, rounds, more, 0)
    o_ref[0] = x_ref[0] + gate_ref[0] * (_rms(acc_ref[...]) * gp_ref[...])


def _combine(starts, pcol, ye, xc, g_post, mod3, gate_chunk, *, tile0, ntile, nchunk, cap, slot0,
             mod_row_ctx, out):
    B, Tt, D = xc.shape
    E = N_EXPERTS
    W = min(cap, COMBINE_WINDOW)
    fresh = out is not None
    if not fresh:
        out = xc
    tok = lambda w: pl.BlockSpec((1, TR, w), lambda b, t, s: (b, tile0 + t, 0))
    mrow = (lambda b: B) if mod_row_ctx else (lambda b: b)
    in_specs = [tok(LANES), pl.BlockSpec(memory_space=pl.ANY), tok(D),
                pl.BlockSpec((1, D), lambda b, t, s: (0, 0)),
                pl.BlockSpec((1, 1, D), lambda b, t, s: (mrow(b), 0, gate_chunk))]
    args = [pcol, ye, xc, g_post.reshape(1, D), mod3]
    kern = functools.partial(_combine_kernel, nchunk=nchunk, cap=cap, W=W, slot0=slot0)
    if fresh:
        out_spec = pl.BlockSpec((1, TR, D), lambda b, t, s: (b, t, 0))
        aliases = {}
    else:
        out_spec = tok(D)
        aliases = {3: 0}
    grid_spec = pltpu.PrefetchScalarGridSpec(
        num_scalar_prefetch=1, grid=(B, ntile), in_specs=in_specs, out_specs=out_spec,
        scratch_shapes=[pltpu.VMEM((2, E * W, D), BF16), pltpu.VMEM((TR, D), F32),
                        pltpu.SemaphoreType.DMA((2, E))])
    return pl.pallas_call(
        kern, grid_spec=grid_spec,
        out_shape=jax.ShapeDtypeStruct(out.shape, F32),
        input_output_aliases=aliases,
        compiler_params=_params(("arbitrary", "arbitrary"), 48),
    )(starts, *args)


def _rope_tables(T, Lc):
    nf = HEAD_DIM // 4
    t = np.arange(T)
    inv = ROPE_THETA ** (-jnp.arange(nf, dtype=F32) / nf)
    row = jnp.asarray(t // GRID_W, F32)
    col = jnp.asarray(t % GRID_W, F32)
    ang = jnp.stack([row[:, None] * inv, col[:, None] * inv], axis=1)
    cos = jnp.broadcast_to(jnp.cos(ang)[:, :, None, :], (T, 2, 2, nf)).reshape(T, HEAD_DIM)
    sin = jnp.broadcast_to(jnp.sin(ang)[:, :, None, :], (T, 2, 2, nf)).reshape(T, HEAD_DIM)
    sign = jnp.asarray(np.tile(np.repeat([-1.0, 1.0], nf), 2), F32)
    cos = jnp.concatenate([cos, jnp.ones((Lc, HEAD_DIM), F32)], axis=0)
    sin = jnp.concatenate([sin * sign, jnp.zeros((Lc, HEAD_DIM), F32)], axis=0)
    return cos, sin


def _split_bf16(w):
    hi = w.astype(BF16)
    return hi, (w - hi.astype(F32)).astype(BF16)


def kernel(x, c, ctx, c_ctx, w_mod, b_mod, g_pre_mix, g_post_mix, g_pre_ffn, g_post_ffn, w_in, mlstm_gate_bias, gqa_q_norm, gqa_k_norm, nat_rel_bias, w_branch, w_out, w_router, w_expert_gate, w_expert_up, w_expert_down):
    B, T, D = x.shape
    Lc = ctx.shape[1]
    depth = w_mod.shape[0]
    E = N_EXPERTS
    assert Lc == TR and T % (2 * TR) == 0 and w_router.shape[2] == E
    Tt = T + Lc
    nT = Tt // TR
    M = B * Tt
    cap_l = EC_CAPACITY_FACTOR * T // E
    cap_c = EC_CAPACITY_FACTOR * Lc // E
    scale = HEAD_DIM ** -0.5
    nchunk = T // TR

    xc = jnp.concatenate([x, ctx], axis=1)
    rows_c = -(-(B + 1) // 8) * 8
    cc = jnp.zeros((rows_c, D), F32).at[:B].set(c).at[B].set(c_ctx)
    mod = _modulation(cc, w_mod, b_mod)
    cos_t, sin_t = _rope_tables(T, Lc)

    for l in range(depth):
        last = l == depth - 1
        mod3 = mod[l].reshape(rows_c, 1, N_MOD * D)
        w_l = w_in[l]
        w_main = jnp.concatenate([w_l[:, :W_IN_GATE_COL], w_l[:, W_IN_GATE_COL + N_GATE_COLS:]], axis=1).astype(BF16)
        w_g = jnp.pad(w_l[:, W_IN_GATE_COL:W_IN_GATE_COL + N_GATE_COLS], ((0, 0), (0, LANES - N_GATE_COLS))).astype(BF16)
        gate_bias = jnp.pad(mlstm_gate_bias[l], (0, LANES - N_GATE_COLS)).reshape(1, LANES)
        nw = jnp.concatenate([jnp.tile(gqa_q_norm[l][None] * scale, (GQA_HEADS, 1)),
                              jnp.tile(gqa_k_norm[l][None], (GQA_KV_HEADS, 1))], axis=0)[:, None, :]

        P, gates = _project(xc, g_pre_mix[l], mod3, w_main, w_g, nw, cos_t, sin_t, T)
        P2 = P.reshape(M, -1)

        hf, hb = _mlstm(P, gates, gate_bias, T, Lc)

        yg = _flash(P, C_GQ, P, C_GK, P, C_GV, G=GQA_KV_HEADS, R=GQA_HEADS // GQA_KV_HEADS, T=T, Lc=Lc)

        bias_tab = _nat_bias_tables(nat_rel_bias[l], T)
        yn = _nat(P, bias_tab, T, Lc, scale)

        merged = _merge(hf.reshape(M, -1), hb.reshape(M, -1), P2, yg.reshape(M, -1), yn.reshape(M, -1),
                        w_branch[l].astype(BF16))
        xc = _wout_post(merged.reshape(B, Tt, D), w_out[l].astype(BF16), xc, g_post_mix[l], mod3, 2)

        wr = jnp.pad(w_router[l], ((0, 0), (0, LANES - E)))
        hx, acol, arow = _norm_mod_router(xc, g_pre_ffn[l], mod3, 3, 4, *_split_bf16(wr))
        prow, pcol, starts = _route(arow, acol, T, Lc, cap_l, cap_c)
        starts_flat = starts[:, :, :nchunk + 1].reshape(-1)
        prow3 = prow.reshape(B * E, nT, TR)
        arow3 = arow.reshape(B * E, nT, TR)

        S = B * cap_l + (0 if last else B * cap_c)
        slots = None if last else (jnp.zeros((E, S, D), BF16), jnp.zeros((E, S, LANES), F32))
        slots = _gather(starts_flat, prow3, arow3, hx, slots, S_tot=S, chunk0=0, nchunk=nchunk, cap=cap_l,
                        slot_blk0=0)
        ctx_starts = jnp.tile(jnp.asarray([0, cap_c], jnp.int32), B * E)
        if not last:
            slots = _gather(ctx_starts, prow3, arow3, hx, slots, S_tot=S, chunk0=nchunk, nchunk=1, cap=cap_c,
                            slot_blk0=B * cap_l // cap_c)
        xs, gs = slots
        hid = _expert_matmul(_ffn1_kernel, xs, (w_expert_gate, w_expert_up), l)
        ye = _expert_matmul(_ffn2_kernel, hid, (w_expert_down,), l, row_scale=gs)

        if last:
            return _combine(starts_flat, pcol, ye, xc, g_post_ffn[l], mod3, 5, tile0=0, ntile=nchunk,
                            nchunk=nchunk, cap=cap_l, slot0=0, mod_row_ctx=False,
                            out=jax.ShapeDtypeStruct((B, T, D), F32))
        xc = _combine(starts_flat, pcol, ye, xc, g_post_ffn[l], mod3, 5, tile0=0, ntile=nchunk,
                      nchunk=nchunk, cap=cap_l, slot0=0, mod_row_ctx=False, out=None)
        xc = _combine(ctx_starts, pcol, ye, xc, g_post_ffn[l], mod3, 5, tile0=nT - 1, ntile=1,
                      nchunk=1, cap=cap_c, slot0=B * cap_l, mod_row_ctx=True, out=None)
```

```python
import functools
import math

import numpy as np
import jax
import jax.numpy as jnp
from jax import lax
from jax.experimental import pallas as pl
from jax.experimental.pallas import tpu as pltpu

F32 = jnp.float32
BF16 = jnp.bfloat16

LANES = 128
HEAD_DIM = 128
MLSTM_HEADS = 4
MLSTM_DQK = 128
MLSTM_DV = 256
MLSTM_CHUNK = 256
GQA_HEADS = 8
GQA_KV_HEADS = 2
NAT_HEADS = 8
NAT_WIN_ROWS = 8
NAT_WIN_COLS = 16
GRID_W = 64
Q_BLOCK = 128
N_EXPERTS = 16
EC_CAPACITY_FACTOR = 2
ROPE_THETA = 10000.0
NORM_EPS = 1e-6
NEG_INF = -1e30
N_MOD = 6
N_BRANCHES = 3
BRANCH_WIDTH = 1024
TR = 256
N_GATE_COLS = 4 * MLSTM_HEADS

C_MQ = 0
C_MK = C_MQ + MLSTM_HEADS * MLSTM_DQK
C_MV = C_MK + MLSTM_HEADS * MLSTM_DQK
C_MO = C_MV + MLSTM_HEADS * MLSTM_DV
C_GQ = C_MO + MLSTM_HEADS * MLSTM_DV
C_GK = C_GQ + GQA_HEADS * HEAD_DIM
C_GV = C_GK + GQA_KV_HEADS * HEAD_DIM
C_NQ = C_GV + GQA_KV_HEADS * HEAD_DIM
C_NK = C_NQ + NAT_HEADS * HEAD_DIM
C_NV = C_NK + NAT_HEADS * HEAD_DIM
C_GATE = C_NV + NAT_HEADS * HEAD_DIM
W_IN_GATE_COL = C_GQ


def _params(sem, vmem_mb=None):
    return pltpu.CompilerParams(
        dimension_semantics=sem,
        vmem_limit_bytes=None if vmem_mb is None else vmem_mb * 2 ** 20)


def _pick(n, cands):
    for c in cands:
        if n % c == 0:
            return c
    raise ValueError(f"no tile for {n} in {cands}")


def _dot(a, b):
    return jnp.dot(a, b, preferred_element_type=F32)


def _dot_nt(a, b):
    return lax.dot_general(a, b, (((1,), (1,)), ((), ())), preferred_element_type=F32)


def _sigmoid(x):
    return 0.5 * jnp.tanh(0.5 * x) + 0.5


def _rms(x):
    return x * lax.rsqrt(jnp.mean(x * x, axis=-1, keepdims=True) + NORM_EPS)


def _mod_kernel(c_ref, w_ref, b_ref, o_ref):
    c = c_ref[...]
    a = (c * _sigmoid(c)).astype(BF16)
    o_ref[0] = _dot(a, w_ref[0].astype(BF16)) + b_ref[0]


def _modulation(cc, w_mod, b_mod):
    L, D, N = w_mod.shape
    tn = _pick(N, (768, 512, 256, 128))
    return pl.pallas_call(
        _mod_kernel,
        grid=(L, N // tn),
        in_specs=[pl.BlockSpec((cc.shape[0], D), lambda l, n: (0, 0)),
                  pl.BlockSpec((1, D, tn), lambda l, n: (l, 0, n)),
                  pl.BlockSpec((1, 1, tn), lambda l, n: (l, 0, n))],
        out_specs=pl.BlockSpec((1, cc.shape[0], tn), lambda l, n: (l, 0, n)),
        out_shape=jax.ShapeDtypeStruct((L, cc.shape[0], N), F32),
        compiler_params=_params(("parallel", "parallel")),
    )(cc, w_mod, b_mod.reshape(L, 1, N))


def _norm_mod_router_kernel(x_ref, g_ref, sh_ref, sc_ref, whi_ref, wlo_ref, o_ref, acol_ref, arow_ref):
    y = _rms(x_ref[0]) * g_ref[...]
    h = y * (1.0 + sc_ref[0]) + sh_ref[0]
    hb = h.astype(BF16)
    o_ref[0] = hb
    hlo = (h - hb.astype(F32)).astype(BF16)
    logits = _dot(hb, whi_ref[...]) + _dot(hlo, whi_ref[...]) + _dot(hb, wlo_ref[...])
    lane = lax.broadcasted_iota(jnp.int32, logits.shape, 1)
    valid = lane < N_EXPERTS
    logits = jnp.where(valid, logits, NEG_INF)
    ex = jnp.where(valid, jnp.exp(logits - jnp.max(logits, axis=1, keepdims=True)), 0.0)
    aff = ex / jnp.sum(ex, axis=1, keepdims=True)
    acol_ref[0] = aff
    arow_ref[0] = aff.T[:N_EXPERTS]


def _norm_mod_router(xc, g, mod3, shift_chunk, scale_chunk, whi, wlo):
    B, Tt, D = xc.shape
    nT = Tt // TR
    tile = pl.BlockSpec((1, TR, D), lambda b, i: (b, i, 0))
    row = lambda b, i: jnp.where(i == nT - 1, B, b)
    in_specs = [tile, pl.BlockSpec((1, D), lambda b, i: (0, 0)),
                pl.BlockSpec((1, 1, D), lambda b, i: (row(b, i), 0, shift_chunk)),
                pl.BlockSpec((1, 1, D), lambda b, i: (row(b, i), 0, scale_chunk))]
    args = [xc, g.reshape(1, D), mod3, mod3]
    wspec = pl.BlockSpec((D, LANES), lambda b, i: (0, 0))
    return pl.pallas_call(
        _norm_mod_router_kernel, grid=(B, nT), in_specs=in_specs + [wspec, wspec],
        out_specs=[tile, pl.BlockSpec((1, TR, LANES), lambda b, i: (b, i, 0)),
                   pl.BlockSpec((1, N_EXPERTS, TR), lambda b, i: (b, 0, i))],
        out_shape=[jax.ShapeDtypeStruct((B, Tt, D), BF16),
                   jax.ShapeDtypeStruct((B, Tt, LANES), F32),
                   jax.ShapeDtypeStruct((B, N_EXPERTS, Tt), F32)],
        compiler_params=_params(("parallel", "parallel")),
    )(*args, whi, wlo)


PROJ_ROW_SPLIT = 4


def _proj_kernel(x_ref, g_ref, shx_ref, scx_ref, shc_ref, scc_ref, w_ref, wg_ref, o_ref, og_ref, h_ref, *, T):
    tm = h_ref.shape[0]
    tn = o_ref.shape[2]
    sub = tm // PROJ_ROW_SPLIT
    n = pl.program_id(2)

    @pl.when(n == 0)
    def _():
        for k in range(PROJ_ROW_SPLIT):
            rows = slice(k * sub, (k + 1) * sub)
            y = _rms(x_ref[0, rows, :]) * g_ref[...]
            tok = pl.program_id(1) * tm + k * sub + lax.broadcasted_iota(jnp.int32, (sub, 1), 0)
            is_ctx = tok >= T
            sc = jnp.where(is_ctx, scc_ref[0], scx_ref[0])
            sh = jnp.where(is_ctx, shc_ref[0], shx_ref[0])
            h = (y * (1.0 + sc) + sh).astype(BF16)
            h_ref[rows, :] = h
            og_ref[0, rows, :] = _dot(h, wg_ref[...])

    acc = _dot(h_ref[...], w_ref[...])
    o_ref[0] = jnp.where(n >= C_GATE // tn, _sigmoid(acc), acc).astype(o_ref.dtype)


def _project(xc, g, mod3, w_main, w_g, T):
    B, Tt, D = xc.shape
    N = w_main.shape[1]
    tm = Tt // PROJ_ROW_SPLIT
    tn = _pick(math.gcd(N, C_GATE), (768, 512, 256, 128))
    mrow = lambda row, chunk: pl.BlockSpec((1, 1, D), lambda b, i, n: (row(b), 0, chunk))
    own, ctx = (lambda b: b), (lambda b: B)
    return pl.pallas_call(
        functools.partial(_proj_kernel, T=T), grid=(B, PROJ_ROW_SPLIT, N // tn),
        in_specs=[pl.BlockSpec((1, tm, D), lambda b, i, n: (b, i, 0)),
                  pl.BlockSpec((1, D), lambda b, i, n: (0, 0)),
                  mrow(own, 0), mrow(own, 1), mrow(ctx, 0), mrow(ctx, 1),
                  pl.BlockSpec((D, tn), lambda b, i, n: (0, n)),
                  pl.BlockSpec((D, LANES), lambda b, i, n: (0, 0))],
        out_specs=[pl.BlockSpec((1, tm, tn), lambda b, i, n: (b, i, n)),
                   pl.BlockSpec((1, tm, LANES), lambda b, i, n: (b, i, 0))],
        out_shape=[jax.ShapeDtypeStruct((B, Tt, N), BF16), jax.ShapeDtypeStruct((B, Tt, LANES), F32)],
        scratch_shapes=[pltpu.VMEM((tm, D), BF16)],
        compiler_params=_params(("parallel", "parallel", "arbitrary"), 48),
    )(xc, g.reshape(1, D), mod3, mod3, mod3, mod3, w_main, w_g)


def _qk_prep_kernel(q_ref, k_ref, nw_ref, cos_ref, sin_ref, o_ref):
    lane = lax.broadcasted_iota(jnp.int32, (q_ref.shape[1], HEAD_DIM), 1)
    first = (lane % (HEAD_DIM // 2)) < HEAD_DIM // 4
    cos = cos_ref[...]
    sin = sin_ref[...]
    for h in range(o_ref.shape[2] // HEAD_DIM):
        src, hh = (q_ref, h) if h < GQA_HEADS else (k_ref, h - GQA_HEADS)
        y = _rms(src[0, :, hh * HEAD_DIM:(hh + 1) * HEAD_DIM].astype(F32)) * nw_ref[:, h * HEAD_DIM:(h + 1) * HEAD_DIM]
        partner = jnp.where(first, pltpu.roll(y, HEAD_DIM - HEAD_DIM // 4, 1), pltpu.roll(y, HEAD_DIM // 4, 1))
        o_ref[0, :, h * HEAD_DIM:(h + 1) * HEAD_DIM] = (y * cos + partner * sin).astype(BF16)


def _qk_prep(P, nw, cos_t, sin_t):
    B, Tt, _ = P.shape
    qw, kw = GQA_HEADS * HEAD_DIM, GQA_KV_HEADS * HEAD_DIM
    w = qw + kw
    tr = _pick(Tt, (544, TR))
    return pl.pallas_call(
        _qk_prep_kernel, grid=(B, Tt // tr),
        in_specs=[pl.BlockSpec((1, tr, qw), lambda b, i: (b, i, C_GQ // qw)),
                  pl.BlockSpec((1, tr, kw), lambda b, i: (b, i, C_GK // kw)),
                  pl.BlockSpec((1, w), lambda b, i: (0, 0)),
                  pl.BlockSpec((tr, HEAD_DIM), lambda b, i: (i, 0)),
                  pl.BlockSpec((tr, HEAD_DIM), lambda b, i: (i, 0))],
        out_specs=pl.BlockSpec((1, tr, w), lambda b, i: (b, i, 0)),
        out_shape=jax.ShapeDtypeStruct((B, Tt, w), BF16),
        compiler_params=_params(("parallel", "parallel")),
    )(P, P, nw.reshape(1, w), cos_t, sin_t)


def _flash_kernel(q_ref, k_ref, v_ref, o_ref, m_ref, acc_ref, *, R, Lc, ck, n_lat_tiles):
    tile = pl.program_id(2)

    m_ref[...] = jnp.full(m_ref.shape, NEG_INF, F32)
    acc_ref[...] = jnp.zeros(acc_ref.shape, F32)

    def step(k, v):
        vaug = jnp.concatenate([v, jnp.ones(v.shape, BF16)], axis=1)
        for r in range(R):
            rows = slice(r * TR, (r + 1) * TR)
            s = _dot_nt(q_ref[0, :, r * HEAD_DIM:(r + 1) * HEAD_DIM], k)
            slabs = [s[:, c * LANES:(c + 1) * LANES] for c in range(s.shape[1] // LANES)]
            m_old = m_ref[rows, :]
            m_new = jnp.maximum(m_old, jnp.max(functools.reduce(jnp.maximum, slabs), axis=1, keepdims=True))
            alpha = jnp.exp(m_old - m_new)
            p = jnp.concatenate([jnp.exp(sl - m_new) for sl in slabs], axis=1).astype(BF16)
            acc_ref[rows, :] = jnp.concatenate([alpha, alpha], axis=1) * acc_ref[rows, :] + _dot(p, vaug)
            m_ref[rows, :] = m_new

    kv_rows = k_ref.shape[1]

    @pl.when(tile >= n_lat_tiles)
    def _():
        step(k_ref[0, kv_rows - Lc:, :], v_ref[0, kv_rows - Lc:, :])

    @pl.when(tile < n_lat_tiles)
    def _():
        def body(j, carry):
            r0 = pl.multiple_of(j * ck, ck)
            step(k_ref[0, pl.ds(r0, ck), :], v_ref[0, pl.ds(r0, ck), :])
            return carry
        lax.fori_loop(0, kv_rows // ck, body, 0)

    o = acc_ref[:, :HEAD_DIM] / acc_ref[:, HEAD_DIM:]
    for r in range(R):
        o_ref[0, :, r * HEAD_DIM:(r + 1) * HEAD_DIM] = o[r * TR:(r + 1) * TR].astype(o_ref.dtype)


def _flash(q_arr, q_col, k_arr, k_col, v_arr, v_col, *, G, R, T, Lc):
    B, Tt, _ = q_arr.shape
    qw = R * HEAD_DIM
    ck = max(c for c in range(LANES, 2304 + 1, LANES) if Tt % c == 0)
    return pl.pallas_call(
        functools.partial(_flash_kernel, R=R, Lc=Lc, ck=ck, n_lat_tiles=T // TR),
        grid=(B, G, Tt // TR),
        in_specs=[pl.BlockSpec((1, TR, qw), lambda b, g, i: (b, i, q_col // qw + g)),
                  pl.BlockSpec((1, Tt, HEAD_DIM), lambda b, g, i: (b, 0, k_col // HEAD_DIM + g)),
                  pl.BlockSpec((1, Tt, HEAD_DIM), lambda b, g, i: (b, 0, v_col // HEAD_DIM + g))],
        out_specs=pl.BlockSpec((1, TR, qw), lambda b, g, i: (b, i, g)),
        out_shape=jax.ShapeDtypeStruct((B, Tt, G * qw), BF16),
        scratch_shapes=[pltpu.VMEM((R * TR, LANES), F32), pltpu.VMEM((R * TR, 2 * HEAD_DIM), F32)],
        compiler_params=_params(("parallel", "parallel", "arbitrary"), 48),
    )(q_arr, k_arr, v_arr)


def _nat_geometry(T):
    rows = T // GRID_W
    kh = min(NAT_WIN_ROWS, rows)
    nkr = min(kh + 1, rows)
    qbr = Q_BLOCK // GRID_W
    nbr = rows // qbr
    kb = np.zeros(nbr, np.int32)
    var = np.zeros(nbr, np.int32)
    variants = []
    for j in range(nbr):
        qrow = j * qbr + np.arange(qbr)
        start_r = np.clip(qrow - kh // 2, 0, rows - kh)
        kb[j] = min(start_r[0], rows - nkr)
        sig = (int(kb[j] - j * qbr), tuple(int(s - kb[j]) for s in start_r))
        if sig not in variants:
            variants.append(sig)
        var[j] = variants.index(sig)
    return kh, nkr, qbr, nbr, kb, var, variants


def _nat_bias_kernel(rb_ref, o_ref, *, variants, kh, nkr, qbr):
    h = pl.program_id(0)
    ndr = 2 * NAT_WIN_ROWS - 1
    ndc = 2 * NAT_WIN_COLS - 1
    qc = lax.broadcasted_iota(jnp.int32, (GRID_W, GRID_W), 0)
    kc = lax.broadcasted_iota(jnp.int32, (GRID_W, GRID_W), 1)
    dc = kc - qc + NAT_WIN_COLS - 1
    start_c = jnp.clip(qc - NAT_WIN_COLS // 2, 0, GRID_W - NAT_WIN_COLS)
    col_in = (kc >= start_c) & (kc < start_c + NAT_WIN_COLS)
    masked = jnp.full((GRID_W, GRID_W), NEG_INF, F32)
    toeplitz = []
    for dr in range(ndr):
        t = jnp.zeros((GRID_W, GRID_W), F32)
        for d in range(ndc):
            t = jnp.where(dc == d, rb_ref[(h * ndr + dr) * ndc + d], t)
        toeplitz.append(jnp.where(col_in, t, NEG_INF))
    for v, (delta, srel) in enumerate(variants):
        for qr in range(qbr):
            blocks = []
            for kr in range(nkr):
                if srel[qr] <= kr < srel[qr] + kh:
                    blocks.append(toeplitz[int(np.clip(kr + delta - qr + NAT_WIN_ROWS - 1, 0, ndr - 1))])
                else:
                    blocks.append(masked)
            rows = slice(qr * GRID_W, (qr + 1) * GRID_W)
            for kr in range(0, nkr - 1, 2):
                o_ref[v, 0, rows, kr * GRID_W:(kr + 2) * GRID_W] = jnp.concatenate(blocks[kr:kr + 2], axis=1)
            if nkr % 2:
                o_ref[v, 0, rows, (nkr - 1) * GRID_W:] = blocks[-1]


def _nat_bias_tables(rel_bias, T):
    kh, nkr, qbr, _, _, _, variants = _nat_geometry(T)
    V = len(variants)
    nk = nkr * GRID_W
    return pl.pallas_call(
        functools.partial(_nat_bias_kernel, variants=variants, kh=kh, nkr=nkr, qbr=qbr),
        grid=(NAT_HEADS,),
        in_specs=[pl.BlockSpec(memory_space=pltpu.SMEM)],
        out_specs=pl.BlockSpec((V, 1, Q_BLOCK, nk), lambda h: (0, h, 0, 0)),
        out_shape=jax.ShapeDtypeStruct((V, NAT_HEADS, Q_BLOCK, nk), F32),
        compiler_params=_params(("parallel",)),
    )(rel_bias.reshape(-1))


def _nat_kernel(kb_ref, var_ref, q_ref, k_ref, v_ref, bias_ref, o_ref, s1_ref, s2_ref, *, T, nbr, nk, scale):
    kc = k_ref[0, T:, :]
    vc = v_ref[0, T:, :]

    def scores(j, slot):
        q = q_ref[0, pl.ds(pl.multiple_of(j * Q_BLOCK, Q_BLOCK), Q_BLOCK), :]
        r0 = pl.multiple_of(kb_ref[j] * GRID_W, GRID_W)
        s1_ref[slot] = _dot_nt(q, k_ref[0, pl.ds(r0, nk), :]) * scale + bias_ref[var_ref[j], 0]
        s2_ref[slot] = _dot_nt(q, kc) * scale

    def finish(j, slot):
        s1 = s1_ref[slot]
        s2 = s2_ref[slot]
        r0 = pl.multiple_of(kb_ref[j] * GRID_W, GRID_W)
        m = jnp.maximum(jnp.max(s1, axis=1, keepdims=True), jnp.max(s2, axis=1, keepdims=True))
        p1 = jnp.exp(s1 - m)
        p2 = jnp.exp(s2 - m)
        l = jnp.sum(p1, axis=1, keepdims=True) + jnp.sum(p2, axis=1, keepdims=True)
        o = _dot(p1.astype(BF16), v_ref[0, pl.ds(r0, nk), :]) + _dot(p2.astype(BF16), vc)
        o_ref[0, pl.ds(pl.multiple_of(j * Q_BLOCK, Q_BLOCK), Q_BLOCK), :] = (o / l).astype(o_ref.dtype)

    scores(0, 0)

    def body(i, carry):
        j = 2 * i
        scores(j + 1, 1)
        finish(j, 0)
        scores(jnp.minimum(j + 2, nbr - 1), 0)
        finish(j + 1, 1)
        return carry

    lax.fori_loop(0, nbr // 2, body, 0, unroll=4)

    s = _dot_nt(q_ref[0, T:, :], kc) * scale
    p = jnp.exp(s - jnp.max(s, axis=1, keepdims=True))
    o_ref[0, T:, :] = (_dot(p.astype(BF16), vc) / jnp.sum(p, axis=1, keepdims=True)).astype(o_ref.dtype)


def _nat(P, bias_tab, T, Lc, scale):
    B, Tt, _ = P.shape
    _, nkr, _, nbr, kb, var, _ = _nat_geometry(T)
    nk = nkr * GRID_W
    V = bias_tab.shape[0]
    grid_spec = pltpu.PrefetchScalarGridSpec(
        num_scalar_prefetch=2, grid=(B, NAT_HEADS),
        in_specs=[pl.BlockSpec((1, Tt, HEAD_DIM), lambda b, h, kb_, vr: (b, 0, C_NQ // HEAD_DIM + h)),
                  pl.BlockSpec((1, Tt, HEAD_DIM), lambda b, h, kb_, vr: (b, 0, C_NK // HEAD_DIM + h)),
                  pl.BlockSpec((1, Tt, HEAD_DIM), lambda b, h, kb_, vr: (b, 0, C_NV // HEAD_DIM + h)),
                  pl.BlockSpec((V, 1, Q_BLOCK, nk), lambda b, h, kb_, vr: (0, h, 0, 0))],
        out_specs=pl.BlockSpec((1, Tt, HEAD_DIM), lambda b, h, kb_, vr: (b, 0, h)),
        scratch_shapes=[pltpu.VMEM((2, Q_BLOCK, nk), F32), pltpu.VMEM((2, Q_BLOCK, Lc), F32)])
    assert nbr % 2 == 0
    return pl.pallas_call(
        functools.partial(_nat_kernel, T=T, nbr=nbr, nk=nk, scale=scale),
        grid_spec=grid_spec,
        out_shape=jax.ShapeDtypeStruct((B, Tt, NAT_HEADS * HEAD_DIM), BF16),
        compiler_params=_params(("parallel", "parallel")),
    )(jnp.asarray(kb), jnp.asarray(var), P, P, P, bias_tab)


def _mlstm_kernel(qf_ref, kf_ref, vf_ref, gf_ref, ktf_ref, qb_ref, kb_ref, vb_ref, gb_ref, ktb_ref, bias_ref,
                  hf_ref, hb_ref, st_ref, m_ref, *, kscale):
    L = MLSTM_CHUNK
    dv = MLSTM_DV

    @pl.when(pl.program_id(1) == 0)
    def _():
        st_ref[...] = jnp.zeros(st_ref.shape, F32)
        m_ref[...] = jnp.full(m_ref.shape, NEG_INF, F32)

    ti = lax.broadcasted_iota(jnp.int32, (L, L), 0)
    si = lax.broadcasted_iota(jnp.int32, (L, L), 1)
    ones_col = (lax.broadcasted_iota(jnp.int32, (L, LANES), 1) == 0).astype(BF16)

    for dirn, (q_ref, k_ref, v_ref, g_ref, kt_ref, h_ref) in enumerate(
            ((qf_ref, kf_ref, vf_ref, gf_ref, ktf_ref, hf_ref), (qb_ref, kb_ref, vb_ref, gb_ref, ktb_ref, hb_ref))):
        tri = (si <= ti) if dirn == 0 else (si >= ti)
        G = g_ref[0] + bias_ref[...]
        LF = jnp.minimum(G, 0.0) - jnp.log(1.0 + jnp.exp(-jnp.abs(G)))
        lf_hi = LF.astype(BF16)
        lf_lo = (LF - lf_hi.astype(F32)).astype(BF16)
        trib = tri.astype(BF16)
        Bc = _dot(trib, lf_hi) + _dot(trib, lf_lo)
        BcT = Bc.T
        GT = G.T
        bl_row = Bc[L - 1:L, :] if dirn == 0 else Bc[0:1, :]
        for h in range(MLSTM_HEADS):
            ci = dirn * 2 * MLSTM_HEADS + h
            cf = ci + MLSTM_HEADS
            bcol = Bc[:, cf:cf + 1]
            brow = BcT[cf:cf + 1, :]
            licol = G[:, ci:ci + 1]
            lirow = GT[ci:ci + 1, :]
            bl = bl_row[:, cf:cf + 1]
            m_old = m_ref[dirn, h]
            gcol = bl - bcol + licol
            m_new = jnp.maximum(bl + m_old, jnp.max(gcol, axis=0, keepdims=True))
            decay = jnp.exp(bl + m_old - m_new)
            wk = jnp.exp(gcol - m_new)
            dmat = jnp.where(tri, bcol - brow + lirow, NEG_INF)
            inter = bcol + m_old
            m_t = jnp.maximum(inter, jnp.max(dmat, axis=1, keepdims=True))
            w = jnp.exp(dmat - m_t)
            a = jnp.exp(inter - m_t)
            qh = q_ref[0, :, h * MLSTM_DQK:(h + 1) * MLSTM_DQK]
            kh = k_ref[0, :, h * MLSTM_DQK:(h + 1) * MLSTM_DQK]
            vaug = jnp.concatenate([v_ref[0, :, h * dv:(h + 1) * dv], ones_col], axis=1)
            smat = _dot_nt(qh, kh) * kscale * w
            state = st_ref[dirn, h]
            nd = _dot(smat.astype(BF16), vaug) + a * _dot(qh, state.astype(BF16))
            den = jnp.maximum(jnp.abs(nd[:, dv:dv + 1]), jnp.exp(-m_t))
            h_ref[0, :, h * dv:(h + 1) * dv] = (nd[:, :dv] / den).astype(h_ref.dtype)
            kT = kt_ref[0, h * MLSTM_DQK:(h + 1) * MLSTM_DQK, :]
            upd = _dot(kT, (wk * vaug.astype(F32)).astype(BF16)) * kscale
            st_ref[dirn, h] = decay * state + upd
            m_ref[dirn, h] = m_new


def _mlstm(P, gates, gate_bias, T, Lc):
    B, Tt, _ = P.shape
    L = MLSTM_CHUNK
    kt = jnp.swapaxes(P[:, :, C_MK:C_MK + MLSTM_HEADS * MLSTM_DQK], 1, 2)
    ncl, ncc = T // L, Lc // L
    fwd = lambda i: jnp.where(i < ncc, ncl + i, i - ncc)
    bwd = lambda i: jnp.where(i < ncc, ncl + ncc - 1 - i, ncl - 1 - (i - ncc))
    qw = MLSTM_HEADS * MLSTM_DQK
    vw = MLSTM_HEADS * MLSTM_DV

    def specs(order):
        return [pl.BlockSpec((1, L, qw), lambda b, i: (b, order(i), C_MQ // qw)),
                pl.BlockSpec((1, L, qw), lambda b, i: (b, order(i), C_MK // qw)),
                pl.BlockSpec((1, L, vw), lambda b, i: (b, order(i), C_MV // vw)),
                pl.BlockSpec((1, L, LANES), lambda b, i: (b, order(i), 0)),
                pl.BlockSpec((1, qw, L), lambda b, i: (b, 0, order(i)))]

    out = jax.ShapeDtypeStruct((B, Tt, vw), BF16)
    return pl.pallas_call(
        functools.partial(_mlstm_kernel, kscale=MLSTM_DQK ** -0.5),
        grid=(B, ncl + ncc),
        in_specs=specs(fwd) + specs(bwd) + [pl.BlockSpec((1, LANES), lambda b, i: (0, 0))],
        out_specs=[pl.BlockSpec((1, L, vw), lambda b, i: (b, fwd(i), 0)),
                   pl.BlockSpec((1, L, vw), lambda b, i: (b, bwd(i), 0))],
        out_shape=[out, out],
        scratch_shapes=[pltpu.VMEM((2, MLSTM_HEADS, MLSTM_DQK, MLSTM_DV + LANES), F32),
                        pltpu.VMEM((2, MLSTM_HEADS, 1, 1), F32)],
        compiler_params=_params(("parallel", "arbitrary")),
    )(P, P, P, gates, kt, P, P, P, gates, kt, gate_bias)


def _merge_kernel(hf_ref, hb_ref, mo_ref, yg_ref, yn_ref, g0_ref, g1_ref, g2_ref, wb_ref, o_ref, ym_ref):
    @pl.when(pl.program_id(1) == 0)
    def _():
        h = hf_ref[...].astype(F32) + hb_ref[...].astype(F32)
        for hh in range(MLSTM_HEADS):
            cols = slice(hh * MLSTM_DV, (hh + 1) * MLSTM_DV)
            ym_ref[:, cols] = (_sigmoid(mo_ref[:, cols].astype(F32)) * _rms(h[:, cols])).astype(BF16)

    gate = lambda r: r[...].astype(F32)
    acc = gate(g0_ref) * _dot(ym_ref[...], wb_ref[0])
    acc += gate(g1_ref) * _dot(yg_ref[...], wb_ref[1])
    acc += gate(g2_ref) * _dot(yn_ref[...], wb_ref[2])
    o_ref[...] = acc.astype(BF16)


def _merge(hf, hb, P2, yg, yn, wb):
    M, W = hf.shape
    D = wb.shape[2]
    tm = _pick(M, (1024, 512, 256))
    tn = _pick(D, (512, 256))
    row = lambda c: pl.BlockSpec((tm, W), lambda i, n: (i, c))
    gate = lambda br: pl.BlockSpec((tm, tn), lambda i, n: (i, (C_GATE + br * D) // tn + n))
    return pl.pallas_call(
        _merge_kernel, grid=(M // tm, D // tn),
        in_specs=[row(0), row(0), row(C_MO // W), row(0), row(0), gate(0), gate(1), gate(2),
                  pl.BlockSpec((N_BRANCHES, W, tn), lambda i, n: (0, 0, n))],
        out_specs=pl.BlockSpec((tm, tn), lambda i, n: (i, n)),
        out_shape=jax.ShapeDtypeStruct((M, D), BF16),
        scratch_shapes=[pltpu.VMEM((tm, W), BF16)],
        compiler_params=_params(("parallel", "arbitrary"), 56),
    )(hf, hb, P2, yg, yn, P2, P2, P2, wb)


def _wout_post_kernel(a_ref, w_ref, x_ref, gp_ref, gate_ref, o_ref):
    y = _dot(a_ref[0], w_ref[...])
    o_ref[0] = x_ref[0] + gate_ref[0] * (_rms(y) * gp_ref[...])


def _wout_post(merged, w_out, xc, g_post, mod3, gate_chunk):
    B, Tt, D = xc.shape
    nT = Tt // TR
    tile = pl.BlockSpec((1, TR, D), lambda b, i: (b, i, 0))
    row = lambda b, i: jnp.where(i == nT - 1, B, b)
    return pl.pallas_call(
        _wout_post_kernel, grid=(B, nT),
        in_specs=[tile, pl.BlockSpec((D, D), lambda b, i: (0, 0)), tile,
                  pl.BlockSpec((1, D), lambda b, i: (0, 0)),
                  pl.BlockSpec((1, 1, D), lambda b, i: (row(b, i), 0, gate_chunk))],
        out_specs=tile,
        out_shape=jax.ShapeDtypeStruct((B, Tt, D), F32),
        compiler_params=_params(("parallel", "parallel"), 48),
    )(merged, w_out, xc, g_post.reshape(1, D), mod3)


def _route_segment(arow, acol, k, chunk):
    E, T = arow.shape
    nchunk = T // chunk
    ind = lambda m: jnp.where(m, 1.0, 0.0)
    bits = pltpu.bitcast(arow, jnp.int32)
    thr = jnp.zeros((E, 1), jnp.int32)
    for bit in range(30, -1, -1):
        cand = thr | (1 << bit)
        cnt = jnp.sum(ind(bits >= cand), axis=1, keepdims=True)
        thr = jnp.where(cnt >= k, cand, thr)
    thr_val = pltpu.bitcast(thr, F32)
    gt = arow > thr_val
    eq = arow == thr_val
    need = k - jnp.sum(ind(gt), axis=1, keepdims=True)

    r_i = lax.broadcasted_iota(jnp.int32, (chunk, chunk), 0)
    c_i = lax.broadcasted_iota(jnp.int32, (chunk, chunk), 1)
    before = ind(r_i < c_i).astype(BF16)
    after = ind(c_i < r_i).astype(BF16)

    def prefix_rows(mask):
        carry = jnp.zeros((E, 1), F32)
        parts, carries = [], [carry]
        for c in range(nchunk):
            mc = ind(mask[:, c * chunk:(c + 1) * chunk]).astype(BF16)
            parts.append(_dot(mc, before) + carry)
            carry = carry + jnp.sum(mc.astype(F32), axis=1, keepdims=True)
            carries.append(carry)
        return jnp.concatenate(parts, axis=1), carries

    tie_rank, _ = prefix_rows(eq)
    sel = gt | (eq & (tie_rank < need))
    pos, carries = prefix_rows(sel)
    prow = jnp.where(sel, pos, -1.0)
    lane = lax.broadcasted_iota(jnp.int32, (E, LANES), 1)
    starts = jnp.zeros((E, LANES), F32)
    for c, cv in enumerate(carries):
        starts = jnp.where(lane == c, cv, starts)

    sub = lax.broadcasted_iota(jnp.int32, (E, LANES), 0)
    diag = sub == lane
    thr_row = jnp.sum(jnp.where(diag, thr_val, 0.0), axis=0, keepdims=True)
    need_row = jnp.sum(jnp.where(diag, need, 0.0), axis=0, keepdims=True)
    lane_ok = lax.broadcasted_iota(jnp.int32, (1, LANES), 1) < E
    gt_c = (acol > thr_row) & lane_ok
    eq_c = (acol == thr_row) & lane_ok

    def prefix_cols(mask):
        carry = jnp.zeros((1, LANES), F32)
        parts = []
        for c in range(nchunk):
            mc = ind(mask[c * chunk:(c + 1) * chunk]).astype(BF16)
            parts.append(_dot(after, mc) + carry)
            carry = carry + jnp.sum(mc.astype(F32), axis=0, keepdims=True)
        return jnp.concatenate(parts, axis=0)

    sel_c = gt_c | (eq_c & (prefix_cols(eq_c) < need_row))
    pcol = jnp.where(sel_c, prefix_cols(sel_c), -1.0)
    return prow, pcol, starts


def _route_kernel(arow_ref, acol_ref, prow_ref, pcol_ref, starts_ref, *, T, Lc, cap_l, cap_c):
    prow, pcol, starts = _route_segment(arow_ref[0, :, :T], acol_ref[0, :T], cap_l, TR)
    prow_ref[0, :, :T] = prow
    pcol_ref[0, :T] = pcol
    starts_ref[0] = starts.astype(jnp.int32)
    prow, pcol, _ = _route_segment(arow_ref[0, :, T:], acol_ref[0, T:], cap_c, TR)
    prow_ref[0, :, T:] = prow
    pcol_ref[0, T:] = pcol


def _route(arow, acol, T, Lc, cap_l, cap_c):
    B, E, Tt = arow.shape
    return pl.pallas_call(
        functools.partial(_route_kernel, T=T, Lc=Lc, cap_l=cap_l, cap_c=cap_c),
        grid=(B,),
        in_specs=[pl.BlockSpec((1, E, Tt), lambda b: (b, 0, 0)), pl.BlockSpec((1, Tt, LANES), lambda b: (b, 0, 0))],
        out_specs=[pl.BlockSpec((1, E, Tt), lambda b: (b, 0, 0)), pl.BlockSpec((1, Tt, LANES), lambda b: (b, 0, 0)),
                   pl.BlockSpec((1, E, LANES), lambda b: (b, 0, 0))],
        out_shape=[jax.ShapeDtypeStruct((B, E, Tt), F32), jax.ShapeDtypeStruct((B, Tt, LANES), F32),
                   jax.ShapeDtypeStruct((B, E, LANES), jnp.int32)],
        compiler_params=_params(("parallel",), 48),
    )(arow, acol)


def _gather_kernel(st_ref, p_ref, a_ref, h_ref, o_ref, g_ref, acc_ref, gacc_ref, *, chunk0, nchunk, cap, st):
    b = pl.program_id(0)
    e = pl.program_id(1)
    base = (b * N_EXPERTS + e) * (nchunk + 1)
    for S in range(cap // st):
        acc_ref[...] = jnp.zeros(acc_ref.shape, F32)
        gacc_ref[...] = jnp.zeros(gacc_ref.shape, F32)
        slot = (lax.broadcasted_iota(jnp.int32, (st, TR), 0) + S * st).astype(F32)

        def body(c, carry):
            lo = st_ref[base + c]
            hi = st_ref[base + c + 1]

            @pl.when((lo < (S + 1) * st) & (hi > S * st))
            def _():
                hit = p_ref[0, pl.ds(chunk0 + c, 1), :] == slot
                r0 = pl.multiple_of(c * TR, TR)
                acc_ref[...] += _dot(jnp.where(hit, 1.0, 0.0).astype(BF16), h_ref[0, pl.ds(r0, TR), :])
                gacc_ref[...] += jnp.sum(jnp.where(hit, a_ref[0, pl.ds(chunk0 + c, 1), :], 0.0),
                                         axis=1, keepdims=True)
            return carry

        lax.fori_loop(0, nchunk, body, 0)
        o_ref[0, S * st:(S + 1) * st, :] = acc_ref[...].astype(BF16)
        g_ref[0, S * st:(S + 1) * st, :] = gacc_ref[...]


def _gather(starts, prow3, arow3, hbf, prev, *, S_tot, chunk0, nchunk, cap, slot_blk0):
    B, Tt, D = hbf.shape
    E = N_EXPERTS
    nT = Tt // TR
    st = min(cap, 128)
    rows = nchunk * TR
    pspec = pl.BlockSpec((1, nT, TR), lambda b, e, s: (b * E + e, 0, 0))
    in_specs = [pspec, pspec, pl.BlockSpec((1, rows, D), lambda b, e, s: (b, chunk0 * TR // rows, 0))]
    args = [prow3, arow3, hbf]
    kern = functools.partial(_gather_kernel, chunk0=chunk0, nchunk=nchunk, cap=cap, st=st)
    aliases = {}
    if prev is not None:
        in_specs += [pl.BlockSpec(memory_space=pl.ANY)] * 2
        args += list(prev)
        aliases = {4: 0, 5: 1}
        inner = kern
        kern = lambda s, p, a, h, _x, _g, o, g, acc, gacc: inner(s, p, a, h, o, g, acc, gacc)
    grid_spec = pltpu.PrefetchScalarGridSpec(
        num_scalar_prefetch=1, grid=(B, E), in_specs=in_specs,
        out_specs=[pl.BlockSpec((1, cap, D), lambda b, e, s: (e, slot_blk0 + b, 0)),
                   pl.BlockSpec((1, cap, LANES), lambda b, e, s: (e, slot_blk0 + b, 0))],
        scratch_shapes=[pltpu.VMEM((st, D), F32), pltpu.VMEM((st, LANES), F32)])
    return pl.pallas_call(
        kern, grid_spec=grid_spec,
        out_shape=[jax.ShapeDtypeStruct((E, S_tot, D), BF16), jax.ShapeDtypeStruct((E, S_tot, LANES), F32)],
        input_output_aliases=aliases,
        compiler_params=_params(("parallel", "arbitrary"), 56),
    )(starts, *args)


def _ffn1_kernel(x_ref, wg_ref, wu_ref, o_ref, wgb_ref, wub_ref):
    @pl.when(pl.program_id(2) == 0)
    def _():
        wgb_ref[...] = wg_ref[0].astype(BF16)
        wub_ref[...] = wu_ref[0].astype(BF16)
    x = x_ref[0]
    g = _dot(x, wgb_ref[...])
    u = _dot(x, wub_ref[...])
    o_ref[0] = (g * _sigmoid(g) * u).astype(BF16)


def _ffn2_kernel(x_ref, w_ref, g_ref, o_ref, wb_ref):
    @pl.when(pl.program_id(2) == 0)
    def _():
        wb_ref[...] = w_ref[0].astype(BF16)
    o_ref[0] = (_dot(x_ref[0], wb_ref[...]) * g_ref[0, :, :1]).astype(BF16)


def _expert_matmul(kernel, x, ws, layer, row_scale=None):
    E, S, K = x.shape
    N = ws[0].shape[3]
    ws = [w.reshape(-1, K, N) for w in ws]
    tm = _pick(S, (1088, 1024, 544, 512, 272, 256, 160, 128))
    tn = _pick(N, (1024, 512, 256) if len(ws) == 1 else (512, 256))
    wspec = pl.BlockSpec((1, K, tn), lambda e, n, m: (layer * E + e, 0, n))
    in_specs = [pl.BlockSpec((1, tm, K), lambda e, n, m: (e, m, 0))] + [wspec] * len(ws)
    args = [x, *ws]
    if row_scale is not None:
        in_specs.append(pl.BlockSpec((1, tm, LANES), lambda e, n, m: (e, m, 0)))
        args.append(row_scale)
    return pl.pallas_call(
        kernel, grid=(E, N // tn, S // tm),
        in_specs=in_specs,
        out_specs=pl.BlockSpec((1, tm, tn), lambda e, n, m: (e, m, n)),
        out_shape=jax.ShapeDtypeStruct((E, S, N), BF16),
        scratch_shapes=[pltpu.VMEM((K, tn), BF16)] * len(ws),
        compiler_params=_params(("parallel", "parallel", "arbitrary"), 48),
    )(*args)


SLOT_ALIGN = 16
COMBINE_WINDOW = 64


def _combine_kernel(st_ref, pc_ref, ye_ref, x_ref, gp_ref, gate_ref, o_ref, win_ref, acc_ref, sem,
                    *, nchunk, cap, W, slot0):
    E = N_EXPERTS
    b = pl.program_id(0)
    t = pl.program_id(1)
    nb, nt = pl.num_programs(0), pl.num_programs(1)
    step = b * nt + t
    buf = step % 2

    def bounds(bb, tt, e):
        base = (bb * E + e) * (nchunk + 1) + tt
        return (st_ref[base] // SLOT_ALIGN) * SLOT_ALIGN, st_ref[base + 1]

    def window(bb, tt, e, r):
        first, _ = bounds(bb, tt, e)
        want = first + r * W
        return want, jnp.minimum(want, cap - W)

    def copy(bb, tt, e, r, into):
        _, start = window(bb, tt, e, r)
        row0 = pl.multiple_of(slot0 + bb * cap + start, SLOT_ALIGN)
        return pltpu.make_async_copy(ye_ref.at[e, pl.ds(row0, W), :],
                                     win_ref.at[into, pl.ds(e * W, W), :], sem.at[into, e])

    @pl.when(step == 0)
    def _():
        for e in range(E):
            copy(b, t, e, 0, buf).start()

    @pl.when(step + 1 < nb * nt)
    def _():
        last_t = t + 1 == nt
        b_next = jnp.where(last_t, b + 1, b)
        t_next = jnp.where(last_t, 0, t + 1)
        for e in range(E):
            copy(b_next, t_next, e, 0, 1 - buf).start()

    rounds = jnp.int32(1)
    for e in range(E):
        first, hi = bounds(b, t, e)
        rounds = jnp.maximum(rounds, (hi - first + W - 1) // W)

    pos1 = pc_ref[0] + 1.0
    pos_hi = jnp.floor(pos1 * (1.0 / SLOT_ALIGN))
    pos_lo = pos1 - pos_hi * SLOT_ALIGN
    col = lax.broadcasted_iota(jnp.int32, (LANES, E * W), 1)
    row = lax.broadcasted_iota(jnp.int32, (LANES, E * W), 0)
    expand = jnp.where(col // W == row, 1.0, 0.0).astype(BF16)
    pos_rep = SLOT_ALIGN * _dot(pos_hi.astype(BF16), expand) + _dot(pos_lo.astype(BF16), expand) - 1.0
    lane = lax.broadcasted_iota(jnp.int32, (1, E * W), 1)
    in_win = (lane % W).astype(F32)

    def onehot(r):
        want_row = jnp.zeros((1, E * W), F32)
        start_row = jnp.zeros((1, E * W), F32)
        for e in range(E):
            want, start = window(b, t, e, r)
            mine = lane // W == e
            want_row = jnp.where(mine, want.astype(F32), want_row)
            start_row = jnp.where(mine, start.astype(F32), start_row)
        hit = (pos_rep - start_row == in_win) & (pos_rep >= want_row)
        return jnp.where(hit, 1.0, 0.0).astype(BF16)

    lhs = onehot(0)
    for e in range(E):
        copy(b, t, e, 0, buf).wait()
    acc_ref[...] = _dot(lhs, win_ref[buf])

    def more(r, carry):
        for e in range(E):
            copy(b, t, e, r, buf).start()
        lhs_r = onehot(r)
        for e in range(E):
            copy(b, t, e, r, buf).wait()
        acc_ref[...] += _dot(lhs_r, win_ref[buf])
        return carry

    lax.fori_loop(1, rounds, more, 0)
    o_ref[0] = x_ref[0] + gate_ref[0] * (_rms(acc_ref[...]) * gp_ref[...])


def _combine(starts, pcol, ye, xc, g_post, mod3, gate_chunk, *, tile0, ntile, nchunk, cap, slot0,
             mod_row_ctx, out):
    B, Tt, D = xc.shape
    E = N_EXPERTS
    W = min(cap, COMBINE_WINDOW)
    fresh = out is not None
    if not fresh:
        out = xc
    tok = lambda w: pl.BlockSpec((1, TR, w), lambda b, t, s: (b, tile0 + t, 0))
    mrow = (lambda b: B) if mod_row_ctx else (lambda b: b)
    in_specs = [tok(LANES), pl.BlockSpec(memory_space=pl.ANY), tok(D),
                pl.BlockSpec((1, D), lambda b, t, s: (0, 0)),
                pl.BlockSpec((1, 1, D), lambda b, t, s: (mrow(b), 0, gate_chunk))]
    args = [pcol, ye, xc, g_post.reshape(1, D), mod3]
    kern = functools.partial(_combine_kernel, nchunk=nchunk, cap=cap, W=W, slot0=slot0)
    if fresh:
        out_spec = pl.BlockSpec((1, TR, D), lambda b, t, s: (b, t, 0))
        aliases = {}
    else:
        out_spec = tok(D)
        aliases = {3: 0}
    grid_spec = pltpu.PrefetchScalarGridSpec(
        num_scalar_prefetch=1, grid=(B, ntile), in_specs=in_specs, out_specs=out_spec,
        scratch_shapes=[pltpu.VMEM((2, E * W, D), BF16), pltpu.VMEM((TR, D), F32),
                        pltpu.SemaphoreType.DMA((2, E))])
    return pl.pallas_call(
        kern, grid_spec=grid_spec,
        out_shape=jax.ShapeDtypeStruct(out.shape, F32),
        input_output_aliases=aliases,
        compiler_params=_params(("arbitrary", "arbitrary"), 48),
    )(starts, *args)


def _rope_tables(T, Lc):
    nf = HEAD_DIM // 4
    t = np.arange(T)
    inv = ROPE_THETA ** (-jnp.arange(nf, dtype=F32) / nf)
    row = jnp.asarray(t // GRID_W, F32)
    col = jnp.asarray(t % GRID_W, F32)
    ang = jnp.stack([row[:, None] * inv, col[:, None] * inv], axis=1)
    cos = jnp.broadcast_to(jnp.cos(ang)[:, :, None, :], (T, 2, 2, nf)).reshape(T, HEAD_DIM)
    sin = jnp.broadcast_to(jnp.sin(ang)[:, :, None, :], (T, 2, 2, nf)).reshape(T, HEAD_DIM)
    sign = jnp.asarray(np.tile(np.repeat([-1.0, 1.0], nf), 2), F32)
    cos = jnp.concatenate([cos, jnp.ones((Lc, HEAD_DIM), F32)], axis=0)
    sin = jnp.concatenate([sin * sign, jnp.zeros((Lc, HEAD_DIM), F32)], axis=0)
    return cos, sin


def _split_bf16(w):
    hi = w.astype(BF16)
    return hi, (w - hi.astype(F32)).astype(BF16)


def kernel(x, c, ctx, c_ctx, w_mod, b_mod, g_pre_mix, g_post_mix, g_pre_ffn, g_post_ffn, w_in, mlstm_gate_bias, gqa_q_norm, gqa_k_norm, nat_rel_bias, w_branch, w_out, w_router, w_expert_gate, w_expert_up, w_expert_down):
    B, T, D = x.shape
    Lc = ctx.shape[1]
    depth = w_mod.shape[0]
    E = N_EXPERTS
    assert Lc == TR and T % (2 * TR) == 0 and w_router.shape[2] == E
    Tt = T + Lc
    nT = Tt // TR
    M = B * Tt
    cap_l = EC_CAPACITY_FACTOR * T // E
    cap_c = EC_CAPACITY_FACTOR * Lc // E
    scale = HEAD_DIM ** -0.5
    nchunk = T // TR

    xc = jnp.concatenate([x, ctx], axis=1)
    rows_c = -(-(B + 1) // 8) * 8
    cc = jnp.zeros((rows_c, D), F32).at[:B].set(c).at[B].set(c_ctx)
    mod = _modulation(cc, w_mod, b_mod)
    cos_t, sin_t = _rope_tables(T, Lc)

    for l in range(depth):
        last = l == depth - 1
        mod3 = mod[l].reshape(rows_c, 1, N_MOD * D)
        w_l = w_in[l]
        w_main = jnp.concatenate([w_l[:, :W_IN_GATE_COL], w_l[:, W_IN_GATE_COL + N_GATE_COLS:]], axis=1).astype(BF16)
        w_g = jnp.pad(w_l[:, W_IN_GATE_COL:W_IN_GATE_COL + N_GATE_COLS], ((0, 0), (0, LANES - N_GATE_COLS))).astype(BF16)
        gate_bias = jnp.pad(mlstm_gate_bias[l], (0, LANES - N_GATE_COLS)).reshape(1, LANES)
        nw = jnp.concatenate([jnp.tile(gqa_q_norm[l][None] * scale, (GQA_HEADS, 1)),
                              jnp.tile(gqa_k_norm[l][None], (GQA_KV_HEADS, 1))], axis=0)[:, None, :]

        P, gates = _project(xc, g_pre_mix[l], mod3, w_main, w_g, T)
        P2 = P.reshape(M, -1)

        hf, hb = _mlstm(P, gates, gate_bias, T, Lc)

        qk = _qk_prep(P, nw, cos_t, sin_t)
        yg = _flash(qk, 0, qk, GQA_HEADS * HEAD_DIM, P, C_GV, G=GQA_KV_HEADS, R=GQA_HEADS // GQA_KV_HEADS,
                    T=T, Lc=Lc)

        bias_tab = _nat_bias_tables(nat_rel_bias[l], T)
        yn = _nat(P, bias_tab, T, Lc, scale)

        merged = _merge(hf.reshape(M, -1), hb.reshape(M, -1), P2, yg.reshape(M, -1), yn.reshape(M, -1),
                        w_branch[l].astype(BF16))
        xc = _wout_post(merged.reshape(B, Tt, D), w_out[l].astype(BF16), xc, g_post_mix[l], mod3, 2)

        wr = jnp.pad(w_router[l], ((0, 0), (0, LANES - E)))
        hx, acol, arow = _norm_mod_router(xc, g_pre_ffn[l], mod3, 3, 4, *_split_bf16(wr))
        prow, pcol, starts = _route(arow, acol, T, Lc, cap_l, cap_c)
        starts_flat = starts[:, :, :nchunk + 1].reshape(-1)
        prow3 = prow.reshape(B * E, nT, TR)
        arow3 = arow.reshape(B * E, nT, TR)

        S = B * cap_l + (0 if last else B * cap_c)
        slots = None if last else (jnp.zeros((E, S, D), BF16), jnp.zeros((E, S, LANES), F32))
        slots = _gather(starts_flat, prow3, arow3, hx, slots, S_tot=S, chunk0=0, nchunk=nchunk, cap=cap_l,
                        slot_blk0=0)
        ctx_starts = jnp.tile(jnp.asarray([0, cap_c], jnp.int32), B * E)
        if not last:
            slots = _gather(ctx_starts, prow3, arow3, hx, slots, S_tot=S, chunk0=nchunk, nchunk=1, cap=cap_c,
                            slot_blk0=B * cap_l // cap_c)
        xs, gs = slots
        hid = _expert_matmul(_ffn1_kernel, xs, (w_expert_gate, w_expert_up), l)
        ye = _expert_matmul(_ffn2_kernel, hid, (w_expert_down,), l, row_scale=gs)

        if last:
            return _combine(starts_flat, pcol, ye, xc, g_post_ffn[l], mod3, 5, tile0=0, ntile=nchunk,
                            nchunk=nchunk, cap=cap_l, slot0=0, mod_row_ctx=False,
                            out=jax.ShapeDtypeStruct((B, T, D), F32))
        xc = _combine(starts_flat, pcol, ye, xc, g_post_ffn[l], mod3, 5, tile0=0, ntile=nchunk,
                      nchunk=nchunk, cap=cap_l, slot0=0, mod_row_ctx=False, out=None)
        xc = _combine(ctx_starts, pcol, ye, xc, g_post_ffn[l], mod3, 5, tile0=nT - 1, ntile=1,
                      nchunk=1, cap=cap_c, slot0=B * cap_l, mod_row_ctx=True, out=None)
```

```python
import functools

import numpy as np
import jax
import jax.numpy as jnp
from jax import lax
from jax.experimental import pallas as pl
from jax.experimental.pallas import tpu as pltpu

F32 = jnp.float32
BF16 = jnp.bfloat16

LANES = 128
HEAD_DIM = 128
MLSTM_HEADS = 4
MLSTM_DQK = 128
MLSTM_DV = 256
MLSTM_CHUNK = 256
GQA_HEADS = 8
GQA_KV_HEADS = 2
NAT_HEADS = 8
NAT_WIN_ROWS = 8
NAT_WIN_COLS = 16
GRID_W = 64
Q_BLOCK = 128
N_EXPERTS = 16
EC_CAPACITY_FACTOR = 2
ROPE_THETA = 10000.0
NORM_EPS = 1e-6
NEG_INF = -1e30
N_MOD = 6
N_BRANCHES = 3
BRANCH_WIDTH = 1024
TR = 256
N_GATE_COLS = 4 * MLSTM_HEADS

C_MQ = 0
C_MK = C_MQ + MLSTM_HEADS * MLSTM_DQK
C_MV = C_MK + MLSTM_HEADS * MLSTM_DQK
C_MO = C_MV + MLSTM_HEADS * MLSTM_DV
C_GQ = C_MO + MLSTM_HEADS * MLSTM_DV
C_GK = C_GQ + GQA_HEADS * HEAD_DIM
C_GV = C_GK + GQA_KV_HEADS * HEAD_DIM
C_NQ = C_GV + GQA_KV_HEADS * HEAD_DIM
C_NK = C_NQ + NAT_HEADS * HEAD_DIM
C_NV = C_NK + NAT_HEADS * HEAD_DIM
C_GATE = C_NV + NAT_HEADS * HEAD_DIM
W_IN_GATE_COL = C_GQ


def _params(sem, vmem_mb=None):
    return pltpu.CompilerParams(
        dimension_semantics=sem,
        vmem_limit_bytes=None if vmem_mb is None else vmem_mb * 2 ** 20)


def _pick(n, cands):
    for c in cands:
        if n % c == 0:
            return c
    raise ValueError(f"no tile for {n} in {cands}")


def _dot(a, b):
    return jnp.dot(a, b, preferred_element_type=F32)


def _dot_nt(a, b):
    return lax.dot_general(a, b, (((1,), (1,)), ((), ())), preferred_element_type=F32)


def _sigmoid(x):
    return 0.5 * jnp.tanh(0.5 * x) + 0.5


def _rms(x):
    return x * lax.rsqrt(jnp.mean(x * x, axis=-1, keepdims=True) + NORM_EPS)


def _mod_kernel(c_ref, w_ref, b_ref, o_ref):
    c = c_ref[...]
    a = (c * _sigmoid(c)).astype(BF16)
    o_ref[0] = _dot(a, w_ref[0].astype(BF16)) + b_ref[0]


def _modulation(cc, w_mod, b_mod):
    L, D, N = w_mod.shape
    tn = _pick(N, (768, 512, 256, 128))
    return pl.pallas_call(
        _mod_kernel,
        grid=(L, N // tn),
        in_specs=[pl.BlockSpec((cc.shape[0], D), lambda l, n: (0, 0)),
                  pl.BlockSpec((1, D, tn), lambda l, n: (l, 0, n)),
                  pl.BlockSpec((1, 1, tn), lambda l, n: (l, 0, n))],
        out_specs=pl.BlockSpec((1, cc.shape[0], tn), lambda l, n: (l, 0, n)),
        out_shape=jax.ShapeDtypeStruct((L, cc.shape[0], N), F32),
        compiler_params=_params(("parallel", "parallel")),
    )(cc, w_mod, b_mod.reshape(L, 1, N))


def _norm_mod_router_kernel(x_ref, g_ref, sh_ref, sc_ref, whi_ref, wlo_ref, o_ref, acol_ref, arow_ref):
    y = _rms(x_ref[0]) * g_ref[...]
    h = y * (1.0 + sc_ref[0]) + sh_ref[0]
    hb = h.astype(BF16)
    o_ref[0] = hb
    hlo = (h - hb.astype(F32)).astype(BF16)
    logits = _dot(hb, whi_ref[...]) + _dot(hlo, whi_ref[...]) + _dot(hb, wlo_ref[...])
    lane = lax.broadcasted_iota(jnp.int32, logits.shape, 1)
    valid = lane < N_EXPERTS
    logits = jnp.where(valid, logits, NEG_INF)
    ex = jnp.where(valid, jnp.exp(logits - jnp.max(logits, axis=1, keepdims=True)), 0.0)
    aff = ex / jnp.sum(ex, axis=1, keepdims=True)
    acol_ref[0] = aff
    arow_ref[0] = aff.T[:N_EXPERTS]


def _norm_mod_router(xc, g, mod3, shift_chunk, scale_chunk, whi, wlo):
    B, Tt, D = xc.shape
    nT = Tt // TR
    tile = pl.BlockSpec((1, TR, D), lambda b, i: (b, i, 0))
    row = lambda b, i: jnp.where(i == nT - 1, B, b)
    in_specs = [tile, pl.BlockSpec((1, D), lambda b, i: (0, 0)),
                pl.BlockSpec((1, 1, D), lambda b, i: (row(b, i), 0, shift_chunk)),
                pl.BlockSpec((1, 1, D), lambda b, i: (row(b, i), 0, scale_chunk))]
    args = [xc, g.reshape(1, D), mod3, mod3]
    wspec = pl.BlockSpec((D, LANES), lambda b, i: (0, 0))
    return pl.pallas_call(
        _norm_mod_router_kernel, grid=(B, nT), in_specs=in_specs + [wspec, wspec],
        out_specs=[tile, pl.BlockSpec((1, TR, LANES), lambda b, i: (b, i, 0)),
                   pl.BlockSpec((1, N_EXPERTS, TR), lambda b, i: (b, 0, i))],
        out_shape=[jax.ShapeDtypeStruct((B, Tt, D), BF16),
                   jax.ShapeDtypeStruct((B, Tt, LANES), F32),
                   jax.ShapeDtypeStruct((B, N_EXPERTS, Tt), F32)],
        compiler_params=_params(("parallel", "parallel")),
    )(*args, whi, wlo)


PROJ_ROW_SPLIT = 4


def _proj_kernel(x_ref, g_ref, shx_ref, scx_ref, shc_ref, scc_ref, w_ref, wg_ref, o_ref, og_ref, h_ref, *, T):
    tm = h_ref.shape[0]
    tn = o_ref.shape[2]
    sub = tm // PROJ_ROW_SPLIT
    n = pl.program_id(2)

    @pl.when(n == 0)
    def _():
        for k in range(PROJ_ROW_SPLIT):
            rows = slice(k * sub, (k + 1) * sub)
            y = _rms(x_ref[0, rows, :]) * g_ref[...]
            tok = pl.program_id(1) * tm + k * sub + lax.broadcasted_iota(jnp.int32, (sub, 1), 0)
            is_ctx = tok >= T
            sc = jnp.where(is_ctx, scc_ref[0], scx_ref[0])
            sh = jnp.where(is_ctx, shc_ref[0], shx_ref[0])
            h = (y * (1.0 + sc) + sh).astype(BF16)
            h_ref[rows, :] = h
            og_ref[0, rows, :] = _dot(h, wg_ref[...])

    acc = _dot(h_ref[...], w_ref[...])
    is_gate = n * tn + lax.broadcasted_iota(jnp.int32, (1, tn), 1) >= C_GATE
    o_ref[0] = jnp.where(is_gate, _sigmoid(acc), acc).astype(o_ref.dtype)


def _project(xc, g, mod3, w_main, w_g, T):
    B, Tt, D = xc.shape
    N = w_main.shape[1]
    tm = Tt // PROJ_ROW_SPLIT
    tn = _pick(N, (1152, 768, 512, 384, 256, 128))
    mrow = lambda row, chunk: pl.BlockSpec((1, 1, D), lambda b, i, n: (row(b), 0, chunk))
    own, ctx = (lambda b: b), (lambda b: B)
    return pl.pallas_call(
        functools.partial(_proj_kernel, T=T), grid=(B, PROJ_ROW_SPLIT, N // tn),
        in_specs=[pl.BlockSpec((1, tm, D), lambda b, i, n: (b, i, 0)),
                  pl.BlockSpec((1, D), lambda b, i, n: (0, 0)),
                  mrow(own, 0), mrow(own, 1), mrow(ctx, 0), mrow(ctx, 1),
                  pl.BlockSpec((D, tn), lambda b, i, n: (0, n)),
                  pl.BlockSpec((D, LANES), lambda b, i, n: (0, 0))],
        out_specs=[pl.BlockSpec((1, tm, tn), lambda b, i, n: (b, i, n)),
                   pl.BlockSpec((1, tm, LANES), lambda b, i, n: (b, i, 0))],
        out_shape=[jax.ShapeDtypeStruct((B, Tt, N), BF16), jax.ShapeDtypeStruct((B, Tt, LANES), F32)],
        scratch_shapes=[pltpu.VMEM((tm, D), BF16)],
        compiler_params=_params(("parallel", "parallel", "arbitrary"), 48),
    )(xc, g.reshape(1, D), mod3, mod3, mod3, mod3, w_main, w_g)


def _qk_prep_kernel(q_ref, k_ref, nw_ref, cos_ref, sin_ref, o_ref):
    lane = lax.broadcasted_iota(jnp.int32, (q_ref.shape[1], HEAD_DIM), 1)
    first = (lane % (HEAD_DIM // 2)) < HEAD_DIM // 4
    cos = cos_ref[...]
    sin = sin_ref[...]
    for h in range(o_ref.shape[2] // HEAD_DIM):
        src, hh = (q_ref, h) if h < GQA_HEADS else (k_ref, h - GQA_HEADS)
        y = _rms(src[0, :, hh * HEAD_DIM:(hh + 1) * HEAD_DIM].astype(F32)) * nw_ref[:, h * HEAD_DIM:(h + 1) * HEAD_DIM]
        partner = jnp.where(first, pltpu.roll(y, HEAD_DIM - HEAD_DIM // 4, 1), pltpu.roll(y, HEAD_DIM // 4, 1))
        o_ref[0, :, h * HEAD_DIM:(h + 1) * HEAD_DIM] = (y * cos + partner * sin).astype(BF16)


def _qk_prep(P, nw, cos_t, sin_t):
    B, Tt, _ = P.shape
    qw, kw = GQA_HEADS * HEAD_DIM, GQA_KV_HEADS * HEAD_DIM
    w = qw + kw
    tr = _pick(Tt, (544, TR))
    return pl.pallas_call(
        _qk_prep_kernel, grid=(B, Tt // tr),
        in_specs=[pl.BlockSpec((1, tr, qw), lambda b, i: (b, i, C_GQ // qw)),
                  pl.BlockSpec((1, tr, kw), lambda b, i: (b, i, C_GK // kw)),
                  pl.BlockSpec((1, w), lambda b, i: (0, 0)),
                  pl.BlockSpec((tr, HEAD_DIM), lambda b, i: (i, 0)),
                  pl.BlockSpec((tr, HEAD_DIM), lambda b, i: (i, 0))],
        out_specs=pl.BlockSpec((1, tr, w), lambda b, i: (b, i, 0)),
        out_shape=jax.ShapeDtypeStruct((B, Tt, w), BF16),
        compiler_params=_params(("parallel", "parallel")),
    )(P, P, nw.reshape(1, w), cos_t, sin_t)


def _flash_kernel(q_ref, k_ref, v_ref, o_ref, m_ref, acc_ref, *, R, Lc, ck, n_lat_tiles):
    tile = pl.program_id(2)

    m_ref[...] = jnp.full(m_ref.shape, NEG_INF, F32)
    acc_ref[...] = jnp.zeros(acc_ref.shape, F32)

    def step(k, v):
        vaug = jnp.concatenate([v, jnp.ones(v.shape, BF16)], axis=1)
        for r in range(R):
            rows = slice(r * TR, (r + 1) * TR)
            s = _dot_nt(q_ref[0, :, r * HEAD_DIM:(r + 1) * HEAD_DIM], k)
            slabs = [s[:, c * LANES:(c + 1) * LANES] for c in range(s.shape[1] // LANES)]
            m_old = m_ref[rows, :]
            m_new = jnp.maximum(m_old, jnp.max(functools.reduce(jnp.maximum, slabs), axis=1, keepdims=True))
            alpha = jnp.exp(m_old - m_new)
            p = jnp.concatenate([jnp.exp(sl - m_new) for sl in slabs], axis=1).astype(BF16)
            acc_ref[rows, :] = jnp.concatenate([alpha, alpha], axis=1) * acc_ref[rows, :] + _dot(p, vaug)
            m_ref[rows, :] = m_new

    kv_rows = k_ref.shape[1]

    @pl.when(tile >= n_lat_tiles)
    def _():
        step(k_ref[0, kv_rows - Lc:, :], v_ref[0, kv_rows - Lc:, :])

    @pl.when(tile < n_lat_tiles)
    def _():
        def body(j, carry):
            r0 = pl.multiple_of(j * ck, ck)
            step(k_ref[0, pl.ds(r0, ck), :], v_ref[0, pl.ds(r0, ck), :])
            return carry
        lax.fori_loop(0, kv_rows // ck, body, 0)

    o = acc_ref[:, :HEAD_DIM] / acc_ref[:, HEAD_DIM:]
    for r in range(R):
        o_ref[0, :, r * HEAD_DIM:(r + 1) * HEAD_DIM] = o[r * TR:(r + 1) * TR].astype(o_ref.dtype)


def _flash(q_arr, q_col, k_arr, k_col, v_arr, v_col, *, G, R, T, Lc):
    B, Tt, _ = q_arr.shape
    qw = R * HEAD_DIM
    ck = max(c for c in range(LANES, 2304 + 1, LANES) if Tt % c == 0)
    return pl.pallas_call(
        functools.partial(_flash_kernel, R=R, Lc=Lc, ck=ck, n_lat_tiles=T // TR),
        grid=(B, G, Tt // TR),
        in_specs=[pl.BlockSpec((1, TR, qw), lambda b, g, i: (b, i, q_col // qw + g)),
                  pl.BlockSpec((1, Tt, HEAD_DIM), lambda b, g, i: (b, 0, k_col // HEAD_DIM + g)),
                  pl.BlockSpec((1, Tt, HEAD_DIM), lambda b, g, i: (b, 0, v_col // HEAD_DIM + g))],
        out_specs=pl.BlockSpec((1, TR, qw), lambda b, g, i: (b, i, g)),
        out_shape=jax.ShapeDtypeStruct((B, Tt, G * qw), BF16),
        scratch_shapes=[pltpu.VMEM((R * TR, LANES), F32), pltpu.VMEM((R * TR, 2 * HEAD_DIM), F32)],
        compiler_params=_params(("parallel", "parallel", "arbitrary"), 48),
    )(q_arr, k_arr, v_arr)


def _nat_geometry(T):
    rows = T // GRID_W
    kh = min(NAT_WIN_ROWS, rows)
    nkr = min(kh + 1, rows)
    qbr = Q_BLOCK // GRID_W
    nbr = rows // qbr
    kb = np.zeros(nbr, np.int32)
    var = np.zeros(nbr, np.int32)
    variants = []
    for j in range(nbr):
        qrow = j * qbr + np.arange(qbr)
        start_r = np.clip(qrow - kh // 2, 0, rows - kh)
        kb[j] = min(start_r[0], rows - nkr)
        sig = (int(kb[j] - j * qbr), tuple(int(s - kb[j]) for s in start_r))
        if sig not in variants:
            variants.append(sig)
        var[j] = variants.index(sig)
    return kh, nkr, qbr, nbr, kb, var, variants


def _nat_bias_kernel(rb_ref, o_ref, *, variants, kh, nkr, qbr):
    h = pl.program_id(0)
    ndr = 2 * NAT_WIN_ROWS - 1
    ndc = 2 * NAT_WIN_COLS - 1
    qc = lax.broadcasted_iota(jnp.int32, (GRID_W, GRID_W), 0)
    kc = lax.broadcasted_iota(jnp.int32, (GRID_W, GRID_W), 1)
    dc = kc - qc + NAT_WIN_COLS - 1
    start_c = jnp.clip(qc - NAT_WIN_COLS // 2, 0, GRID_W - NAT_WIN_COLS)
    col_in = (kc >= start_c) & (kc < start_c + NAT_WIN_COLS)
    masked = jnp.full((GRID_W, GRID_W), NEG_INF, F32)
    toeplitz = []
    for dr in range(ndr):
        t = jnp.zeros((GRID_W, GRID_W), F32)
        for d in range(ndc):
            t = jnp.where(dc == d, rb_ref[(h * ndr + dr) * ndc + d], t)
        toeplitz.append(jnp.where(col_in, t, NEG_INF))
    for v, (delta, srel) in enumerate(variants):
        for qr in range(qbr):
            blocks = []
            for kr in range(nkr):
                if srel[qr] <= kr < srel[qr] + kh:
                    blocks.append(toeplitz[int(np.clip(kr + delta - qr + NAT_WIN_ROWS - 1, 0, ndr - 1))])
                else:
                    blocks.append(masked)
            rows = slice(qr * GRID_W, (qr + 1) * GRID_W)
            for kr in range(0, nkr - 1, 2):
                o_ref[v, 0, rows, kr * GRID_W:(kr + 2) * GRID_W] = jnp.concatenate(blocks[kr:kr + 2], axis=1)
            if nkr % 2:
                o_ref[v, 0, rows, (nkr - 1) * GRID_W:] = blocks[-1]


def _nat_bias_tables(rel_bias, T):
    kh, nkr, qbr, _, _, _, variants = _nat_geometry(T)
    V = len(variants)
    nk = nkr * GRID_W
    return pl.pallas_call(
        functools.partial(_nat_bias_kernel, variants=variants, kh=kh, nkr=nkr, qbr=qbr),
        grid=(NAT_HEADS,),
        in_specs=[pl.BlockSpec(memory_space=pltpu.SMEM)],
        out_specs=pl.BlockSpec((V, 1, Q_BLOCK, nk), lambda h: (0, h, 0, 0)),
        out_shape=jax.ShapeDtypeStruct((V, NAT_HEADS, Q_BLOCK, nk), F32),
        compiler_params=_params(("parallel",)),
    )(rel_bias.reshape(-1))


def _nat_kernel(kb_ref, var_ref, q_ref, k_ref, v_ref, bias_ref, o_ref, s1_ref, s2_ref, *, T, nbr, nk, scale):
    kc = k_ref[0, T:, :]
    vc = v_ref[0, T:, :]

    def scores(j, slot):
        q = q_ref[0, pl.ds(pl.multiple_of(j * Q_BLOCK, Q_BLOCK), Q_BLOCK), :]
        r0 = pl.multiple_of(kb_ref[j] * GRID_W, GRID_W)
        s1_ref[slot] = _dot_nt(q, k_ref[0, pl.ds(r0, nk), :]) * scale + bias_ref[var_ref[j], 0]
        s2_ref[slot] = _dot_nt(q, kc) * scale

    def finish(j, slot):
        s1 = s1_ref[slot]
        s2 = s2_ref[slot]
        r0 = pl.multiple_of(kb_ref[j] * GRID_W, GRID_W)
        m = jnp.maximum(jnp.max(s1, axis=1, keepdims=True), jnp.max(s2, axis=1, keepdims=True))
        p1 = jnp.exp(s1 - m)
        p2 = jnp.exp(s2 - m)
        l = jnp.sum(p1, axis=1, keepdims=True) + jnp.sum(p2, axis=1, keepdims=True)
        o = _dot(p1.astype(BF16), v_ref[0, pl.ds(r0, nk), :]) + _dot(p2.astype(BF16), vc)
        o_ref[0, pl.ds(pl.multiple_of(j * Q_BLOCK, Q_BLOCK), Q_BLOCK), :] = (o / l).astype(o_ref.dtype)

    scores(0, 0)

    def body(i, carry):
        j = 2 * i
        scores(j + 1, 1)
        finish(j, 0)
        scores(jnp.minimum(j + 2, nbr - 1), 0)
        finish(j + 1, 1)
        return carry

    lax.fori_loop(0, nbr // 2, body, 0, unroll=4)

    s = _dot_nt(q_ref[0, T:, :], kc) * scale
    p = jnp.exp(s - jnp.max(s, axis=1, keepdims=True))
    o_ref[0, T:, :] = (_dot(p.astype(BF16), vc) / jnp.sum(p, axis=1, keepdims=True)).astype(o_ref.dtype)


def _nat(P, bias_tab, T, Lc, scale):
    B, Tt, _ = P.shape
    _, nkr, _, nbr, kb, var, _ = _nat_geometry(T)
    nk = nkr * GRID_W
    V = bias_tab.shape[0]
    grid_spec = pltpu.PrefetchScalarGridSpec(
        num_scalar_prefetch=2, grid=(B, NAT_HEADS),
        in_specs=[pl.BlockSpec((1, Tt, HEAD_DIM), lambda b, h, kb_, vr: (b, 0, C_NQ // HEAD_DIM + h)),
                  pl.BlockSpec((1, Tt, HEAD_DIM), lambda b, h, kb_, vr: (b, 0, C_NK // HEAD_DIM + h)),
                  pl.BlockSpec((1, Tt, HEAD_DIM), lambda b, h, kb_, vr: (b, 0, C_NV // HEAD_DIM + h)),
                  pl.BlockSpec((V, 1, Q_BLOCK, nk), lambda b, h, kb_, vr: (0, h, 0, 0))],
        out_specs=pl.BlockSpec((1, Tt, HEAD_DIM), lambda b, h, kb_, vr: (b, 0, h)),
        scratch_shapes=[pltpu.VMEM((2, Q_BLOCK, nk), F32), pltpu.VMEM((2, Q_BLOCK, Lc), F32)])
    assert nbr % 2 == 0
    return pl.pallas_call(
        functools.partial(_nat_kernel, T=T, nbr=nbr, nk=nk, scale=scale),
        grid_spec=grid_spec,
        out_shape=jax.ShapeDtypeStruct((B, Tt, NAT_HEADS * HEAD_DIM), BF16),
        compiler_params=_params(("parallel", "parallel")),
    )(jnp.asarray(kb), jnp.asarray(var), P, P, P, bias_tab)


def _mlstm_kernel(qf_ref, kf_ref, vf_ref, gf_ref, ktf_ref, qb_ref, kb_ref, vb_ref, gb_ref, ktb_ref, bias_ref,
                  hf_ref, hb_ref, st_ref, m_ref, *, kscale):
    L = MLSTM_CHUNK
    dv = MLSTM_DV

    @pl.when(pl.program_id(1) == 0)
    def _():
        st_ref[...] = jnp.zeros(st_ref.shape, F32)
        m_ref[...] = jnp.full(m_ref.shape, NEG_INF, F32)

    ti = lax.broadcasted_iota(jnp.int32, (L, L), 0)
    si = lax.broadcasted_iota(jnp.int32, (L, L), 1)
    ones_col = (lax.broadcasted_iota(jnp.int32, (L, LANES), 1) == 0).astype(BF16)

    for dirn, (q_ref, k_ref, v_ref, g_ref, kt_ref, h_ref) in enumerate(
            ((qf_ref, kf_ref, vf_ref, gf_ref, ktf_ref, hf_ref), (qb_ref, kb_ref, vb_ref, gb_ref, ktb_ref, hb_ref))):
        tri = (si <= ti) if dirn == 0 else (si >= ti)
        G = g_ref[0] + bias_ref[...]
        LF = jnp.minimum(G, 0.0) - jnp.log(1.0 + jnp.exp(-jnp.abs(G)))
        lf_hi = LF.astype(BF16)
        lf_lo = (LF - lf_hi.astype(F32)).astype(BF16)
        trib = tri.astype(BF16)
        Bc = _dot(trib, lf_hi) + _dot(trib, lf_lo)
        BcT = Bc.T
        GT = G.T
        bl_row = Bc[L - 1:L, :] if dirn == 0 else Bc[0:1, :]
        for h in range(MLSTM_HEADS):
            ci = dirn * 2 * MLSTM_HEADS + h
            cf = ci + MLSTM_HEADS
            bcol = Bc[:, cf:cf + 1]
            brow = BcT[cf:cf + 1, :]
            licol = G[:, ci:ci + 1]
            lirow = GT[ci:ci + 1, :]
            bl = bl_row[:, cf:cf + 1]
            m_old = m_ref[dirn, h]
            gcol = bl - bcol + licol
            m_new = jnp.maximum(bl + m_old, jnp.max(gcol, axis=0, keepdims=True))
            decay = jnp.exp(bl + m_old - m_new)
            wk = jnp.exp(gcol - m_new)
            dmat = jnp.where(tri, bcol - brow + lirow, NEG_INF)
            inter = bcol + m_old
            m_t = jnp.maximum(inter, jnp.max(dmat, axis=1, keepdims=True))
            w = jnp.exp(dmat - m_t)
            a = jnp.exp(inter - m_t)
            qh = q_ref[0, :, h * MLSTM_DQK:(h + 1) * MLSTM_DQK]
            kh = k_ref[0, :, h * MLSTM_DQK:(h + 1) * MLSTM_DQK]
            vaug = jnp.concatenate([v_ref[0, :, h * dv:(h + 1) * dv], ones_col], axis=1)
            smat = _dot_nt(qh, kh) * kscale * w
            state = st_ref[dirn, h]
            nd = _dot(smat.astype(BF16), vaug) + a * _dot(qh, state.astype(BF16))
            den = jnp.maximum(jnp.abs(nd[:, dv:dv + 1]), jnp.exp(-m_t))
            h_ref[0, :, h * dv:(h + 1) * dv] = (nd[:, :dv] / den).astype(h_ref.dtype)
            kT = kt_ref[0, h * MLSTM_DQK:(h + 1) * MLSTM_DQK, :]
            upd = _dot(kT, (wk * vaug.astype(F32)).astype(BF16)) * kscale
            st_ref[dirn, h] = decay * state + upd
            m_ref[dirn, h] = m_new


def _mlstm(P, gates, gate_bias, T, Lc):
    B, Tt, _ = P.shape
    L = MLSTM_CHUNK
    kt = jnp.swapaxes(P[:, :, C_MK:C_MK + MLSTM_HEADS * MLSTM_DQK], 1, 2)
    ncl, ncc = T // L, Lc // L
    fwd = lambda i: jnp.where(i < ncc, ncl + i, i - ncc)
    bwd = lambda i: jnp.where(i < ncc, ncl + ncc - 1 - i, ncl - 1 - (i - ncc))
    qw = MLSTM_HEADS * MLSTM_DQK
    vw = MLSTM_HEADS * MLSTM_DV

    def specs(order):
        return [pl.BlockSpec((1, L, qw), lambda b, i: (b, order(i), C_MQ // qw)),
                pl.BlockSpec((1, L, qw), lambda b, i: (b, order(i), C_MK // qw)),
                pl.BlockSpec((1, L, vw), lambda b, i: (b, order(i), C_MV // vw)),
                pl.BlockSpec((1, L, LANES), lambda b, i: (b, order(i), 0)),
                pl.BlockSpec((1, qw, L), lambda b, i: (b, 0, order(i)))]

    out = jax.ShapeDtypeStruct((B, Tt, vw), BF16)
    return pl.pallas_call(
        functools.partial(_mlstm_kernel, kscale=MLSTM_DQK ** -0.5),
        grid=(B, ncl + ncc),
        in_specs=specs(fwd) + specs(bwd) + [pl.BlockSpec((1, LANES), lambda b, i: (0, 0))],
        out_specs=[pl.BlockSpec((1, L, vw), lambda b, i: (b, fwd(i), 0)),
                   pl.BlockSpec((1, L, vw), lambda b, i: (b, bwd(i), 0))],
        out_shape=[out, out],
        scratch_shapes=[pltpu.VMEM((2, MLSTM_HEADS, MLSTM_DQK, MLSTM_DV + LANES), F32),
                        pltpu.VMEM((2, MLSTM_HEADS, 1, 1), F32)],
        compiler_params=_params(("parallel", "arbitrary")),
    )(P, P, P, gates, kt, P, P, P, gates, kt, gate_bias)


def _merge_kernel(hf_ref, hb_ref, mo_ref, yg_ref, yn_ref, g0_ref, g1_ref, g2_ref, wb_ref, o_ref, ym_ref):
    @pl.when(pl.program_id(1) == 0)
    def _():
        h = hf_ref[...].astype(F32) + hb_ref[...].astype(F32)
        for hh in range(MLSTM_HEADS):
            cols = slice(hh * MLSTM_DV, (hh + 1) * MLSTM_DV)
            ym_ref[:, cols] = (_sigmoid(mo_ref[:, cols].astype(F32)) * _rms(h[:, cols])).astype(BF16)

    gate = lambda r: r[...].astype(F32)
    acc = gate(g0_ref) * _dot(ym_ref[...], wb_ref[0])
    acc += gate(g1_ref) * _dot(yg_ref[...], wb_ref[1])
    acc += gate(g2_ref) * _dot(yn_ref[...], wb_ref[2])
    o_ref[...] = acc.astype(BF16)


def _merge(hf, hb, P2, yg, yn, wb):
    M, W = hf.shape
    D = wb.shape[2]
    tm = _pick(M, (1024, 512, 256))
    tn = _pick(D, (512, 256))
    row = lambda c: pl.BlockSpec((tm, W), lambda i, n: (i, c))
    gate = lambda br: pl.BlockSpec((tm, tn), lambda i, n: (i, (C_GATE + br * D) // tn + n))
    return pl.pallas_call(
        _merge_kernel, grid=(M // tm, D // tn),
        in_specs=[row(0), row(0), row(C_MO // W), row(0), row(0), gate(0), gate(1), gate(2),
                  pl.BlockSpec((N_BRANCHES, W, tn), lambda i, n: (0, 0, n))],
        out_specs=pl.BlockSpec((tm, tn), lambda i, n: (i, n)),
        out_shape=jax.ShapeDtypeStruct((M, D), BF16),
        scratch_shapes=[pltpu.VMEM((tm, W), BF16)],
        compiler_params=_params(("parallel", "arbitrary"), 56),
    )(hf, hb, P2, yg, yn, P2, P2, P2, wb)


def _wout_post_kernel(a_ref, w_ref, x_ref, gp_ref, gate_ref, o_ref):
    y = _dot(a_ref[0], w_ref[...])
    o_ref[0] = x_ref[0] + gate_ref[0] * (_rms(y) * gp_ref[...])


def _wout_post(merged, w_out, xc, g_post, mod3, gate_chunk):
    B, Tt, D = xc.shape
    nT = Tt // TR
    tile = pl.BlockSpec((1, TR, D), lambda b, i: (b, i, 0))
    row = lambda b, i: jnp.where(i == nT - 1, B, b)
    return pl.pallas_call(
        _wout_post_kernel, grid=(B, nT),
        in_specs=[tile, pl.BlockSpec((D, D), lambda b, i: (0, 0)), tile,
                  pl.BlockSpec((1, D), lambda b, i: (0, 0)),
                  pl.BlockSpec((1, 1, D), lambda b, i: (row(b, i), 0, gate_chunk))],
        out_specs=tile,
        out_shape=jax.ShapeDtypeStruct((B, Tt, D), F32),
        compiler_params=_params(("parallel", "parallel"), 48),
    )(merged, w_out, xc, g_post.reshape(1, D), mod3)


def _route_segment(arow, acol, k, chunk):
    E, T = arow.shape
    nchunk = T // chunk
    ind = lambda m: jnp.where(m, 1.0, 0.0)
    bits = pltpu.bitcast(arow, jnp.int32)
    thr = jnp.zeros((E, 1), jnp.int32)
    for bit in range(30, -1, -1):
        cand = thr | (1 << bit)
        cnt = jnp.sum(ind(bits >= cand), axis=1, keepdims=True)
        thr = jnp.where(cnt >= k, cand, thr)
    thr_val = pltpu.bitcast(thr, F32)
    gt = arow > thr_val
    eq = arow == thr_val
    need = k - jnp.sum(ind(gt), axis=1, keepdims=True)

    r_i = lax.broadcasted_iota(jnp.int32, (chunk, chunk), 0)
    c_i = lax.broadcasted_iota(jnp.int32, (chunk, chunk), 1)
    before = ind(r_i < c_i).astype(BF16)
    after = ind(c_i < r_i).astype(BF16)

    def prefix_rows(mask):
        carry = jnp.zeros((E, 1), F32)
        parts, carries = [], [carry]
        for c in range(nchunk):
            mc = ind(mask[:, c * chunk:(c + 1) * chunk]).astype(BF16)
            parts.append(_dot(mc, before) + carry)
            carry = carry + jnp.sum(mc.astype(F32), axis=1, keepdims=True)
            carries.append(carry)
        return jnp.concatenate(parts, axis=1), carries

    tie_rank, _ = prefix_rows(eq)
    sel = gt | (eq & (tie_rank < need))
    pos, carries = prefix_rows(sel)
    prow = jnp.where(sel, pos, -1.0)
    lane = lax.broadcasted_iota(jnp.int32, (E, LANES), 1)
    starts = jnp.zeros((E, LANES), F32)
    for c, cv in enumerate(carries):
        starts = jnp.where(lane == c, cv, starts)

    sub = lax.broadcasted_iota(jnp.int32, (E, LANES), 0)
    diag = sub == lane
    thr_row = jnp.sum(jnp.where(diag, thr_val, 0.0), axis=0, keepdims=True)
    need_row = jnp.sum(jnp.where(diag, need, 0.0), axis=0, keepdims=True)
    lane_ok = lax.broadcasted_iota(jnp.int32, (1, LANES), 1) < E
    gt_c = (acol > thr_row) & lane_ok
    eq_c = (acol == thr_row) & lane_ok

    def prefix_cols(mask):
        carry = jnp.zeros((1, LANES), F32)
        parts = []
        for c in range(nchunk):
            mc = ind(mask[c * chunk:(c + 1) * chunk]).astype(BF16)
            parts.append(_dot(after, mc) + carry)
            carry = carry + jnp.sum(mc.astype(F32), axis=0, keepdims=True)
        return jnp.concatenate(parts, axis=0)

    sel_c = gt_c | (eq_c & (prefix_cols(eq_c) < need_row))
    pcol = jnp.where(sel_c, prefix_cols(sel_c), -1.0)
    return prow, pcol, starts


def _route_kernel(arow_ref, acol_ref, prow_ref, pcol_ref, starts_ref, *, T, Lc, cap_l, cap_c):
    prow, pcol, starts = _route_segment(arow_ref[0, :, :T], acol_ref[0, :T], cap_l, TR)
    prow_ref[0, :, :T] = prow
    pcol_ref[0, :T] = pcol
    starts_ref[0] = starts.astype(jnp.int32)
    prow, pcol, _ = _route_segment(arow_ref[0, :, T:], acol_ref[0, T:], cap_c, TR)
    prow_ref[0, :, T:] = prow
    pcol_ref[0, T:] = pcol


def _route(arow, acol, T, Lc, cap_l, cap_c):
    B, E, Tt = arow.shape
    return pl.pallas_call(
        functools.partial(_route_kernel, T=T, Lc=Lc, cap_l=cap_l, cap_c=cap_c),
        grid=(B,),
        in_specs=[pl.BlockSpec((1, E, Tt), lambda b: (b, 0, 0)), pl.BlockSpec((1, Tt, LANES), lambda b: (b, 0, 0))],
        out_specs=[pl.BlockSpec((1, E, Tt), lambda b: (b, 0, 0)), pl.BlockSpec((1, Tt, LANES), lambda b: (b, 0, 0)),
                   pl.BlockSpec((1, E, LANES), lambda b: (b, 0, 0))],
        out_shape=[jax.ShapeDtypeStruct((B, E, Tt), F32), jax.ShapeDtypeStruct((B, Tt, LANES), F32),
                   jax.ShapeDtypeStruct((B, E, LANES), jnp.int32)],
        compiler_params=_params(("parallel",), 48),
    )(arow, acol)


def _gather_kernel(st_ref, p_ref, a_ref, h_ref, o_ref, g_ref, acc_ref, gacc_ref, *, nchunk, cap_l, cap_c, st):
    b = pl.program_id(0)
    e = pl.program_id(1)
    base = (b * N_EXPERTS + e) * (nchunk + 1)
    acc_ref[...] = jnp.zeros(acc_ref.shape, F32)
    gacc_ref[...] = jnp.zeros(gacc_ref.shape, F32)
    tile_row = lax.broadcasted_iota(jnp.int32, (st, TR), 0)

    def chunk(c, carry):
        lo = st_ref[base + c]
        hi = st_ref[base + c + 1]
        pos = p_ref[0, pl.ds(c, 1), :]
        aff = a_ref[0, pl.ds(c, 1), :]
        hc = h_ref[0, pl.ds(pl.multiple_of(c * TR, TR), TR), :]

        def tile(S, carry2):
            s0 = pl.multiple_of(S * st, st)
            hit = pos == (tile_row + s0).astype(F32)
            acc_ref[pl.ds(s0, st), :] += _dot(jnp.where(hit, 1.0, 0.0).astype(BF16), hc)
            gacc_ref[pl.ds(s0, st), :] += jnp.sum(jnp.where(hit, aff, 0.0), axis=1, keepdims=True)
            return carry2

        lax.fori_loop(lo // st, (hi + st - 1) // st, tile, 0)
        return carry

    lax.fori_loop(0, nchunk, chunk, 0)
    o_ref[0, :cap_l, :] = acc_ref[...].astype(BF16)
    g_ref[0, :cap_l, :] = gacc_ref[...]

    if cap_c:
        hit = p_ref[0, nchunk:nchunk + 1, :] == lax.broadcasted_iota(jnp.int32, (cap_c, TR), 0).astype(F32)
        o_ref[0, cap_l:, :] = _dot(jnp.where(hit, 1.0, 0.0).astype(BF16), h_ref[0, nchunk * TR:, :]).astype(BF16)
        g_ref[0, cap_l:, :] = jnp.broadcast_to(
            jnp.sum(jnp.where(hit, a_ref[0, nchunk:nchunk + 1, :], 0.0), axis=1, keepdims=True), (cap_c, LANES))


def _gather(starts, prow3, arow3, hbf, *, nchunk, cap_l, cap_c):
    B, Tt, D = hbf.shape
    E = N_EXPERTS
    nT = Tt // TR
    st = min(cap_l, 128)
    capb = cap_l + cap_c
    pspec = pl.BlockSpec((1, nT, TR), lambda b, e, s: (b * E + e, 0, 0))
    grid_spec = pltpu.PrefetchScalarGridSpec(
        num_scalar_prefetch=1, grid=(B, E),
        in_specs=[pspec, pspec, pl.BlockSpec((1, Tt, D), lambda b, e, s: (b, 0, 0))],
        out_specs=[pl.BlockSpec((1, capb, D), lambda b, e, s: (e, b, 0)),
                   pl.BlockSpec((1, capb, LANES), lambda b, e, s: (e, b, 0))],
        scratch_shapes=[pltpu.VMEM((cap_l, D), F32), pltpu.VMEM((cap_l, LANES), F32)])
    return pl.pallas_call(
        functools.partial(_gather_kernel, nchunk=nchunk, cap_l=cap_l, cap_c=cap_c, st=st),
        grid_spec=grid_spec,
        out_shape=[jax.ShapeDtypeStruct((E, B * capb, D), BF16), jax.ShapeDtypeStruct((E, B * capb, LANES), F32)],
        compiler_params=_params(("parallel", "arbitrary"), 56),
    )(starts, prow3, arow3, hbf)


def _ffn1_kernel(x_ref, wg_ref, wu_ref, o_ref, wgb_ref, wub_ref):
    @pl.when(pl.program_id(2) == 0)
    def _():
        wgb_ref[...] = wg_ref[0].astype(BF16)
        wub_ref[...] = wu_ref[0].astype(BF16)
    x = x_ref[0]
    g = _dot(x, wgb_ref[...])
    u = _dot(x, wub_ref[...])
    o_ref[0] = (g * _sigmoid(g) * u).astype(BF16)


def _ffn2_kernel(x_ref, w_ref, g_ref, o_ref, wb_ref):
    @pl.when(pl.program_id(2) == 0)
    def _():
        wb_ref[...] = w_ref[0].astype(BF16)
    o_ref[0] = (_dot(x_ref[0], wb_ref[...]) * g_ref[0, :, :1]).astype(BF16)


def _expert_matmul(kernel, x, ws, layer, row_scale=None):
    E, S, K = x.shape
    N = ws[0].shape[3]
    ws = [w.reshape(-1, K, N) for w in ws]
    tm = _pick(S, (1088, 1024, 544, 512, 272, 256, 160, 128))
    tn = _pick(N, (1024, 512, 256) if len(ws) == 1 else (512, 256))
    wspec = pl.BlockSpec((1, K, tn), lambda e, n, m: (layer * E + e, 0, n))
    in_specs = [pl.BlockSpec((1, tm, K), lambda e, n, m: (e, m, 0))] + [wspec] * len(ws)
    args = [x, *ws]
    if row_scale is not None:
        in_specs.append(pl.BlockSpec((1, tm, LANES), lambda e, n, m: (e, m, 0)))
        args.append(row_scale)
    return pl.pallas_call(
        kernel, grid=(E, N // tn, S // tm),
        in_specs=in_specs,
        out_specs=pl.BlockSpec((1, tm, tn), lambda e, n, m: (e, m, n)),
        out_shape=jax.ShapeDtypeStruct((E, S, N), BF16),
        scratch_shapes=[pltpu.VMEM((K, tn), BF16)] * len(ws),
        compiler_params=_params(("parallel", "parallel", "arbitrary"), 48),
    )(*args)


SLOT_ALIGN = 16
COMBINE_WINDOW = 64


def _combine_kernel(st_ref, pc_ref, ye_ref, x_ref, gp_ref, gate_ref, o_ref, win_ref, acc_ref, sem,
                    *, nchunk, cap, W, stride, off):
    E = N_EXPERTS
    b = pl.program_id(0)
    t = pl.program_id(1)
    nb, nt = pl.num_programs(0), pl.num_programs(1)
    step = b * nt + t
    buf = step % 2

    def bounds(bb, tt, e):
        base = (bb * E + e) * (nchunk + 1) + tt
        return (st_ref[base] // SLOT_ALIGN) * SLOT_ALIGN, st_ref[base + 1]

    def window(bb, tt, e, r):
        first, _ = bounds(bb, tt, e)
        want = first + r * W
        return want, jnp.minimum(want, cap - W)

    def copy(bb, tt, e, r, into):
        _, start = window(bb, tt, e, r)
        row0 = pl.multiple_of(bb * stride + off + start, SLOT_ALIGN)
        return pltpu.make_async_copy(ye_ref.at[e, pl.ds(row0, W), :],
                                     win_ref.at[into, pl.ds(e * W, W), :], sem.at[into, e])

    @pl.when(step == 0)
    def _():
        for e in range(E):
            copy(b, t, e, 0, buf).start()

    @pl.when(step + 1 < nb * nt)
    def _():
        last_t = t + 1 == nt
        b_next = jnp.where(last_t, b + 1, b)
        t_next = jnp.where(last_t, 0, t + 1)
        for e in range(E):
            copy(b_next, t_next, e, 0, 1 - buf).start()

    rounds = jnp.int32(1)
    for e in range(E):
        first, hi = bounds(b, t, e)
        rounds = jnp.maximum(rounds, (hi - first + W - 1) // W)

    pos1 = pc_ref[0] + 1.0
    pos_hi = jnp.floor(pos1 * (1.0 / SLOT_ALIGN))
    pos_lo = pos1 - pos_hi * SLOT_ALIGN
    col = lax.broadcasted_iota(jnp.int32, (LANES, E * W), 1)
    row = lax.broadcasted_iota(jnp.int32, (LANES, E * W), 0)
    expand = jnp.where(col // W == row, 1.0, 0.0).astype(BF16)
    pos_rep = SLOT_ALIGN * _dot(pos_hi.astype(BF16), expand) + _dot(pos_lo.astype(BF16), expand) - 1.0
    lane = lax.broadcasted_iota(jnp.int32, (1, E * W), 1)
    in_win = (lane % W).astype(F32)

    def onehot(r):
        want_row = jnp.zeros((1, E * W), F32)
        start_row = jnp.zeros((1, E * W), F32)
        for e in range(E):
            want, start = window(b, t, e, r)
            mine = lane // W == e
            want_row = jnp.where(mine, want.astype(F32), want_row)
            start_row = jnp.where(mine, start.astype(F32), start_row)
        hit = (pos_rep - start_row == in_win) & (pos_rep >= want_row)
        return jnp.where(hit, 1.0, 0.0).astype(BF16)

    lhs = onehot(0)
    for e in range(E):
        copy(b, t, e, 0, buf).wait()
    acc_ref[...] = _dot(lhs, win_ref[buf])

    def more(r, carry):
        for e in range(E):
            copy(b, t, e, r, buf).start()
        lhs_r = onehot(r)
        for e in range(E):
            copy(b, t, e, r, buf).wait()
        acc_ref[...] += _dot(lhs_r, win_ref[buf])
        return carry

    lax.fori_loop(1, rounds, more, 0)
    o_ref[0] = x_ref[0] + gate_ref[0] * (_rms(acc_ref[...]) * gp_ref[...])


def _combine(starts, pcol, ye, xc, g_post, mod3, gate_chunk, *, tile0, ntile, nchunk, cap, stride, off,
             mod_row_ctx, out):
    B, Tt, D = xc.shape
    E = N_EXPERTS
    W = min(cap, COMBINE_WINDOW)
    fresh = out is not None
    if not fresh:
        out = xc
    tok = lambda w: pl.BlockSpec((1, TR, w), lambda b, t, s: (b, tile0 + t, 0))
    mrow = (lambda b: B) if mod_row_ctx else (lambda b: b)
    in_specs = [tok(LANES), pl.BlockSpec(memory_space=pl.ANY), tok(D),
                pl.BlockSpec((1, D), lambda b, t, s: (0, 0)),
                pl.BlockSpec((1, 1, D), lambda b, t, s: (mrow(b), 0, gate_chunk))]
    args = [pcol, ye, xc, g_post.reshape(1, D), mod3]
    kern = functools.partial(_combine_kernel, nchunk=nchunk, cap=cap, W=W, stride=stride, off=off)
    if fresh:
        out_spec = pl.BlockSpec((1, TR, D), lambda b, t, s: (b, t, 0))
        aliases = {}
    else:
        out_spec = tok(D)
        aliases = {3: 0}
    grid_spec = pltpu.PrefetchScalarGridSpec(
        num_scalar_prefetch=1, grid=(B, ntile), in_specs=in_specs, out_specs=out_spec,
        scratch_shapes=[pltpu.VMEM((2, E * W, D), BF16), pltpu.VMEM((TR, D), F32),
                        pltpu.SemaphoreType.DMA((2, E))])
    return pl.pallas_call(
        kern, grid_spec=grid_spec,
        out_shape=jax.ShapeDtypeStruct(out.shape, F32),
        input_output_aliases=aliases,
        compiler_params=_params(("arbitrary", "arbitrary"), 48),
    )(starts, *args)


def _rope_tables(T, Lc):
    nf = HEAD_DIM // 4
    t = np.arange(T)
    inv = ROPE_THETA ** (-jnp.arange(nf, dtype=F32) / nf)
    row = jnp.asarray(t // GRID_W, F32)
    col = jnp.asarray(t % GRID_W, F32)
    ang = jnp.stack([row[:, None] * inv, col[:, None] * inv], axis=1)
    cos = jnp.broadcast_to(jnp.cos(ang)[:, :, None, :], (T, 2, 2, nf)).reshape(T, HEAD_DIM)
    sin = jnp.broadcast_to(jnp.sin(ang)[:, :, None, :], (T, 2, 2, nf)).reshape(T, HEAD_DIM)
    sign = jnp.asarray(np.tile(np.repeat([-1.0, 1.0], nf), 2), F32)
    cos = jnp.concatenate([cos, jnp.ones((Lc, HEAD_DIM), F32)], axis=0)
    sin = jnp.concatenate([sin * sign, jnp.zeros((Lc, HEAD_DIM), F32)], axis=0)
    return cos, sin


def _split_bf16(w):
    hi = w.astype(BF16)
    return hi, (w - hi.astype(F32)).astype(BF16)


def kernel(x, c, ctx, c_ctx, w_mod, b_mod, g_pre_mix, g_post_mix, g_pre_ffn, g_post_ffn, w_in, mlstm_gate_bias, gqa_q_norm, gqa_k_norm, nat_rel_bias, w_branch, w_out, w_router, w_expert_gate, w_expert_up, w_expert_down):
    B, T, D = x.shape
    Lc = ctx.shape[1]
    depth = w_mod.shape[0]
    E = N_EXPERTS
    assert Lc == TR and T % (2 * TR) == 0 and w_router.shape[2] == E
    Tt = T + Lc
    nT = Tt // TR
    M = B * Tt
    cap_l = EC_CAPACITY_FACTOR * T // E
    cap_c = EC_CAPACITY_FACTOR * Lc // E
    scale = HEAD_DIM ** -0.5
    nchunk = T // TR

    xc = jnp.concatenate([x, ctx], axis=1)
    rows_c = -(-(B + 1) // 8) * 8
    cc = jnp.zeros((rows_c, D), F32).at[:B].set(c).at[B].set(c_ctx)
    mod = _modulation(cc, w_mod, b_mod)
    cos_t, sin_t = _rope_tables(T, Lc)

    for l in range(depth):
        last = l == depth - 1
        mod3 = mod[l].reshape(rows_c, 1, N_MOD * D)
        w_l = w_in[l]
        w_main = jnp.concatenate([w_l[:, :W_IN_GATE_COL], w_l[:, W_IN_GATE_COL + N_GATE_COLS:]], axis=1).astype(BF16)
        w_g = jnp.pad(w_l[:, W_IN_GATE_COL:W_IN_GATE_COL + N_GATE_COLS], ((0, 0), (0, LANES - N_GATE_COLS))).astype(BF16)
        gate_bias = jnp.pad(mlstm_gate_bias[l], (0, LANES - N_GATE_COLS)).reshape(1, LANES)
        nw = jnp.concatenate([jnp.tile(gqa_q_norm[l][None] * scale, (GQA_HEADS, 1)),
                              jnp.tile(gqa_k_norm[l][None], (GQA_KV_HEADS, 1))], axis=0)[:, None, :]

        P, gates = _project(xc, g_pre_mix[l], mod3, w_main, w_g, T)
        P2 = P.reshape(M, -1)

        hf, hb = _mlstm(P, gates, gate_bias, T, Lc)

        qk = _qk_prep(P, nw, cos_t, sin_t)
        yg = _flash(qk, 0, qk, GQA_HEADS * HEAD_DIM, P, C_GV, G=GQA_KV_HEADS, R=GQA_HEADS // GQA_KV_HEADS,
                    T=T, Lc=Lc)

        bias_tab = _nat_bias_tables(nat_rel_bias[l], T)
        yn = _nat(P, bias_tab, T, Lc, scale)

        merged = _merge(hf.reshape(M, -1), hb.reshape(M, -1), P2, yg.reshape(M, -1), yn.reshape(M, -1),
                        w_branch[l].astype(BF16))
        xc = _wout_post(merged.reshape(B, Tt, D), w_out[l].astype(BF16), xc, g_post_mix[l], mod3, 2)

        wr = jnp.pad(w_router[l], ((0, 0), (0, LANES - E)))
        hx, acol, arow = _norm_mod_router(xc, g_pre_ffn[l], mod3, 3, 4, *_split_bf16(wr))
        prow, pcol, starts = _route(arow, acol, T, Lc, cap_l, cap_c)
        starts_flat = starts[:, :, :nchunk + 1].reshape(-1)
        prow3 = prow.reshape(B * E, nT, TR)
        arow3 = arow.reshape(B * E, nT, TR)

        cap_ctx = 0 if last else cap_c
        capb = cap_l + cap_ctx
        xs, gs = _gather(starts_flat, prow3, arow3, hx, nchunk=nchunk, cap_l=cap_l, cap_c=cap_ctx)
        hid = _expert_matmul(_ffn1_kernel, xs, (w_expert_gate, w_expert_up), l)
        ye = _expert_matmul(_ffn2_kernel, hid, (w_expert_down,), l, row_scale=gs)

        if last:
            return _combine(starts_flat, pcol, ye, xc, g_post_ffn[l], mod3, 5, tile0=0, ntile=nchunk,
                            nchunk=nchunk, cap=cap_l, stride=capb, off=0, mod_row_ctx=False,
                            out=jax.ShapeDtypeStruct((B, T, D), F32))
        xc = _combine(starts_flat, pcol, ye, xc, g_post_ffn[l], mod3, 5, tile0=0, ntile=nchunk,
                      nchunk=nchunk, cap=cap_l, stride=capb, off=0, mod_row_ctx=False, out=None)
        ctx_starts = jnp.tile(jnp.asarray([0, cap_c], jnp.int32), B * E)
        xc = _combine(ctx_starts, pcol, ye, xc, g_post_ffn[l], mod3, 5, tile0=nT - 1, ntile=1,
                      nchunk=1, cap=cap_c, stride=capb, off=cap_l, mod_row_ctx=True, out=None)
```

```python
import functools

import numpy as np
import jax
import jax.numpy as jnp
from jax import lax
from jax.experimental import pallas as pl
from jax.experimental.pallas import tpu as pltpu

F32 = jnp.float32
BF16 = jnp.bfloat16

LANES = 128
HEAD_DIM = 128
MLSTM_HEADS = 4
MLSTM_DQK = 128
MLSTM_DV = 256
MLSTM_CHUNK = 256
GQA_HEADS = 8
GQA_KV_HEADS = 2
NAT_HEADS = 8
NAT_WIN_ROWS = 8
NAT_WIN_COLS = 16
GRID_W = 64
Q_BLOCK = 128
N_EXPERTS = 16
EC_CAPACITY_FACTOR = 2
ROPE_THETA = 10000.0
NORM_EPS = 1e-6
NEG_INF = -1e30
N_MOD = 6
N_BRANCHES = 3
BRANCH_WIDTH = 1024
TR = 256
N_GATE_COLS = 4 * MLSTM_HEADS

C_MQ = 0
C_MK = C_MQ + MLSTM_HEADS * MLSTM_DQK
C_MV = C_MK + MLSTM_HEADS * MLSTM_DQK
C_MO = C_MV + MLSTM_HEADS * MLSTM_DV
C_GQ = C_MO + MLSTM_HEADS * MLSTM_DV
C_GK = C_GQ + GQA_HEADS * HEAD_DIM
C_GV = C_GK + GQA_KV_HEADS * HEAD_DIM
C_NQ = C_GV + GQA_KV_HEADS * HEAD_DIM
C_NK = C_NQ + NAT_HEADS * HEAD_DIM
C_NV = C_NK + NAT_HEADS * HEAD_DIM
C_GATE = C_NV + NAT_HEADS * HEAD_DIM
W_IN_GATE_COL = C_GQ


def _params(sem, vmem_mb=None):
    return pltpu.CompilerParams(
        dimension_semantics=sem,
        vmem_limit_bytes=None if vmem_mb is None else vmem_mb * 2 ** 20)


def _pick(n, cands):
    for c in cands:
        if n % c == 0:
            return c
    raise ValueError(f"no tile for {n} in {cands}")


def _dot(a, b):
    return jnp.dot(a, b, preferred_element_type=F32)


def _dot_nt(a, b):
    return lax.dot_general(a, b, (((1,), (1,)), ((), ())), preferred_element_type=F32)


def _sigmoid(x):
    return 0.5 * jnp.tanh(0.5 * x) + 0.5


def _rms(x):
    return x * lax.rsqrt(jnp.mean(x * x, axis=-1, keepdims=True) + NORM_EPS)


def _mod_kernel(c_ref, w_ref, b_ref, o_ref):
    c = c_ref[...]
    a = (c * _sigmoid(c)).astype(BF16)
    o_ref[0] = _dot(a, w_ref[0].astype(BF16)) + b_ref[0]


def _modulation(cc, w_mod, b_mod):
    L, D, N = w_mod.shape
    tn = _pick(N, (768, 512, 256, 128))
    return pl.pallas_call(
        _mod_kernel,
        grid=(L, N // tn),
        in_specs=[pl.BlockSpec((cc.shape[0], D), lambda l, n: (0, 0)),
                  pl.BlockSpec((1, D, tn), lambda l, n: (l, 0, n)),
                  pl.BlockSpec((1, 1, tn), lambda l, n: (l, 0, n))],
        out_specs=pl.BlockSpec((1, cc.shape[0], tn), lambda l, n: (l, 0, n)),
        out_shape=jax.ShapeDtypeStruct((L, cc.shape[0], N), F32),
        compiler_params=_params(("parallel", "parallel")),
    )(cc, w_mod, b_mod.reshape(L, 1, N))


PROJ_ROW_SPLIT = 4


def _proj_kernel(x_ref, g_ref, shx_ref, scx_ref, shc_ref, scc_ref, w_ref, wg_ref, o_ref, og_ref, h_ref, *, T):
    tm = h_ref.shape[0]
    tn = o_ref.shape[2]
    sub = tm // PROJ_ROW_SPLIT
    n = pl.program_id(2)

    @pl.when(n == 0)
    def _():
        for k in range(PROJ_ROW_SPLIT):
            rows = slice(k * sub, (k + 1) * sub)
            y = _rms(x_ref[0, rows, :]) * g_ref[...]
            tok = pl.program_id(1) * tm + k * sub + lax.broadcasted_iota(jnp.int32, (sub, 1), 0)
            is_ctx = tok >= T
            sc = jnp.where(is_ctx, scc_ref[0], scx_ref[0])
            sh = jnp.where(is_ctx, shc_ref[0], shx_ref[0])
            h = (y * (1.0 + sc) + sh).astype(BF16)
            h_ref[rows, :] = h
            og_ref[0, rows, :] = _dot(h, wg_ref[...])

    acc = _dot(h_ref[...], w_ref[...])
    is_gate = n * tn + lax.broadcasted_iota(jnp.int32, (1, tn), 1) >= C_GATE
    o_ref[0] = jnp.where(is_gate, _sigmoid(acc), acc).astype(o_ref.dtype)


def _project(xc, g, mod3, w_main, w_g, T):
    B, Tt, D = xc.shape
    N = w_main.shape[1]
    tm = Tt // PROJ_ROW_SPLIT
    tn = _pick(N, (768, 512, 384, 256, 128))
    mrow = lambda row, chunk: pl.BlockSpec((1, 1, D), lambda b, i, n: (row(b), 0, chunk))
    own, ctx = (lambda b: b), (lambda b: B)
    return pl.pallas_call(
        functools.partial(_proj_kernel, T=T), grid=(B, PROJ_ROW_SPLIT, N // tn),
        in_specs=[pl.BlockSpec((1, tm, D), lambda b, i, n: (b, i, 0)),
                  pl.BlockSpec((1, D), lambda b, i, n: (0, 0)),
                  mrow(own, 0), mrow(own, 1), mrow(ctx, 0), mrow(ctx, 1),
                  pl.BlockSpec((D, tn), lambda b, i, n: (0, n)),
                  pl.BlockSpec((D, LANES), lambda b, i, n: (0, 0))],
        out_specs=[pl.BlockSpec((1, tm, tn), lambda b, i, n: (b, i, n)),
                   pl.BlockSpec((1, tm, LANES), lambda b, i, n: (b, i, 0))],
        out_shape=[jax.ShapeDtypeStruct((B, Tt, N), BF16), jax.ShapeDtypeStruct((B, Tt, LANES), F32)],
        scratch_shapes=[pltpu.VMEM((tm, D), BF16)],
        compiler_params=_params(("parallel", "parallel", "arbitrary"), 48),
    )(xc, g.reshape(1, D), mod3, mod3, mod3, mod3, w_main, w_g)


def _qk_prep_kernel(q_ref, k_ref, nw_ref, cos_ref, sin_ref, o_ref):
    lane = lax.broadcasted_iota(jnp.int32, (q_ref.shape[1], HEAD_DIM), 1)
    first = (lane % (HEAD_DIM // 2)) < HEAD_DIM // 4
    cos = cos_ref[...]
    sin = sin_ref[...]
    for h in range(o_ref.shape[2] // HEAD_DIM):
        src, hh = (q_ref, h) if h < GQA_HEADS else (k_ref, h - GQA_HEADS)
        y = _rms(src[0, :, hh * HEAD_DIM:(hh + 1) * HEAD_DIM].astype(F32)) * nw_ref[:, h * HEAD_DIM:(h + 1) * HEAD_DIM]
        partner = jnp.where(first, pltpu.roll(y, HEAD_DIM - HEAD_DIM // 4, 1), pltpu.roll(y, HEAD_DIM // 4, 1))
        o_ref[0, :, h * HEAD_DIM:(h + 1) * HEAD_DIM] = (y * cos + partner * sin).astype(BF16)


def _qk_prep(P, nw, cos_t, sin_t):
    B, Tt, _ = P.shape
    qw, kw = GQA_HEADS * HEAD_DIM, GQA_KV_HEADS * HEAD_DIM
    w = qw + kw
    tr = _pick(Tt, (544, TR))
    return pl.pallas_call(
        _qk_prep_kernel, grid=(B, Tt // tr),
        in_specs=[pl.BlockSpec((1, tr, qw), lambda b, i: (b, i, C_GQ // qw)),
                  pl.BlockSpec((1, tr, kw), lambda b, i: (b, i, C_GK // kw)),
                  pl.BlockSpec((1, w), lambda b, i: (0, 0)),
                  pl.BlockSpec((tr, HEAD_DIM), lambda b, i: (i, 0)),
                  pl.BlockSpec((tr, HEAD_DIM), lambda b, i: (i, 0))],
        out_specs=pl.BlockSpec((1, tr, w), lambda b, i: (b, i, 0)),
        out_shape=jax.ShapeDtypeStruct((B, Tt, w), BF16),
        compiler_params=_params(("parallel", "parallel")),
    )(P, P, nw.reshape(1, w), cos_t, sin_t)


def _flash_kernel(q_ref, k_ref, v_ref, o_ref, m_ref, acc_ref, *, R, Lc, ck, n_lat_tiles):
    tile = pl.program_id(2)

    m_ref[...] = jnp.full(m_ref.shape, NEG_INF, F32)
    acc_ref[...] = jnp.zeros(acc_ref.shape, F32)

    def step(k, v):
        vaug = jnp.concatenate([v, jnp.ones(v.shape, BF16)], axis=1)
        for r in range(R):
            rows = slice(r * TR, (r + 1) * TR)
            s = _dot_nt(q_ref[0, :, r * HEAD_DIM:(r + 1) * HEAD_DIM], k)
            slabs = [s[:, c * LANES:(c + 1) * LANES] for c in range(s.shape[1] // LANES)]
            m_old = m_ref[rows, :]
            m_new = jnp.maximum(m_old, jnp.max(functools.reduce(jnp.maximum, slabs), axis=1, keepdims=True))
            alpha = jnp.exp(m_old - m_new)
            p = jnp.concatenate([jnp.exp(sl - m_new) for sl in slabs], axis=1).astype(BF16)
            acc_ref[rows, :] = jnp.concatenate([alpha, alpha], axis=1) * acc_ref[rows, :] + _dot(p, vaug)
            m_ref[rows, :] = m_new

    kv_rows = k_ref.shape[1]

    @pl.when(tile >= n_lat_tiles)
    def _():
        step(k_ref[0, kv_rows - Lc:, :], v_ref[0, kv_rows - Lc:, :])

    @pl.when(tile < n_lat_tiles)
    def _():
        def body(j, carry):
            r0 = pl.multiple_of(j * ck, ck)
            step(k_ref[0, pl.ds(r0, ck), :], v_ref[0, pl.ds(r0, ck), :])
            return carry
        lax.fori_loop(0, kv_rows // ck, body, 0)

    o = acc_ref[:, :HEAD_DIM] / acc_ref[:, HEAD_DIM:]
    for r in range(R):
        o_ref[0, :, r * HEAD_DIM:(r + 1) * HEAD_DIM] = o[r * TR:(r + 1) * TR].astype(o_ref.dtype)


def _flash(q_arr, q_col, k_arr, k_col, v_arr, v_col, *, G, R, T, Lc):
    B, Tt, _ = q_arr.shape
    qw = R * HEAD_DIM
    ck = max(c for c in range(LANES, 2304 + 1, LANES) if Tt % c == 0)
    return pl.pallas_call(
        functools.partial(_flash_kernel, R=R, Lc=Lc, ck=ck, n_lat_tiles=T // TR),
        grid=(B, G, Tt // TR),
        in_specs=[pl.BlockSpec((1, TR, qw), lambda b, g, i: (b, i, q_col // qw + g)),
                  pl.BlockSpec((1, Tt, HEAD_DIM), lambda b, g, i: (b, 0, k_col // HEAD_DIM + g)),
                  pl.BlockSpec((1, Tt, HEAD_DIM), lambda b, g, i: (b, 0, v_col // HEAD_DIM + g))],
        out_specs=pl.BlockSpec((1, TR, qw), lambda b, g, i: (b, i, g)),
        out_shape=jax.ShapeDtypeStruct((B, Tt, G * qw), BF16),
        scratch_shapes=[pltpu.VMEM((R * TR, LANES), F32), pltpu.VMEM((R * TR, 2 * HEAD_DIM), F32)],
        compiler_params=_params(("parallel", "parallel", "arbitrary"), 48),
    )(q_arr, k_arr, v_arr)


def _nat_geometry(T):
    rows = T // GRID_W
    kh = min(NAT_WIN_ROWS, rows)
    nkr = min(kh + 1, rows)
    qbr = Q_BLOCK // GRID_W
    nbr = rows // qbr
    kb = np.zeros(nbr, np.int32)
    var = np.zeros(nbr, np.int32)
    variants = []
    for j in range(nbr):
        qrow = j * qbr + np.arange(qbr)
        start_r = np.clip(qrow - kh // 2, 0, rows - kh)
        kb[j] = min(start_r[0], rows - nkr)
        sig = (int(kb[j] - j * qbr), tuple(int(s - kb[j]) for s in start_r))
        if sig not in variants:
            variants.append(sig)
        var[j] = variants.index(sig)
    return kh, nkr, qbr, nbr, kb, var, variants


def _nat_bias_kernel(rb_ref, o_ref, *, variants, kh, nkr, qbr):
    h = pl.program_id(0)
    ndr = 2 * NAT_WIN_ROWS - 1
    ndc = 2 * NAT_WIN_COLS - 1
    qc = lax.broadcasted_iota(jnp.int32, (GRID_W, GRID_W), 0)
    kc = lax.broadcasted_iota(jnp.int32, (GRID_W, GRID_W), 1)
    dc = kc - qc + NAT_WIN_COLS - 1
    start_c = jnp.clip(qc - NAT_WIN_COLS // 2, 0, GRID_W - NAT_WIN_COLS)
    col_in = (kc >= start_c) & (kc < start_c + NAT_WIN_COLS)
    masked = jnp.full((GRID_W, GRID_W), NEG_INF, F32)
    toeplitz = []
    for dr in range(ndr):
        t = jnp.zeros((GRID_W, GRID_W), F32)
        for d in range(ndc):
            t = jnp.where(dc == d, rb_ref[(h * ndr + dr) * ndc + d], t)
        toeplitz.append(jnp.where(col_in, t, NEG_INF))
    for v, (delta, srel) in enumerate(variants):
        for qr in range(qbr):
            blocks = []
            for kr in range(nkr):
                if srel[qr] <= kr < srel[qr] + kh:
                    blocks.append(toeplitz[int(np.clip(kr + delta - qr + NAT_WIN_ROWS - 1, 0, ndr - 1))])
                else:
                    blocks.append(masked)
            rows = slice(qr * GRID_W, (qr + 1) * GRID_W)
            for kr in range(0, nkr - 1, 2):
                o_ref[v, 0, rows, kr * GRID_W:(kr + 2) * GRID_W] = jnp.concatenate(blocks[kr:kr + 2], axis=1)
            if nkr % 2:
                o_ref[v, 0, rows, (nkr - 1) * GRID_W:] = blocks[-1]


def _nat_bias_tables(rel_bias, T):
    kh, nkr, qbr, _, _, _, variants = _nat_geometry(T)
    V = len(variants)
    nk = nkr * GRID_W
    return pl.pallas_call(
        functools.partial(_nat_bias_kernel, variants=variants, kh=kh, nkr=nkr, qbr=qbr),
        grid=(NAT_HEADS,),
        in_specs=[pl.BlockSpec(memory_space=pltpu.SMEM)],
        out_specs=pl.BlockSpec((V, 1, Q_BLOCK, nk), lambda h: (0, h, 0, 0)),
        out_shape=jax.ShapeDtypeStruct((V, NAT_HEADS, Q_BLOCK, nk), F32),
        compiler_params=_params(("parallel",)),
    )(rel_bias.reshape(-1))


def _nat_kernel(kb_ref, var_ref, q_ref, k_ref, v_ref, bias_ref, o_ref, s1_ref, s2_ref, *, T, nbr, nk, scale):
    kc = k_ref[0, T:, :]
    vc = v_ref[0, T:, :]

    def scores(j, slot):
        q = q_ref[0, pl.ds(pl.multiple_of(j * Q_BLOCK, Q_BLOCK), Q_BLOCK), :]
        r0 = pl.multiple_of(kb_ref[j] * GRID_W, GRID_W)
        s1_ref[slot] = _dot_nt(q, k_ref[0, pl.ds(r0, nk), :]) * scale + bias_ref[var_ref[j], 0]
        s2_ref[slot] = _dot_nt(q, kc) * scale

    def finish(j, slot):
        s1 = s1_ref[slot]
        s2 = s2_ref[slot]
        r0 = pl.multiple_of(kb_ref[j] * GRID_W, GRID_W)
        m = jnp.maximum(jnp.max(s1, axis=1, keepdims=True), jnp.max(s2, axis=1, keepdims=True))
        p1 = jnp.exp(s1 - m)
        p2 = jnp.exp(s2 - m)
        l = jnp.sum(p1, axis=1, keepdims=True) + jnp.sum(p2, axis=1, keepdims=True)
        o = _dot(p1.astype(BF16), v_ref[0, pl.ds(r0, nk), :]) + _dot(p2.astype(BF16), vc)
        o_ref[0, pl.ds(pl.multiple_of(j * Q_BLOCK, Q_BLOCK), Q_BLOCK), :] = (o / l).astype(o_ref.dtype)

    scores(0, 0)

    def body(i, carry):
        j = 2 * i
        scores(j + 1, 1)
        finish(j, 0)
        scores(jnp.minimum(j + 2, nbr - 1), 0)
        finish(j + 1, 1)
        return carry

    lax.fori_loop(0, nbr // 2, body, 0, unroll=4)

    s = _dot_nt(q_ref[0, T:, :], kc) * scale
    p = jnp.exp(s - jnp.max(s, axis=1, keepdims=True))
    o_ref[0, T:, :] = (_dot(p.astype(BF16), vc) / jnp.sum(p, axis=1, keepdims=True)).astype(o_ref.dtype)


def _nat(P, bias_tab, T, Lc, scale):
    B, Tt, _ = P.shape
    _, nkr, _, nbr, kb, var, _ = _nat_geometry(T)
    nk = nkr * GRID_W
    V = bias_tab.shape[0]
    grid_spec = pltpu.PrefetchScalarGridSpec(
        num_scalar_prefetch=2, grid=(B, NAT_HEADS),
        in_specs=[pl.BlockSpec((1, Tt, HEAD_DIM), lambda b, h, kb_, vr: (b, 0, C_NQ // HEAD_DIM + h)),
                  pl.BlockSpec((1, Tt, HEAD_DIM), lambda b, h, kb_, vr: (b, 0, C_NK // HEAD_DIM + h)),
                  pl.BlockSpec((1, Tt, HEAD_DIM), lambda b, h, kb_, vr: (b, 0, C_NV // HEAD_DIM + h)),
                  pl.BlockSpec((V, 1, Q_BLOCK, nk), lambda b, h, kb_, vr: (0, h, 0, 0))],
        out_specs=pl.BlockSpec((1, Tt, HEAD_DIM), lambda b, h, kb_, vr: (b, 0, h)),
        scratch_shapes=[pltpu.VMEM((2, Q_BLOCK, nk), F32), pltpu.VMEM((2, Q_BLOCK, Lc), F32)])
    assert nbr % 2 == 0
    return pl.pallas_call(
        functools.partial(_nat_kernel, T=T, nbr=nbr, nk=nk, scale=scale),
        grid_spec=grid_spec,
        out_shape=jax.ShapeDtypeStruct((B, Tt, NAT_HEADS * HEAD_DIM), BF16),
        compiler_params=_params(("parallel", "parallel")),
    )(jnp.asarray(kb), jnp.asarray(var), P, P, P, bias_tab)


def _mlstm_kernel(qf_ref, kf_ref, vf_ref, gf_ref, ktf_ref, qb_ref, kb_ref, vb_ref, gb_ref, ktb_ref, bias_ref,
                  hf_ref, hb_ref, st_ref, m_ref, *, kscale):
    L = MLSTM_CHUNK
    dv = MLSTM_DV

    @pl.when(pl.program_id(1) == 0)
    def _():
        st_ref[...] = jnp.zeros(st_ref.shape, F32)
        m_ref[...] = jnp.full(m_ref.shape, NEG_INF, F32)

    ti = lax.broadcasted_iota(jnp.int32, (L, L), 0)
    si = lax.broadcasted_iota(jnp.int32, (L, L), 1)
    ones_col = (lax.broadcasted_iota(jnp.int32, (L, LANES), 1) == 0).astype(BF16)

    for dirn, (q_ref, k_ref, v_ref, g_ref, kt_ref, h_ref) in enumerate(
            ((qf_ref, kf_ref, vf_ref, gf_ref, ktf_ref, hf_ref), (qb_ref, kb_ref, vb_ref, gb_ref, ktb_ref, hb_ref))):
        tri = (si <= ti) if dirn == 0 else (si >= ti)
        G = g_ref[0] + bias_ref[...]
        LF = jnp.minimum(G, 0.0) - jnp.log(1.0 + jnp.exp(-jnp.abs(G)))
        lf_hi = LF.astype(BF16)
        lf_lo = (LF - lf_hi.astype(F32)).astype(BF16)
        trib = tri.astype(BF16)
        Bc = _dot(trib, lf_hi) + _dot(trib, lf_lo)
        BcT = Bc.T
        GT = G.T
        bl_row = Bc[L - 1:L, :] if dirn == 0 else Bc[0:1, :]
        for h in range(MLSTM_HEADS):
            ci = dirn * 2 * MLSTM_HEADS + h
            cf = ci + MLSTM_HEADS
            bcol = Bc[:, cf:cf + 1]
            brow = BcT[cf:cf + 1, :]
            licol = G[:, ci:ci + 1]
            lirow = GT[ci:ci + 1, :]
            bl = bl_row[:, cf:cf + 1]
            m_old = m_ref[dirn, h]
            gcol = bl - bcol + licol
            m_new = jnp.maximum(bl + m_old, jnp.max(gcol, axis=0, keepdims=True))
            decay = jnp.exp(bl + m_old - m_new)
            wk = jnp.exp(gcol - m_new)
            dmat = jnp.where(tri, bcol - brow + lirow, NEG_INF)
            inter = bcol + m_old
            m_t = jnp.maximum(inter, jnp.max(dmat, axis=1, keepdims=True))
            w = jnp.exp(dmat - m_t)
            a = jnp.exp(inter - m_t)
            qh = q_ref[0, :, h * MLSTM_DQK:(h + 1) * MLSTM_DQK]
            kh = k_ref[0, :, h * MLSTM_DQK:(h + 1) * MLSTM_DQK]
            vaug = jnp.concatenate([v_ref[0, :, h * dv:(h + 1) * dv], ones_col], axis=1)
            smat = _dot_nt(qh, kh) * kscale * w
            state = st_ref[dirn, h]
            nd = _dot(smat.astype(BF16), vaug) + a * _dot(qh, state.astype(BF16))
            den = jnp.maximum(jnp.abs(nd[:, dv:dv + 1]), jnp.exp(-m_t))
            h_ref[0, :, h * dv:(h + 1) * dv] = (nd[:, :dv] / den).astype(h_ref.dtype)
            kT = kt_ref[0, h * MLSTM_DQK:(h + 1) * MLSTM_DQK, :]
            upd = _dot(kT, (wk * vaug.astype(F32)).astype(BF16)) * kscale
            st_ref[dirn, h] = decay * state + upd
            m_ref[dirn, h] = m_new


def _mlstm(P, gates, gate_bias, T, Lc):
    B, Tt, _ = P.shape
    L = MLSTM_CHUNK
    kt = jnp.swapaxes(P[:, :, C_MK:C_MK + MLSTM_HEADS * MLSTM_DQK], 1, 2)
    ncl, ncc = T // L, Lc // L
    fwd = lambda i: jnp.where(i < ncc, ncl + i, i - ncc)
    bwd = lambda i: jnp.where(i < ncc, ncl + ncc - 1 - i, ncl - 1 - (i - ncc))
    qw = MLSTM_HEADS * MLSTM_DQK
    vw = MLSTM_HEADS * MLSTM_DV

    def specs(order):
        return [pl.BlockSpec((1, L, qw), lambda b, i: (b, order(i), C_MQ // qw)),
                pl.BlockSpec((1, L, qw), lambda b, i: (b, order(i), C_MK // qw)),
                pl.BlockSpec((1, L, vw), lambda b, i: (b, order(i), C_MV // vw)),
                pl.BlockSpec((1, L, LANES), lambda b, i: (b, order(i), 0)),
                pl.BlockSpec((1, qw, L), lambda b, i: (b, 0, order(i)))]

    out = jax.ShapeDtypeStruct((B, Tt, vw), BF16)
    return pl.pallas_call(
        functools.partial(_mlstm_kernel, kscale=MLSTM_DQK ** -0.5),
        grid=(B, ncl + ncc),
        in_specs=specs(fwd) + specs(bwd) + [pl.BlockSpec((1, LANES), lambda b, i: (0, 0))],
        out_specs=[pl.BlockSpec((1, L, vw), lambda b, i: (b, fwd(i), 0)),
                   pl.BlockSpec((1, L, vw), lambda b, i: (b, bwd(i), 0))],
        out_shape=[out, out],
        scratch_shapes=[pltpu.VMEM((2, MLSTM_HEADS, MLSTM_DQK, MLSTM_DV + LANES), F32),
                        pltpu.VMEM((2, MLSTM_HEADS, 1, 1), F32)],
        compiler_params=_params(("parallel", "arbitrary")),
    )(P, P, P, gates, kt, P, P, P, gates, kt, gate_bias)


def _merge_kernel(hf_ref, hb_ref, mo_ref, yg_ref, yn_ref, g0_ref, g1_ref, g2_ref, wb_ref, o_ref, ym_ref):
    @pl.when(pl.program_id(1) == 0)
    def _():
        h = hf_ref[...].astype(F32) + hb_ref[...].astype(F32)
        for hh in range(MLSTM_HEADS):
            cols = slice(hh * MLSTM_DV, (hh + 1) * MLSTM_DV)
            ym_ref[:, cols] = (_sigmoid(mo_ref[:, cols].astype(F32)) * _rms(h[:, cols])).astype(BF16)

    gate = lambda r: r[...].astype(F32)
    acc = gate(g0_ref) * _dot(ym_ref[...], wb_ref[0])
    acc += gate(g1_ref) * _dot(yg_ref[...], wb_ref[1])
    acc += gate(g2_ref) * _dot(yn_ref[...], wb_ref[2])
    o_ref[...] = acc.astype(BF16)


def _merge(hf, hb, P2, yg, yn, wb):
    M, W = hf.shape
    D = wb.shape[2]
    tm = _pick(M, (1024, 512, 256))
    tn = _pick(D, (512, 256))
    row = lambda c: pl.BlockSpec((tm, W), lambda i, n: (i, c))
    gate = lambda br: pl.BlockSpec((tm, tn), lambda i, n: (i, (C_GATE + br * D) // tn + n))
    return pl.pallas_call(
        _merge_kernel, grid=(M // tm, D // tn),
        in_specs=[row(0), row(0), row(C_MO // W), row(0), row(0), gate(0), gate(1), gate(2),
                  pl.BlockSpec((N_BRANCHES, W, tn), lambda i, n: (0, 0, n))],
        out_specs=pl.BlockSpec((tm, tn), lambda i, n: (i, n)),
        out_shape=jax.ShapeDtypeStruct((M, D), BF16),
        scratch_shapes=[pltpu.VMEM((tm, W), BF16)],
        compiler_params=_params(("parallel", "arbitrary"), 56),
    )(hf, hb, P2, yg, yn, P2, P2, P2, wb)


def _wout_post_kernel(a_ref, w_ref, x_ref, gp_ref, gate_ref, gf_ref, sh_ref, sc_ref, whi_ref, wlo_ref,
                      o_ref, h_ref, acol_ref, arow_ref):
    y = _dot(a_ref[0], w_ref[...])
    x = x_ref[0] + gate_ref[0] * (_rms(y) * gp_ref[...])
    o_ref[0] = x

    h = _rms(x) * gf_ref[...] * (1.0 + sc_ref[0]) + sh_ref[0]
    hb = h.astype(BF16)
    h_ref[0] = hb
    hlo = (h - hb.astype(F32)).astype(BF16)
    logits = _dot(hb, whi_ref[...]) + _dot(hlo, whi_ref[...]) + _dot(hb, wlo_ref[...])
    lane = lax.broadcasted_iota(jnp.int32, logits.shape, 1)
    valid = lane < N_EXPERTS
    logits = jnp.where(valid, logits, NEG_INF)
    ex = jnp.where(valid, jnp.exp(logits - jnp.max(logits, axis=1, keepdims=True)), 0.0)
    aff = ex / jnp.sum(ex, axis=1, keepdims=True)
    acol_ref[0] = aff
    arow_ref[0] = aff.T[:N_EXPERTS]


def _wout_post(merged, w_out, xc, g_post, mod3, gate_chunk, g_ffn, shift_chunk, scale_chunk, whi, wlo):
    B, Tt, D = xc.shape
    nT = Tt // TR
    tile = pl.BlockSpec((1, TR, D), lambda b, i: (b, i, 0))
    vec = pl.BlockSpec((1, D), lambda b, i: (0, 0))
    row = lambda b, i: jnp.where(i == nT - 1, B, b)
    mod = lambda chunk: pl.BlockSpec((1, 1, D), lambda b, i: (row(b, i), 0, chunk))
    wspec = pl.BlockSpec((D, LANES), lambda b, i: (0, 0))
    return pl.pallas_call(
        _wout_post_kernel, grid=(B, nT),
        in_specs=[tile, pl.BlockSpec((D, D), lambda b, i: (0, 0)), tile, vec, mod(gate_chunk),
                  vec, mod(shift_chunk), mod(scale_chunk), wspec, wspec],
        out_specs=[tile, tile, pl.BlockSpec((1, TR, LANES), lambda b, i: (b, i, 0)),
                   pl.BlockSpec((1, N_EXPERTS, TR), lambda b, i: (b, 0, i))],
        out_shape=[jax.ShapeDtypeStruct((B, Tt, D), F32), jax.ShapeDtypeStruct((B, Tt, D), BF16),
                   jax.ShapeDtypeStruct((B, Tt, LANES), F32),
                   jax.ShapeDtypeStruct((B, N_EXPERTS, Tt), F32)],
        compiler_params=_params(("parallel", "parallel"), 48),
    )(merged, w_out, xc, g_post.reshape(1, D), mod3, g_ffn.reshape(1, D), mod3, mod3, whi, wlo)


def _route_segment(arow, acol, k, chunk):
    E, T = arow.shape
    nchunk = T // chunk
    ind = lambda m: jnp.where(m, 1.0, 0.0)
    bits = pltpu.bitcast(arow, jnp.int32)
    thr = jnp.zeros((E, 1), jnp.int32)
    for bit in range(30, -1, -1):
        cand = thr | (1 << bit)
        cnt = jnp.sum(ind(bits >= cand), axis=1, keepdims=True)
        thr = jnp.where(cnt >= k, cand, thr)
    thr_val = pltpu.bitcast(thr, F32)
    gt = arow > thr_val
    eq = arow == thr_val
    need = k - jnp.sum(ind(gt), axis=1, keepdims=True)

    r_i = lax.broadcasted_iota(jnp.int32, (chunk, chunk), 0)
    c_i = lax.broadcasted_iota(jnp.int32, (chunk, chunk), 1)
    before = ind(r_i < c_i).astype(BF16)
    after = ind(c_i < r_i).astype(BF16)

    def prefix_rows(mask):
        carry = jnp.zeros((E, 1), F32)
        parts, carries = [], [carry]
        for c in range(nchunk):
            mc = ind(mask[:, c * chunk:(c + 1) * chunk]).astype(BF16)
            parts.append(_dot(mc, before) + carry)
            carry = carry + jnp.sum(mc.astype(F32), axis=1, keepdims=True)
            carries.append(carry)
        return jnp.concatenate(parts, axis=1), carries

    tie_rank, _ = prefix_rows(eq)
    sel = gt | (eq & (tie_rank < need))
    pos, carries = prefix_rows(sel)
    prow = jnp.where(sel, pos, -1.0)
    lane = lax.broadcasted_iota(jnp.int32, (E, LANES), 1)
    starts = jnp.zeros((E, LANES), F32)
    for c, cv in enumerate(carries):
        starts = jnp.where(lane == c, cv, starts)

    sub = lax.broadcasted_iota(jnp.int32, (E, LANES), 0)
    diag = sub == lane
    thr_row = jnp.sum(jnp.where(diag, thr_val, 0.0), axis=0, keepdims=True)
    need_row = jnp.sum(jnp.where(diag, need, 0.0), axis=0, keepdims=True)
    lane_ok = lax.broadcasted_iota(jnp.int32, (1, LANES), 1) < E
    gt_c = (acol > thr_row) & lane_ok
    eq_c = (acol == thr_row) & lane_ok

    def prefix_cols(mask):
        carry = jnp.zeros((1, LANES), F32)
        parts = []
        for c in range(nchunk):
            mc = ind(mask[c * chunk:(c + 1) * chunk]).astype(BF16)
            parts.append(_dot(after, mc) + carry)
            carry = carry + jnp.sum(mc.astype(F32), axis=0, keepdims=True)
        return jnp.concatenate(parts, axis=0)

    sel_c = gt_c | (eq_c & (prefix_cols(eq_c) < need_row))
    pcol = jnp.where(sel_c, prefix_cols(sel_c), -1.0)
    return prow, pcol, starts


def _route_kernel(arow_ref, acol_ref, prow_ref, pcol_ref, starts_ref, *, T, Lc, cap_l, cap_c):
    prow, pcol, starts = _route_segment(arow_ref[0, :, :T], acol_ref[0, :T], cap_l, TR)
    prow_ref[0, :, :T] = prow
    pcol_ref[0, :T] = pcol
    starts_ref[0] = starts.astype(jnp.int32)
    prow, pcol, _ = _route_segment(arow_ref[0, :, T:], acol_ref[0, T:], cap_c, TR)
    prow_ref[0, :, T:] = prow
    pcol_ref[0, T:] = pcol


def _route(arow, acol, T, Lc, cap_l, cap_c):
    B, E, Tt = arow.shape
    return pl.pallas_call(
        functools.partial(_route_kernel, T=T, Lc=Lc, cap_l=cap_l, cap_c=cap_c),
        grid=(B,),
        in_specs=[pl.BlockSpec((1, E, Tt), lambda b: (b, 0, 0)), pl.BlockSpec((1, Tt, LANES), lambda b: (b, 0, 0))],
        out_specs=[pl.BlockSpec((1, E, Tt), lambda b: (b, 0, 0)), pl.BlockSpec((1, Tt, LANES), lambda b: (b, 0, 0)),
                   pl.BlockSpec((1, E, LANES), lambda b: (b, 0, 0))],
        out_shape=[jax.ShapeDtypeStruct((B, E, Tt), F32), jax.ShapeDtypeStruct((B, Tt, LANES), F32),
                   jax.ShapeDtypeStruct((B, E, LANES), jnp.int32)],
        compiler_params=_params(("parallel",), 48),
    )(arow, acol)


def _gather_kernel(st_ref, p_ref, a_ref, h_ref, o_ref, g_ref, acc_ref, gacc_ref, *, nchunk, cap_l, cap_c, st):
    b = pl.program_id(0)
    e = pl.program_id(1)
    base = (b * N_EXPERTS + e) * (nchunk + 1)
    acc_ref[...] = jnp.zeros(acc_ref.shape, F32)
    gacc_ref[...] = jnp.zeros(gacc_ref.shape, F32)
    tile_row = lax.broadcasted_iota(jnp.int32, (st, TR), 0)

    def chunk(c, carry):
        lo = st_ref[base + c]
        hi = st_ref[base + c + 1]
        pos = p_ref[0, pl.ds(c, 1), :]
        aff = a_ref[0, pl.ds(c, 1), :]
        hc = h_ref[0, pl.ds(pl.multiple_of(c * TR, TR), TR), :]

        def tile(S, carry2):
            s0 = pl.multiple_of(S * st, st)
            hit = pos == (tile_row + s0).astype(F32)
            acc_ref[pl.ds(s0, st), :] += _dot(jnp.where(hit, 1.0, 0.0).astype(BF16), hc)
            gacc_ref[pl.ds(s0, st), :] += jnp.sum(jnp.where(hit, aff, 0.0), axis=1, keepdims=True)
            return carry2

        lax.fori_loop(lo // st, (hi + st - 1) // st, tile, 0)
        return carry

    lax.fori_loop(0, nchunk, chunk, 0)
    o_ref[0, :cap_l, :] = acc_ref[...].astype(BF16)
    g_ref[0, :cap_l, :] = gacc_ref[...]

    if cap_c:
        hit = p_ref[0, nchunk:nchunk + 1, :] == lax.broadcasted_iota(jnp.int32, (cap_c, TR), 0).astype(F32)
        o_ref[0, cap_l:, :] = _dot(jnp.where(hit, 1.0, 0.0).astype(BF16), h_ref[0, nchunk * TR:, :]).astype(BF16)
        g_ref[0, cap_l:, :] = jnp.broadcast_to(
            jnp.sum(jnp.where(hit, a_ref[0, nchunk:nchunk + 1, :], 0.0), axis=1, keepdims=True), (cap_c, LANES))


def _gather(starts, prow3, arow3, hbf, *, nchunk, cap_l, cap_c):
    B, Tt, D = hbf.shape
    E = N_EXPERTS
    nT = Tt // TR
    st = min(cap_l, 128)
    capb = cap_l + cap_c
    pspec = pl.BlockSpec((1, nT, TR), lambda b, e, s: (b * E + e, 0, 0))
    grid_spec = pltpu.PrefetchScalarGridSpec(
        num_scalar_prefetch=1, grid=(B, E),
        in_specs=[pspec, pspec, pl.BlockSpec((1, Tt, D), lambda b, e, s: (b, 0, 0))],
        out_specs=[pl.BlockSpec((1, capb, D), lambda b, e, s: (e, b, 0)),
                   pl.BlockSpec((1, capb, LANES), lambda b, e, s: (e, b, 0))],
        scratch_shapes=[pltpu.VMEM((cap_l, D), F32), pltpu.VMEM((cap_l, LANES), F32)])
    return pl.pallas_call(
        functools.partial(_gather_kernel, nchunk=nchunk, cap_l=cap_l, cap_c=cap_c, st=st),
        grid_spec=grid_spec,
        out_shape=[jax.ShapeDtypeStruct((E, B * capb, D), BF16), jax.ShapeDtypeStruct((E, B * capb, LANES), F32)],
        compiler_params=_params(("parallel", "arbitrary"), 56),
    )(starts, prow3, arow3, hbf)


def _ffn1_kernel(x_ref, wg_ref, wu_ref, o_ref, wgb_ref, wub_ref):
    @pl.when(pl.program_id(2) == 0)
    def _():
        wgb_ref[...] = wg_ref[0].astype(BF16)
        wub_ref[...] = wu_ref[0].astype(BF16)
    x = x_ref[0]
    g = _dot(x, wgb_ref[...])
    u = _dot(x, wub_ref[...])
    o_ref[0] = (g * _sigmoid(g) * u).astype(BF16)


def _ffn2_kernel(x_ref, w_ref, g_ref, o_ref, wb_ref):
    @pl.when(pl.program_id(2) == 0)
    def _():
        wb_ref[...] = w_ref[0].astype(BF16)
    o_ref[0] = (_dot(x_ref[0], wb_ref[...]) * g_ref[0, :, :1]).astype(BF16)


def _expert_matmul(kernel, x, ws, layer, row_scale=None):
    E, S, K = x.shape
    N = ws[0].shape[3]
    ws = [w.reshape(-1, K, N) for w in ws]
    tm = _pick(S, (1088, 1024, 544, 512, 272, 256, 160, 128))
    tn = _pick(N, (1024, 512, 256) if len(ws) == 1 else (512, 256))
    wspec = pl.BlockSpec((1, K, tn), lambda e, n, m: (layer * E + e, 0, n))
    in_specs = [pl.BlockSpec((1, tm, K), lambda e, n, m: (e, m, 0))] + [wspec] * len(ws)
    args = [x, *ws]
    if row_scale is not None:
        in_specs.append(pl.BlockSpec((1, tm, LANES), lambda e, n, m: (e, m, 0)))
        args.append(row_scale)
    return pl.pallas_call(
        kernel, grid=(E, N // tn, S // tm),
        in_specs=in_specs,
        out_specs=pl.BlockSpec((1, tm, tn), lambda e, n, m: (e, m, n)),
        out_shape=jax.ShapeDtypeStruct((E, S, N), BF16),
        scratch_shapes=[pltpu.VMEM((K, tn), BF16)] * len(ws),
        compiler_params=_params(("parallel", "parallel", "arbitrary"), 48),
    )(*args)


SLOT_ALIGN = 16
COMBINE_WINDOW = 64


def _combine_kernel(st_ref, pc_ref, ye_ref, x_ref, gp_ref, gate_ref, o_ref, win_ref, acc_ref, sem,
                    *, nchunk, cap, W, stride, off):
    E = N_EXPERTS
    b = pl.program_id(0)
    t = pl.program_id(1)
    nb, nt = pl.num_programs(0), pl.num_programs(1)
    step = b * nt + t
    buf = step % 2

    def bounds(bb, tt, e):
        base = (bb * E + e) * (nchunk + 1) + tt
        return (st_ref[base] // SLOT_ALIGN) * SLOT_ALIGN, st_ref[base + 1]

    def window(bb, tt, e, r):
        first, _ = bounds(bb, tt, e)
        want = first + r * W
        return want, jnp.minimum(want, cap - W)

    def copy(bb, tt, e, r, into):
        _, start = window(bb, tt, e, r)
        row0 = pl.multiple_of(bb * stride + off + start, SLOT_ALIGN)
        return pltpu.make_async_copy(ye_ref.at[e, pl.ds(row0, W), :],
                                     win_ref.at[into, pl.ds(e * W, W), :], sem.at[into, e])

    @pl.when(step == 0)
    def _():
        for e in range(E):
            copy(b, t, e, 0, buf).start()

    @pl.when(step + 1 < nb * nt)
    def _():
        last_t = t + 1 == nt
        b_next = jnp.where(last_t, b + 1, b)
        t_next = jnp.where(last_t, 0, t + 1)
        for e in range(E):
            copy(b_next, t_next, e, 0, 1 - buf).start()

    rounds = jnp.int32(1)
    for e in range(E):
        first, hi = bounds(b, t, e)
        rounds = jnp.maximum(rounds, (hi - first + W - 1) // W)

    pos1 = pc_ref[0] + 1.0
    pos_hi = jnp.floor(pos1 * (1.0 / SLOT_ALIGN))
    pos_lo = pos1 - pos_hi * SLOT_ALIGN
    col = lax.broadcasted_iota(jnp.int32, (LANES, E * W), 1)
    row = lax.broadcasted_iota(jnp.int32, (LANES, E * W), 0)
    expand = jnp.where(col // W == row, 1.0, 0.0).astype(BF16)
    pos_rep = SLOT_ALIGN * _dot(pos_hi.astype(BF16), expand) + _dot(pos_lo.astype(BF16), expand) - 1.0
    lane = lax.broadcasted_iota(jnp.int32, (1, E * W), 1)
    in_win = (lane % W).astype(F32)

    def onehot(r):
        want_row = jnp.zeros((1, E * W), F32)
        start_row = jnp.zeros((1, E * W), F32)
        for e in range(E):
            want, start = window(b, t, e, r)
            mine = lane // W == e
            want_row = jnp.where(mine, want.astype(F32), want_row)
            start_row = jnp.where(mine, start.astype(F32), start_row)
        hit = (pos_rep - start_row == in_win) & (pos_rep >= want_row)
        return jnp.where(hit, 1.0, 0.0).astype(BF16)

    lhs = onehot(0)
    for e in range(E):
        copy(b, t, e, 0, buf).wait()
    acc_ref[...] = _dot(lhs, win_ref[buf])

    def more(r, carry):
        for e in range(E):
            copy(b, t, e, r, buf).start()
        lhs_r = onehot(r)
        for e in range(E):
            copy(b, t, e, r, buf).wait()
        acc_ref[...] += _dot(lhs_r, win_ref[buf])
        return carry

    lax.fori_loop(1, rounds, more, 0)
    o_ref[0] = x_ref[0] + gate_ref[0] * (_rms(acc_ref[...]) * gp_ref[...])


def _combine(starts, pcol, ye, xc, g_post, mod3, gate_chunk, *, tile0, ntile, nchunk, cap, stride, off,
             mod_row_ctx, out):
    B, Tt, D = xc.shape
    E = N_EXPERTS
    W = min(cap, COMBINE_WINDOW)
    fresh = out is not None
    if not fresh:
        out = xc
    tok = lambda w: pl.BlockSpec((1, TR, w), lambda b, t, s: (b, tile0 + t, 0))
    mrow = (lambda b: B) if mod_row_ctx else (lambda b: b)
    in_specs = [tok(LANES), pl.BlockSpec(memory_space=pl.ANY), tok(D),
                pl.BlockSpec((1, D), lambda b, t, s: (0, 0)),
                pl.BlockSpec((1, 1, D), lambda b, t, s: (mrow(b), 0, gate_chunk))]
    args = [pcol, ye, xc, g_post.reshape(1, D), mod3]
    kern = functools.partial(_combine_kernel, nchunk=nchunk, cap=cap, W=W, stride=stride, off=off)
    if fresh:
        out_spec = pl.BlockSpec((1, TR, D), lambda b, t, s: (b, t, 0))
        aliases = {}
    else:
        out_spec = tok(D)
        aliases = {3: 0}
    grid_spec = pltpu.PrefetchScalarGridSpec(
        num_scalar_prefetch=1, grid=(B, ntile), in_specs=in_specs, out_specs=out_spec,
        scratch_shapes=[pltpu.VMEM((2, E * W, D), BF16), pltpu.VMEM((TR, D), F32),
                        pltpu.SemaphoreType.DMA((2, E))])
    return pl.pallas_call(
        kern, grid_spec=grid_spec,
        out_shape=jax.ShapeDtypeStruct(out.shape, F32),
        input_output_aliases=aliases,
        compiler_params=_params(("arbitrary", "arbitrary"), 48),
    )(starts, *args)


def _rope_tables(T, Lc):
    nf = HEAD_DIM // 4
    t = np.arange(T)
    inv = ROPE_THETA ** (-jnp.arange(nf, dtype=F32) / nf)
    row = jnp.asarray(t // GRID_W, F32)
    col = jnp.asarray(t % GRID_W, F32)
    ang = jnp.stack([row[:, None] * inv, col[:, None] * inv], axis=1)
    cos = jnp.broadcast_to(jnp.cos(ang)[:, :, None, :], (T, 2, 2, nf)).reshape(T, HEAD_DIM)
    sin = jnp.broadcast_to(jnp.sin(ang)[:, :, None, :], (T, 2, 2, nf)).reshape(T, HEAD_DIM)
    sign = jnp.asarray(np.tile(np.repeat([-1.0, 1.0], nf), 2), F32)
    cos = jnp.concatenate([cos, jnp.ones((Lc, HEAD_DIM), F32)], axis=0)
    sin = jnp.concatenate([sin * sign, jnp.zeros((Lc, HEAD_DIM), F32)], axis=0)
    return cos, sin


def _split_bf16(w):
    hi = w.astype(BF16)
    return hi, (w - hi.astype(F32)).astype(BF16)


def kernel(x, c, ctx, c_ctx, w_mod, b_mod, g_pre_mix, g_post_mix, g_pre_ffn, g_post_ffn, w_in, mlstm_gate_bias, gqa_q_norm, gqa_k_norm, nat_rel_bias, w_branch, w_out, w_router, w_expert_gate, w_expert_up, w_expert_down):
    B, T, D = x.shape
    Lc = ctx.shape[1]
    depth = w_mod.shape[0]
    E = N_EXPERTS
    assert Lc == TR and T % (2 * TR) == 0 and w_router.shape[2] == E
    Tt = T + Lc
    nT = Tt // TR
    M = B * Tt
    cap_l = EC_CAPACITY_FACTOR * T // E
    cap_c = EC_CAPACITY_FACTOR * Lc // E
    scale = HEAD_DIM ** -0.5
    nchunk = T // TR

    xc = jnp.concatenate([x, ctx], axis=1)
    rows_c = -(-(B + 1) // 8) * 8
    cc = jnp.zeros((rows_c, D), F32).at[:B].set(c).at[B].set(c_ctx)
    mod = _modulation(cc, w_mod, b_mod)
    cos_t, sin_t = _rope_tables(T, Lc)

    for l in range(depth):
        last = l == depth - 1
        mod3 = mod[l].reshape(rows_c, 1, N_MOD * D)
        w_l = w_in[l]
        w_main = jnp.concatenate([w_l[:, :W_IN_GATE_COL], w_l[:, W_IN_GATE_COL + N_GATE_COLS:]], axis=1).astype(BF16)
        w_g = jnp.pad(w_l[:, W_IN_GATE_COL:W_IN_GATE_COL + N_GATE_COLS], ((0, 0), (0, LANES - N_GATE_COLS))).astype(BF16)
        gate_bias = jnp.pad(mlstm_gate_bias[l], (0, LANES - N_GATE_COLS)).reshape(1, LANES)
        nw = jnp.concatenate([jnp.tile(gqa_q_norm[l][None] * scale, (GQA_HEADS, 1)),
                              jnp.tile(gqa_k_norm[l][None], (GQA_KV_HEADS, 1))], axis=0)[:, None, :]

        P, gates = _project(xc, g_pre_mix[l], mod3, w_main, w_g, T)
        P2 = P.reshape(M, -1)

        hf, hb = _mlstm(P, gates, gate_bias, T, Lc)

        qk = _qk_prep(P, nw, cos_t, sin_t)
        yg = _flash(qk, 0, qk, GQA_HEADS * HEAD_DIM, P, C_GV, G=GQA_KV_HEADS, R=GQA_HEADS // GQA_KV_HEADS,
                    T=T, Lc=Lc)

        bias_tab = _nat_bias_tables(nat_rel_bias[l], T)
        yn = _nat(P, bias_tab, T, Lc, scale)

        merged = _merge(hf.reshape(M, -1), hb.reshape(M, -1), P2, yg.reshape(M, -1), yn.reshape(M, -1),
                        w_branch[l].astype(BF16))
        wr = jnp.pad(w_router[l], ((0, 0), (0, LANES - E)))
        xc, hx, acol, arow = _wout_post(merged.reshape(B, Tt, D), w_out[l].astype(BF16), xc, g_post_mix[l], mod3, 2,
                                        g_pre_ffn[l], 3, 4, *_split_bf16(wr))

        prow, pcol, starts = _route(arow, acol, T, Lc, cap_l, cap_c)
        starts_flat = starts[:, :, :nchunk + 1].reshape(-1)
        prow3 = prow.reshape(B * E, nT, TR)
        arow3 = arow.reshape(B * E, nT, TR)

        cap_ctx = 0 if last else cap_c
        capb = cap_l + cap_ctx
        xs, gs = _gather(starts_flat, prow3, arow3, hx, nchunk=nchunk, cap_l=cap_l, cap_c=cap_ctx)
        hid = _expert_matmul(_ffn1_kernel, xs, (w_expert_gate, w_expert_up), l)
        ye = _expert_matmul(_ffn2_kernel, hid, (w_expert_down,), l, row_scale=gs)

        if last:
            return _combine(starts_flat, pcol, ye, xc, g_post_ffn[l], mod3, 5, tile0=0, ntile=nchunk,
                            nchunk=nchunk, cap=cap_l, stride=capb, off=0, mod_row_ctx=False,
                            out=jax.ShapeDtypeStruct((B, T, D), F32))
        xc = _combine(starts_flat, pcol, ye, xc, g_post_ffn[l], mod3, 5, tile0=0, ntile=nchunk,
                      nchunk=nchunk, cap=cap_l, stride=capb, off=0, mod_row_ctx=False, out=None)
        ctx_starts = jnp.tile(jnp.asarray([0, cap_c], jnp.int32), B * E)
        xc = _combine(ctx_starts, pcol, ye, xc, g_post_ffn[l], mod3, 5, tile0=nT - 1, ntile=1,
                      nchunk=1, cap=cap_c, stride=capb, off=cap_l, mod_row_ctx=True, out=None)
```

```python
import functools

import numpy as np
import jax
import jax.numpy as jnp
from jax import lax
from jax.experimental import pallas as pl
from jax.experimental.pallas import tpu as pltpu

F32 = jnp.float32
BF16 = jnp.bfloat16

LANES = 128
HEAD_DIM = 128
MLSTM_HEADS = 4
MLSTM_DQK = 128
MLSTM_DV = 256
MLSTM_CHUNK = 256
GQA_HEADS = 8
GQA_KV_HEADS = 2
NAT_HEADS = 8
NAT_WIN_ROWS = 8
NAT_WIN_COLS = 16
GRID_W = 64
Q_BLOCK = 128
N_EXPERTS = 16
EC_CAPACITY_FACTOR = 2
ROPE_THETA = 10000.0
NORM_EPS = 1e-6
NEG_INF = -1e30
N_MOD = 6
N_BRANCHES = 3
BRANCH_WIDTH = 1024
TR = 256
N_GATE_COLS = 4 * MLSTM_HEADS

C_MQ = 0
C_MK = C_MQ + MLSTM_HEADS * MLSTM_DQK
C_MV = C_MK + MLSTM_HEADS * MLSTM_DQK
C_MO = C_MV + MLSTM_HEADS * MLSTM_DV
C_GQ = C_MO + MLSTM_HEADS * MLSTM_DV
C_GK = C_GQ + GQA_HEADS * HEAD_DIM
C_GV = C_GK + GQA_KV_HEADS * HEAD_DIM
C_NQ = C_GV + GQA_KV_HEADS * HEAD_DIM
C_NK = C_NQ + NAT_HEADS * HEAD_DIM
C_NV = C_NK + NAT_HEADS * HEAD_DIM
C_GATE = C_NV + NAT_HEADS * HEAD_DIM
W_IN_GATE_COL = C_GQ


def _params(sem, vmem_mb=None):
    return pltpu.CompilerParams(
        dimension_semantics=sem,
        vmem_limit_bytes=None if vmem_mb is None else vmem_mb * 2 ** 20)


def _pick(n, cands):
    for c in cands:
        if n % c == 0:
            return c
    raise ValueError(f"no tile for {n} in {cands}")


def _dot(a, b):
    return jnp.dot(a, b, preferred_element_type=F32)


def _dot_nt(a, b):
    return lax.dot_general(a, b, (((1,), (1,)), ((), ())), preferred_element_type=F32)


def _sigmoid(x):
    return 0.5 * jnp.tanh(0.5 * x) + 0.5


def _rms(x):
    return x * lax.rsqrt(jnp.mean(x * x, axis=-1, keepdims=True) + NORM_EPS)


def _mod_kernel(c_ref, w_ref, b_ref, o_ref):
    c = c_ref[...]
    a = (c * _sigmoid(c)).astype(BF16)
    o_ref[0] = _dot(a, w_ref[0].astype(BF16)) + b_ref[0]


def _modulation(cc, w_mod, b_mod):
    L, D, N = w_mod.shape
    tn = _pick(N, (768, 512, 256, 128))
    return pl.pallas_call(
        _mod_kernel,
        grid=(L, N // tn),
        in_specs=[pl.BlockSpec((cc.shape[0], D), lambda l, n: (0, 0)),
                  pl.BlockSpec((1, D, tn), lambda l, n: (l, 0, n)),
                  pl.BlockSpec((1, 1, tn), lambda l, n: (l, 0, n))],
        out_specs=pl.BlockSpec((1, cc.shape[0], tn), lambda l, n: (l, 0, n)),
        out_shape=jax.ShapeDtypeStruct((L, cc.shape[0], N), F32),
        compiler_params=_params(("parallel", "parallel")),
    )(cc, w_mod, b_mod.reshape(L, 1, N))


PROJ_ROW_SPLIT = 4


def _proj_kernel(x_ref, g_ref, shx_ref, scx_ref, shc_ref, scc_ref, w_ref, wg_ref, o_ref, og_ref, h_ref, *, T):
    tm = h_ref.shape[0]
    tn = o_ref.shape[2]
    sub = tm // PROJ_ROW_SPLIT
    n = pl.program_id(2)

    @pl.when(n == 0)
    def _():
        for k in range(PROJ_ROW_SPLIT):
            rows = slice(k * sub, (k + 1) * sub)
            y = _rms(x_ref[0, rows, :]) * g_ref[...]
            tok = pl.program_id(1) * tm + k * sub + lax.broadcasted_iota(jnp.int32, (sub, 1), 0)
            is_ctx = tok >= T
            sc = jnp.where(is_ctx, scc_ref[0], scx_ref[0])
            sh = jnp.where(is_ctx, shc_ref[0], shx_ref[0])
            h = (y * (1.0 + sc) + sh).astype(BF16)
            h_ref[rows, :] = h
            og_ref[0, rows, :] = _dot(h, wg_ref[...])

    acc = _dot(h_ref[...], w_ref[...])
    is_gate = n * tn + lax.broadcasted_iota(jnp.int32, (1, tn), 1) >= C_GATE
    o_ref[0] = jnp.where(is_gate, _sigmoid(acc), acc).astype(o_ref.dtype)


def _project(xc, g, mod3, w_main, w_g, T):
    B, Tt, D = xc.shape
    N = w_main.shape[1]
    tm = Tt // PROJ_ROW_SPLIT
    tn = _pick(N, (768, 512, 384, 256, 128))
    mrow = lambda row, chunk: pl.BlockSpec((1, 1, D), lambda b, i, n: (row(b), 0, chunk))
    own, ctx = (lambda b: b), (lambda b: B)
    return pl.pallas_call(
        functools.partial(_proj_kernel, T=T), grid=(B, PROJ_ROW_SPLIT, N // tn),
        in_specs=[pl.BlockSpec((1, tm, D), lambda b, i, n: (b, i, 0)),
                  pl.BlockSpec((1, D), lambda b, i, n: (0, 0)),
                  mrow(own, 0), mrow(own, 1), mrow(ctx, 0), mrow(ctx, 1),
                  pl.BlockSpec((D, tn), lambda b, i, n: (0, n)),
                  pl.BlockSpec((D, LANES), lambda b, i, n: (0, 0))],
        out_specs=[pl.BlockSpec((1, tm, tn), lambda b, i, n: (b, i, n)),
                   pl.BlockSpec((1, tm, LANES), lambda b, i, n: (b, i, 0))],
        out_shape=[jax.ShapeDtypeStruct((B, Tt, N), BF16), jax.ShapeDtypeStruct((B, Tt, LANES), F32)],
        scratch_shapes=[pltpu.VMEM((tm, D), BF16)],
        compiler_params=_params(("parallel", "parallel", "arbitrary"), 48),
    )(xc, g.reshape(1, D), mod3, mod3, mod3, mod3, w_main, w_g)


def _qk_prep_kernel(q_ref, k_ref, nw_ref, cos_ref, sin_ref, o_ref):
    lane = lax.broadcasted_iota(jnp.int32, (q_ref.shape[1], HEAD_DIM), 1)
    first = (lane % (HEAD_DIM // 2)) < HEAD_DIM // 4
    cos = cos_ref[...]
    sin = sin_ref[...]
    for h in range(o_ref.shape[2] // HEAD_DIM):
        src, hh = (q_ref, h) if h < GQA_HEADS else (k_ref, h - GQA_HEADS)
        y = _rms(src[0, :, hh * HEAD_DIM:(hh + 1) * HEAD_DIM].astype(F32)) * nw_ref[:, h * HEAD_DIM:(h + 1) * HEAD_DIM]
        partner = jnp.where(first, pltpu.roll(y, HEAD_DIM - HEAD_DIM // 4, 1), pltpu.roll(y, HEAD_DIM // 4, 1))
        o_ref[0, :, h * HEAD_DIM:(h + 1) * HEAD_DIM] = (y * cos + partner * sin).astype(BF16)


def _qk_prep(P, nw, cos_t, sin_t):
    B, Tt, _ = P.shape
    qw, kw = GQA_HEADS * HEAD_DIM, GQA_KV_HEADS * HEAD_DIM
    w = qw + kw
    tr = _pick(Tt, (544, TR))
    return pl.pallas_call(
        _qk_prep_kernel, grid=(B, Tt // tr),
        in_specs=[pl.BlockSpec((1, tr, qw), lambda b, i: (b, i, C_GQ // qw)),
                  pl.BlockSpec((1, tr, kw), lambda b, i: (b, i, C_GK // kw)),
                  pl.BlockSpec((1, w), lambda b, i: (0, 0)),
                  pl.BlockSpec((tr, HEAD_DIM), lambda b, i: (i, 0)),
                  pl.BlockSpec((tr, HEAD_DIM), lambda b, i: (i, 0))],
        out_specs=pl.BlockSpec((1, tr, w), lambda b, i: (b, i, 0)),
        out_shape=jax.ShapeDtypeStruct((B, Tt, w), BF16),
        compiler_params=_params(("parallel", "parallel")),
    )(P, P, nw.reshape(1, w), cos_t, sin_t)


def _flash_kernel(q_ref, k_ref, v_ref, o_ref, m_ref, acc_ref, *, R, Lc, ck, n_lat_tiles):
    tile = pl.program_id(2)

    m_ref[...] = jnp.full(m_ref.shape, NEG_INF, F32)
    acc_ref[...] = jnp.zeros(acc_ref.shape, F32)

    def step(k, v):
        vaug = jnp.concatenate([v, jnp.ones(v.shape, BF16)], axis=1)
        for r in range(R):
            rows = slice(r * TR, (r + 1) * TR)
            s = _dot_nt(q_ref[0, :, r * HEAD_DIM:(r + 1) * HEAD_DIM], k)
            slabs = [s[:, c * LANES:(c + 1) * LANES] for c in range(s.shape[1] // LANES)]
            m_old = m_ref[rows, :]
            m_new = jnp.maximum(m_old, jnp.max(functools.reduce(jnp.maximum, slabs), axis=1, keepdims=True))
            alpha = jnp.exp(m_old - m_new)
            p = jnp.concatenate([jnp.exp(sl - m_new) for sl in slabs], axis=1).astype(BF16)
            acc_ref[rows, :] = jnp.concatenate([alpha, alpha], axis=1) * acc_ref[rows, :] + _dot(p, vaug)
            m_ref[rows, :] = m_new

    kv_rows = k_ref.shape[1]

    @pl.when(tile >= n_lat_tiles)
    def _():
        step(k_ref[0, kv_rows - Lc:, :], v_ref[0, kv_rows - Lc:, :])

    @pl.when(tile < n_lat_tiles)
    def _():
        def body(j, carry):
            r0 = pl.multiple_of(j * ck, ck)
            step(k_ref[0, pl.ds(r0, ck), :], v_ref[0, pl.ds(r0, ck), :])
            return carry
        lax.fori_loop(0, kv_rows // ck, body, 0)

    o = acc_ref[:, :HEAD_DIM] / acc_ref[:, HEAD_DIM:]
    for r in range(R):
        o_ref[0, :, r * HEAD_DIM:(r + 1) * HEAD_DIM] = o[r * TR:(r + 1) * TR].astype(o_ref.dtype)


def _flash(q_arr, q_col, k_arr, k_col, v_arr, v_col, *, G, R, T, Lc):
    B, Tt, _ = q_arr.shape
    qw = R * HEAD_DIM
    ck = max(c for c in range(LANES, 2304 + 1, LANES) if Tt % c == 0)
    return pl.pallas_call(
        functools.partial(_flash_kernel, R=R, Lc=Lc, ck=ck, n_lat_tiles=T // TR),
        grid=(B, G, Tt // TR),
        in_specs=[pl.BlockSpec((1, TR, qw), lambda b, g, i: (b, i, q_col // qw + g)),
                  pl.BlockSpec((1, Tt, HEAD_DIM), lambda b, g, i: (b, 0, k_col // HEAD_DIM + g)),
                  pl.BlockSpec((1, Tt, HEAD_DIM), lambda b, g, i: (b, 0, v_col // HEAD_DIM + g))],
        out_specs=pl.BlockSpec((1, TR, qw), lambda b, g, i: (b, i, g)),
        out_shape=jax.ShapeDtypeStruct((B, Tt, G * qw), BF16),
        scratch_shapes=[pltpu.VMEM((R * TR, LANES), F32), pltpu.VMEM((R * TR, 2 * HEAD_DIM), F32)],
        compiler_params=_params(("parallel", "parallel", "arbitrary"), 48),
    )(q_arr, k_arr, v_arr)


def _nat_geometry(T):
    rows = T // GRID_W
    kh = min(NAT_WIN_ROWS, rows)
    nkr = min(kh + 1, rows)
    qbr = Q_BLOCK // GRID_W
    nbr = rows // qbr
    kb = np.zeros(nbr, np.int32)
    var = np.zeros(nbr, np.int32)
    variants = []
    for j in range(nbr):
        qrow = j * qbr + np.arange(qbr)
        start_r = np.clip(qrow - kh // 2, 0, rows - kh)
        kb[j] = min(start_r[0], rows - nkr)
        sig = (int(kb[j] - j * qbr), tuple(int(s - kb[j]) for s in start_r))
        if sig not in variants:
            variants.append(sig)
        var[j] = variants.index(sig)
    return kh, nkr, qbr, nbr, kb, var, variants


def _nat_bias_kernel(rb_ref, o_ref, *, variants, kh, nkr, qbr):
    h = pl.program_id(0)
    ndr = 2 * NAT_WIN_ROWS - 1
    ndc = 2 * NAT_WIN_COLS - 1
    qc = lax.broadcasted_iota(jnp.int32, (GRID_W, GRID_W), 0)
    kc = lax.broadcasted_iota(jnp.int32, (GRID_W, GRID_W), 1)
    dc = kc - qc + NAT_WIN_COLS - 1
    start_c = jnp.clip(qc - NAT_WIN_COLS // 2, 0, GRID_W - NAT_WIN_COLS)
    col_in = (kc >= start_c) & (kc < start_c + NAT_WIN_COLS)
    masked = jnp.full((GRID_W, GRID_W), NEG_INF, F32)
    toeplitz = []
    for dr in range(ndr):
        t = jnp.zeros((GRID_W, GRID_W), F32)
        for d in range(ndc):
            t = jnp.where(dc == d, rb_ref[(h * ndr + dr) * ndc + d], t)
        toeplitz.append(jnp.where(col_in, t, NEG_INF))
    for v, (delta, srel) in enumerate(variants):
        for qr in range(qbr):
            blocks = []
            for kr in range(nkr):
                if srel[qr] <= kr < srel[qr] + kh:
                    blocks.append(toeplitz[int(np.clip(kr + delta - qr + NAT_WIN_ROWS - 1, 0, ndr - 1))])
                else:
                    blocks.append(masked)
            rows = slice(qr * GRID_W, (qr + 1) * GRID_W)
            for kr in range(0, nkr - 1, 2):
                o_ref[v, 0, rows, kr * GRID_W:(kr + 2) * GRID_W] = jnp.concatenate(blocks[kr:kr + 2], axis=1)
            if nkr % 2:
                o_ref[v, 0, rows, (nkr - 1) * GRID_W:] = blocks[-1]


def _nat_bias_tables(rel_bias, T):
    kh, nkr, qbr, _, _, _, variants = _nat_geometry(T)
    V = len(variants)
    nk = nkr * GRID_W
    return pl.pallas_call(
        functools.partial(_nat_bias_kernel, variants=variants, kh=kh, nkr=nkr, qbr=qbr),
        grid=(NAT_HEADS,),
        in_specs=[pl.BlockSpec(memory_space=pltpu.SMEM)],
        out_specs=pl.BlockSpec((V, 1, Q_BLOCK, nk), lambda h: (0, h, 0, 0)),
        out_shape=jax.ShapeDtypeStruct((V, NAT_HEADS, Q_BLOCK, nk), F32),
        compiler_params=_params(("parallel",)),
    )(rel_bias.reshape(-1))


def _nat_kernel(kb_ref, var_ref, q_ref, k_ref, v_ref, bias_ref, o_ref, s1_ref, s2_ref, *, T, nbr, nk, scale):
    kc = k_ref[0, T:, :]
    vc = v_ref[0, T:, :]

    def scores(j, slot):
        q = q_ref[0, pl.ds(pl.multiple_of(j * Q_BLOCK, Q_BLOCK), Q_BLOCK), :]
        r0 = pl.multiple_of(kb_ref[j] * GRID_W, GRID_W)
        s1_ref[slot] = _dot_nt(q, k_ref[0, pl.ds(r0, nk), :]) * scale + bias_ref[var_ref[j], 0]
        s2_ref[slot] = _dot_nt(q, kc) * scale

    def finish(j, slot):
        s1 = s1_ref[slot]
        s2 = s2_ref[slot]
        r0 = pl.multiple_of(kb_ref[j] * GRID_W, GRID_W)
        m = jnp.maximum(jnp.max(s1, axis=1, keepdims=True), jnp.max(s2, axis=1, keepdims=True))
        p1 = jnp.exp(s1 - m)
        p2 = jnp.exp(s2 - m)
        l = jnp.sum(p1, axis=1, keepdims=True) + jnp.sum(p2, axis=1, keepdims=True)
        o = _dot(p1.astype(BF16), v_ref[0, pl.ds(r0, nk), :]) + _dot(p2.astype(BF16), vc)
        o_ref[0, pl.ds(pl.multiple_of(j * Q_BLOCK, Q_BLOCK), Q_BLOCK), :] = (o / l).astype(o_ref.dtype)

    scores(0, 0)

    def body(i, carry):
        j = 2 * i
        scores(j + 1, 1)
        finish(j, 0)
        scores(jnp.minimum(j + 2, nbr - 1), 0)
        finish(j + 1, 1)
        return carry

    lax.fori_loop(0, nbr // 2, body, 0, unroll=4)

    s = _dot_nt(q_ref[0, T:, :], kc) * scale
    p = jnp.exp(s - jnp.max(s, axis=1, keepdims=True))
    o_ref[0, T:, :] = (_dot(p.astype(BF16), vc) / jnp.sum(p, axis=1, keepdims=True)).astype(o_ref.dtype)


def _nat(P, bias_tab, T, Lc, scale):
    B, Tt, _ = P.shape
    _, nkr, _, nbr, kb, var, _ = _nat_geometry(T)
    nk = nkr * GRID_W
    V = bias_tab.shape[0]
    grid_spec = pltpu.PrefetchScalarGridSpec(
        num_scalar_prefetch=2, grid=(B, NAT_HEADS),
        in_specs=[pl.BlockSpec((1, Tt, HEAD_DIM), lambda b, h, kb_, vr: (b, 0, C_NQ // HEAD_DIM + h)),
                  pl.BlockSpec((1, Tt, HEAD_DIM), lambda b, h, kb_, vr: (b, 0, C_NK // HEAD_DIM + h)),
                  pl.BlockSpec((1, Tt, HEAD_DIM), lambda b, h, kb_, vr: (b, 0, C_NV // HEAD_DIM + h)),
                  pl.BlockSpec((V, 1, Q_BLOCK, nk), lambda b, h, kb_, vr: (0, h, 0, 0))],
        out_specs=pl.BlockSpec((1, Tt, HEAD_DIM), lambda b, h, kb_, vr: (b, 0, h)),
        scratch_shapes=[pltpu.VMEM((2, Q_BLOCK, nk), F32), pltpu.VMEM((2, Q_BLOCK, Lc), F32)])
    assert nbr % 2 == 0
    return pl.pallas_call(
        functools.partial(_nat_kernel, T=T, nbr=nbr, nk=nk, scale=scale),
        grid_spec=grid_spec,
        out_shape=jax.ShapeDtypeStruct((B, Tt, NAT_HEADS * HEAD_DIM), BF16),
        compiler_params=_params(("parallel", "parallel")),
    )(jnp.asarray(kb), jnp.asarray(var), P, P, P, bias_tab)


def _mlstm_kernel(qf_ref, kf_ref, vf_ref, gf_ref, ktf_ref, qb_ref, kb_ref, vb_ref, gb_ref, ktb_ref, bias_ref,
                  hf_ref, hb_ref, st_ref, m_ref, *, kscale):
    L = MLSTM_CHUNK
    dv = MLSTM_DV

    @pl.when(pl.program_id(1) == 0)
    def _():
        st_ref[...] = jnp.zeros(st_ref.shape, F32)
        m_ref[...] = jnp.full(m_ref.shape, NEG_INF, F32)

    ti = lax.broadcasted_iota(jnp.int32, (L, L), 0)
    si = lax.broadcasted_iota(jnp.int32, (L, L), 1)
    ones_col = (lax.broadcasted_iota(jnp.int32, (L, LANES), 1) == 0).astype(BF16)

    for dirn, (q_ref, k_ref, v_ref, g_ref, kt_ref, h_ref) in enumerate(
            ((qf_ref, kf_ref, vf_ref, gf_ref, ktf_ref, hf_ref), (qb_ref, kb_ref, vb_ref, gb_ref, ktb_ref, hb_ref))):
        tri = (si <= ti) if dirn == 0 else (si >= ti)
        G = g_ref[0] + bias_ref[...]
        LF = jnp.minimum(G, 0.0) - jnp.log(1.0 + jnp.exp(-jnp.abs(G)))
        lf_hi = LF.astype(BF16)
        lf_lo = (LF - lf_hi.astype(F32)).astype(BF16)
        trib = tri.astype(BF16)
        Bc = _dot(trib, lf_hi) + _dot(trib, lf_lo)
        BcT = Bc.T
        GT = G.T
        bl_row = Bc[L - 1:L, :] if dirn == 0 else Bc[0:1, :]
        for h in range(MLSTM_HEADS):
            ci = dirn * 2 * MLSTM_HEADS + h
            cf = ci + MLSTM_HEADS
            bcol = Bc[:, cf:cf + 1]
            brow = BcT[cf:cf + 1, :]
            licol = G[:, ci:ci + 1]
            lirow = GT[ci:ci + 1, :]
            bl = bl_row[:, cf:cf + 1]
            m_old = m_ref[dirn, h]
            gcol = bl - bcol + licol
            m_new = jnp.maximum(bl + m_old, jnp.max(gcol, axis=0, keepdims=True))
            decay = jnp.exp(bl + m_old - m_new)
            wk = jnp.exp(gcol - m_new)
            dmat = jnp.where(tri, bcol - brow + lirow, NEG_INF)
            inter = bcol + m_old
            m_t = jnp.maximum(inter, jnp.max(dmat, axis=1, keepdims=True))
            w = jnp.exp(dmat - m_t)
            a = jnp.exp(inter - m_t)
            qh = q_ref[0, :, h * MLSTM_DQK:(h + 1) * MLSTM_DQK]
            kh = k_ref[0, :, h * MLSTM_DQK:(h + 1) * MLSTM_DQK]
            vaug = jnp.concatenate([v_ref[0, :, h * dv:(h + 1) * dv], ones_col], axis=1)
            smat = _dot_nt(qh, kh) * kscale * w
            state = st_ref[dirn, h]
            nd = _dot(smat.astype(BF16), vaug) + a * _dot(qh, state.astype(BF16))
            den = jnp.maximum(jnp.abs(nd[:, dv:dv + 1]), jnp.exp(-m_t))
            h_ref[0, :, h * dv:(h + 1) * dv] = (nd[:, :dv] / den).astype(h_ref.dtype)
            kT = kt_ref[0, h * MLSTM_DQK:(h + 1) * MLSTM_DQK, :]
            upd = _dot(kT, (wk * vaug.astype(F32)).astype(BF16)) * kscale
            st_ref[dirn, h] = decay * state + upd
            m_ref[dirn, h] = m_new


def _mlstm(P, gates, gate_bias, T, Lc):
    B, Tt, _ = P.shape
    L = MLSTM_CHUNK
    kt = jnp.swapaxes(P[:, :, C_MK:C_MK + MLSTM_HEADS * MLSTM_DQK], 1, 2)
    ncl, ncc = T // L, Lc // L
    fwd = lambda i: jnp.where(i < ncc, ncl + i, i - ncc)
    bwd = lambda i: jnp.where(i < ncc, ncl + ncc - 1 - i, ncl - 1 - (i - ncc))
    qw = MLSTM_HEADS * MLSTM_DQK
    vw = MLSTM_HEADS * MLSTM_DV

    def specs(order):
        return [pl.BlockSpec((1, L, qw), lambda b, i: (b, order(i), C_MQ // qw)),
                pl.BlockSpec((1, L, qw), lambda b, i: (b, order(i), C_MK // qw)),
                pl.BlockSpec((1, L, vw), lambda b, i: (b, order(i), C_MV // vw)),
                pl.BlockSpec((1, L, LANES), lambda b, i: (b, order(i), 0)),
                pl.BlockSpec((1, qw, L), lambda b, i: (b, 0, order(i)))]

    out = jax.ShapeDtypeStruct((B, Tt, vw), BF16)
    return pl.pallas_call(
        functools.partial(_mlstm_kernel, kscale=MLSTM_DQK ** -0.5),
        grid=(B, ncl + ncc),
        in_specs=specs(fwd) + specs(bwd) + [pl.BlockSpec((1, LANES), lambda b, i: (0, 0))],
        out_specs=[pl.BlockSpec((1, L, vw), lambda b, i: (b, fwd(i), 0)),
                   pl.BlockSpec((1, L, vw), lambda b, i: (b, bwd(i), 0))],
        out_shape=[out, out],
        scratch_shapes=[pltpu.VMEM((2, MLSTM_HEADS, MLSTM_DQK, MLSTM_DV + LANES), F32),
                        pltpu.VMEM((2, MLSTM_HEADS, 1, 1), F32)],
        compiler_params=_params(("parallel", "arbitrary")),
    )(P, P, P, gates, kt, P, P, P, gates, kt, gate_bias)


def _merge_kernel(hf_ref, hb_ref, mo_ref, yg_ref, yn_ref, g0_ref, g1_ref, g2_ref, wb_ref, o_ref, ym_ref):
    @pl.when(pl.program_id(1) == 0)
    def _():
        h = hf_ref[...].astype(F32) + hb_ref[...].astype(F32)
        for hh in range(MLSTM_HEADS):
            cols = slice(hh * MLSTM_DV, (hh + 1) * MLSTM_DV)
            ym_ref[:, cols] = (_sigmoid(mo_ref[:, cols].astype(F32)) * _rms(h[:, cols])).astype(BF16)

    gate = lambda r: r[...].astype(F32)
    acc = gate(g0_ref) * _dot(ym_ref[...], wb_ref[0])
    acc += gate(g1_ref) * _dot(yg_ref[...], wb_ref[1])
    acc += gate(g2_ref) * _dot(yn_ref[...], wb_ref[2])
    o_ref[...] = acc.astype(BF16)


def _merge(hf, hb, P2, yg, yn, wb):
    M, W = hf.shape
    D = wb.shape[2]
    tm = _pick(M, (1024, 512, 256))
    tn = _pick(D, (512, 256))
    row = lambda c: pl.BlockSpec((tm, W), lambda i, n: (i, c))
    gate = lambda br: pl.BlockSpec((tm, tn), lambda i, n: (i, (C_GATE + br * D) // tn + n))
    return pl.pallas_call(
        _merge_kernel, grid=(M // tm, D // tn),
        in_specs=[row(0), row(0), row(C_MO // W), row(0), row(0), gate(0), gate(1), gate(2),
                  pl.BlockSpec((N_BRANCHES, W, tn), lambda i, n: (0, 0, n))],
        out_specs=pl.BlockSpec((tm, tn), lambda i, n: (i, n)),
        out_shape=jax.ShapeDtypeStruct((M, D), BF16),
        scratch_shapes=[pltpu.VMEM((tm, W), BF16)],
        compiler_params=_params(("parallel", "arbitrary"), 56),
    )(hf, hb, P2, yg, yn, P2, P2, P2, wb)


def _wout_post_kernel(a_ref, w_ref, x_ref, gp_ref, gate_ref, gf_ref, sh_ref, sc_ref, whi_ref, wlo_ref,
                      o_ref, h_ref, acol_ref, arow_ref):
    y = _dot(a_ref[0], w_ref[...])
    x = x_ref[0] + gate_ref[0] * (_rms(y) * gp_ref[...])
    o_ref[0] = x

    h = _rms(x) * gf_ref[...] * (1.0 + sc_ref[0]) + sh_ref[0]
    hb = h.astype(BF16)
    h_ref[0] = hb
    hlo = (h - hb.astype(F32)).astype(BF16)
    logits = _dot(hb, whi_ref[...]) + _dot(hlo, whi_ref[...]) + _dot(hb, wlo_ref[...])
    lane = lax.broadcasted_iota(jnp.int32, logits.shape, 1)
    valid = lane < N_EXPERTS
    logits = jnp.where(valid, logits, NEG_INF)
    ex = jnp.where(valid, jnp.exp(logits - jnp.max(logits, axis=1, keepdims=True)), 0.0)
    aff = ex / jnp.sum(ex, axis=1, keepdims=True)
    acol_ref[0] = aff
    arow_ref[0] = aff.T[:N_EXPERTS]


def _wout_post(merged, w_out, xc, g_post, mod3, gate_chunk, g_ffn, shift_chunk, scale_chunk, whi, wlo):
    B, Tt, D = xc.shape
    nT = Tt // TR
    tile = pl.BlockSpec((1, TR, D), lambda b, i: (b, i, 0))
    vec = pl.BlockSpec((1, D), lambda b, i: (0, 0))
    row = lambda b, i: jnp.where(i == nT - 1, B, b)
    mod = lambda chunk: pl.BlockSpec((1, 1, D), lambda b, i: (row(b, i), 0, chunk))
    wspec = pl.BlockSpec((D, LANES), lambda b, i: (0, 0))
    return pl.pallas_call(
        _wout_post_kernel, grid=(B, nT),
        in_specs=[tile, pl.BlockSpec((D, D), lambda b, i: (0, 0)), tile, vec, mod(gate_chunk),
                  vec, mod(shift_chunk), mod(scale_chunk), wspec, wspec],
        out_specs=[tile, tile, pl.BlockSpec((1, TR, LANES), lambda b, i: (b, i, 0)),
                   pl.BlockSpec((1, N_EXPERTS, TR), lambda b, i: (b, 0, i))],
        out_shape=[jax.ShapeDtypeStruct((B, Tt, D), F32), jax.ShapeDtypeStruct((B, Tt, D), BF16),
                   jax.ShapeDtypeStruct((B, Tt, LANES), F32),
                   jax.ShapeDtypeStruct((B, N_EXPERTS, Tt), F32)],
        compiler_params=_params(("parallel", "parallel"), 48),
    )(merged, w_out, xc, g_post.reshape(1, D), mod3, g_ffn.reshape(1, D), mod3, mod3, whi, wlo)


def _route_segment(arow, acol, k, chunk):
    E, T = arow.shape
    nchunk = T // chunk
    ind = lambda m: jnp.where(m, 1.0, 0.0)
    bits = pltpu.bitcast(arow, jnp.int32)
    thr = jnp.zeros((E, 1), jnp.int32)
    for bit in range(30, -1, -1):
        cand = thr | (1 << bit)
        cnt = jnp.sum(ind(bits >= cand), axis=1, keepdims=True)
        thr = jnp.where(cnt >= k, cand, thr)
    thr_val = pltpu.bitcast(thr, F32)
    gt = arow > thr_val
    eq = arow == thr_val
    need = k - jnp.sum(ind(gt), axis=1, keepdims=True)

    r_i = lax.broadcasted_iota(jnp.int32, (chunk, chunk), 0)
    c_i = lax.broadcasted_iota(jnp.int32, (chunk, chunk), 1)
    before = ind(r_i < c_i).astype(BF16)
    after = ind(c_i < r_i).astype(BF16)

    def prefix_rows(mask):
        carry = jnp.zeros((E, 1), F32)
        parts, carries = [], [carry]
        for c in range(nchunk):
            mc = ind(mask[:, c * chunk:(c + 1) * chunk]).astype(BF16)
            parts.append(_dot(mc, before) + carry)
            carry = carry + jnp.sum(mc.astype(F32), axis=1, keepdims=True)
            carries.append(carry)
        return jnp.concatenate(parts, axis=1), carries

    tie_rank, _ = prefix_rows(eq)
    sel = gt | (eq & (tie_rank < need))
    pos, carries = prefix_rows(sel)
    prow = jnp.where(sel, pos, -1.0)
    lane = lax.broadcasted_iota(jnp.int32, (E, LANES), 1)
    starts = jnp.zeros((E, LANES), F32)
    for c, cv in enumerate(carries):
        starts = jnp.where(lane == c, cv, starts)

    sub = lax.broadcasted_iota(jnp.int32, (E, LANES), 0)
    diag = sub == lane
    thr_row = jnp.sum(jnp.where(diag, thr_val, 0.0), axis=0, keepdims=True)
    need_row = jnp.sum(jnp.where(diag, need, 0.0), axis=0, keepdims=True)
    lane_ok = lax.broadcasted_iota(jnp.int32, (1, LANES), 1) < E
    gt_c = (acol > thr_row) & lane_ok
    eq_c = (acol == thr_row) & lane_ok

    def prefix_cols(mask):
        carry = jnp.zeros((1, LANES), F32)
        parts = []
        for c in range(nchunk):
            mc = ind(mask[c * chunk:(c + 1) * chunk]).astype(BF16)
            parts.append(_dot(after, mc) + carry)
            carry = carry + jnp.sum(mc.astype(F32), axis=0, keepdims=True)
        return jnp.concatenate(parts, axis=0)

    sel_c = gt_c | (eq_c & (prefix_cols(eq_c) < need_row))
    pcol = jnp.where(sel_c, prefix_cols(sel_c), -1.0)
    return prow, pcol, starts


def _route_kernel(arow_ref, acol_ref, prow_ref, pcol_ref, starts_ref, *, T, Lc, cap_l, cap_c):
    prow, pcol, starts = _route_segment(arow_ref[0, :, :T], acol_ref[0, :T], cap_l, TR)
    prow_ref[0, :, :T] = prow
    pcol_ref[0, :T] = pcol
    starts_ref[0] = starts.astype(jnp.int32)
    prow, pcol, _ = _route_segment(arow_ref[0, :, T:], acol_ref[0, T:], cap_c, TR)
    prow_ref[0, :, T:] = prow
    pcol_ref[0, T:] = pcol


def _route(arow, acol, T, Lc, cap_l, cap_c):
    B, E, Tt = arow.shape
    return pl.pallas_call(
        functools.partial(_route_kernel, T=T, Lc=Lc, cap_l=cap_l, cap_c=cap_c),
        grid=(B,),
        in_specs=[pl.BlockSpec((1, E, Tt), lambda b: (b, 0, 0)), pl.BlockSpec((1, Tt, LANES), lambda b: (b, 0, 0))],
        out_specs=[pl.BlockSpec((1, E, Tt), lambda b: (b, 0, 0)), pl.BlockSpec((1, Tt, LANES), lambda b: (b, 0, 0)),
                   pl.BlockSpec((1, E, LANES), lambda b: (b, 0, 0))],
        out_shape=[jax.ShapeDtypeStruct((B, E, Tt), F32), jax.ShapeDtypeStruct((B, Tt, LANES), F32),
                   jax.ShapeDtypeStruct((B, E, LANES), jnp.int32)],
        compiler_params=_params(("parallel",), 48),
    )(arow, acol)


def _gather_kernel(st_ref, p_ref, a_ref, h_ref, o_ref, g_ref, acc_ref, gacc_ref, *, nchunk, cap_l, cap_c, st):
    b = pl.program_id(0)
    e = pl.program_id(1)
    base = (b * N_EXPERTS + e) * (nchunk + 1)
    for S in range(cap_l // st):
        acc_ref[...] = jnp.zeros(acc_ref.shape, F32)
        gacc_ref[...] = jnp.zeros(gacc_ref.shape, F32)
        slot = (lax.broadcasted_iota(jnp.int32, (st, TR), 0) + S * st).astype(F32)

        def body(c, carry):
            lo = st_ref[base + c]
            hi = st_ref[base + c + 1]

            @pl.when((lo < (S + 1) * st) & (hi > S * st))
            def _():
                hit = p_ref[0, pl.ds(c, 1), :] == slot
                r0 = pl.multiple_of(c * TR, TR)
                acc_ref[...] += _dot(jnp.where(hit, 1.0, 0.0).astype(BF16), h_ref[0, pl.ds(r0, TR), :])
                gacc_ref[...] += jnp.sum(jnp.where(hit, a_ref[0, pl.ds(c, 1), :], 0.0), axis=1, keepdims=True)
            return carry

        lax.fori_loop(0, nchunk, body, 0)
        o_ref[0, S * st:(S + 1) * st, :] = acc_ref[...].astype(BF16)
        g_ref[0, S * st:(S + 1) * st, :] = gacc_ref[...]

    if cap_c:
        hit = p_ref[0, nchunk:nchunk + 1, :] == lax.broadcasted_iota(jnp.int32, (cap_c, TR), 0).astype(F32)
        o_ref[0, cap_l:, :] = _dot(jnp.where(hit, 1.0, 0.0).astype(BF16), h_ref[0, nchunk * TR:, :]).astype(BF16)
        g_ref[0, cap_l:, :] = jnp.broadcast_to(
            jnp.sum(jnp.where(hit, a_ref[0, nchunk:nchunk + 1, :], 0.0), axis=1, keepdims=True), (cap_c, LANES))


def _gather(starts, prow3, arow3, hbf, *, nchunk, cap_l, cap_c):
    B, Tt, D = hbf.shape
    E = N_EXPERTS
    nT = Tt // TR
    st = min(cap_l, 128)
    capb = cap_l + cap_c
    pspec = pl.BlockSpec((1, nT, TR), lambda b, e, s: (b * E + e, 0, 0))
    grid_spec = pltpu.PrefetchScalarGridSpec(
        num_scalar_prefetch=1, grid=(B, E),
        in_specs=[pspec, pspec, pl.BlockSpec((1, Tt, D), lambda b, e, s: (b, 0, 0))],
        out_specs=[pl.BlockSpec((1, capb, D), lambda b, e, s: (e, b, 0)),
                   pl.BlockSpec((1, capb, LANES), lambda b, e, s: (e, b, 0))],
        scratch_shapes=[pltpu.VMEM((st, D), F32), pltpu.VMEM((st, LANES), F32)])
    return pl.pallas_call(
        functools.partial(_gather_kernel, nchunk=nchunk, cap_l=cap_l, cap_c=cap_c, st=st),
        grid_spec=grid_spec,
        out_shape=[jax.ShapeDtypeStruct((E, B * capb, D), BF16), jax.ShapeDtypeStruct((E, B * capb, LANES), F32)],
        compiler_params=_params(("parallel", "arbitrary"), 56),
    )(starts, prow3, arow3, hbf)


def _ffn1_kernel(x_ref, wg_ref, wu_ref, o_ref, wgb_ref, wub_ref):
    @pl.when(pl.program_id(2) == 0)
    def _():
        wgb_ref[...] = wg_ref[0].astype(BF16)
        wub_ref[...] = wu_ref[0].astype(BF16)
    x = x_ref[0]
    g = _dot(x, wgb_ref[...])
    u = _dot(x, wub_ref[...])
    o_ref[0] = (g * _sigmoid(g) * u).astype(BF16)


def _ffn2_kernel(x_ref, w_ref, g_ref, o_ref, wb_ref):
    @pl.when(pl.program_id(2) == 0)
    def _():
        wb_ref[...] = w_ref[0].astype(BF16)
    o_ref[0] = (_dot(x_ref[0], wb_ref[...]) * g_ref[0, :, :1]).astype(BF16)


def _expert_matmul(kernel, x, ws, layer, row_scale=None):
    E, S, K = x.shape
    N = ws[0].shape[3]
    ws = [w.reshape(-1, K, N) for w in ws]
    tm = _pick(S, (1088, 1024, 544, 512, 272, 256, 160, 128))
    tn = _pick(N, (1024, 512, 256) if len(ws) == 1 else (512, 256))
    wspec = pl.BlockSpec((1, K, tn), lambda e, n, m: (layer * E + e, 0, n))
    in_specs = [pl.BlockSpec((1, tm, K), lambda e, n, m: (e, m, 0))] + [wspec] * len(ws)
    args = [x, *ws]
    if row_scale is not None:
        in_specs.append(pl.BlockSpec((1, tm, LANES), lambda e, n, m: (e, m, 0)))
        args.append(row_scale)
    return pl.pallas_call(
        kernel, grid=(E, N // tn, S // tm),
        in_specs=in_specs,
        out_specs=pl.BlockSpec((1, tm, tn), lambda e, n, m: (e, m, n)),
        out_shape=jax.ShapeDtypeStruct((E, S, N), BF16),
        scratch_shapes=[pltpu.VMEM((K, tn), BF16)] * len(ws),
        compiler_params=_params(("parallel", "parallel", "arbitrary"), 48),
    )(*args)


SLOT_ALIGN = 16
COMBINE_WINDOW = 64


def _combine_kernel(st_ref, pc_ref, ye_ref, x_ref, gp_ref, gate_ref, o_ref, win_ref, acc_ref, sem,
                    *, nchunk, cap, W, stride, off):
    E = N_EXPERTS
    b = pl.program_id(0)
    t = pl.program_id(1)
    nb, nt = pl.num_programs(0), pl.num_programs(1)
    step = b * nt + t
    buf = step % 2

    def bounds(bb, tt, e):
        base = (bb * E + e) * (nchunk + 1) + tt
        return (st_ref[base] // SLOT_ALIGN) * SLOT_ALIGN, st_ref[base + 1]

    def window(bb, tt, e, r):
        first, _ = bounds(bb, tt, e)
        want = first + r * W
        return want, jnp.minimum(want, cap - W)

    def copy(bb, tt, e, r, into):
        _, start = window(bb, tt, e, r)
        row0 = pl.multiple_of(bb * stride + off + start, SLOT_ALIGN)
        return pltpu.make_async_copy(ye_ref.at[e, pl.ds(row0, W), :],
                                     win_ref.at[into, pl.ds(e * W, W), :], sem.at[into, e])

    @pl.when(step == 0)
    def _():
        for e in range(E):
            copy(b, t, e, 0, buf).start()

    @pl.when(step + 1 < nb * nt)
    def _():
        last_t = t + 1 == nt
        b_next = jnp.where(last_t, b + 1, b)
        t_next = jnp.where(last_t, 0, t + 1)
        for e in range(E):
            copy(b_next, t_next, e, 0, 1 - buf).start()

    rounds = jnp.int32(1)
    for e in range(E):
        first, hi = bounds(b, t, e)
        rounds = jnp.maximum(rounds, (hi - first + W - 1) // W)

    pos1 = pc_ref[0] + 1.0
    pos_hi = jnp.floor(pos1 * (1.0 / SLOT_ALIGN))
    pos_lo = pos1 - pos_hi * SLOT_ALIGN
    col = lax.broadcasted_iota(jnp.int32, (LANES, E * W), 1)
    row = lax.broadcasted_iota(jnp.int32, (LANES, E * W), 0)
    expand = jnp.where(col // W == row, 1.0, 0.0).astype(BF16)
    pos_rep = SLOT_ALIGN * _dot(pos_hi.astype(BF16), expand) + _dot(pos_lo.astype(BF16), expand) - 1.0
    lane = lax.broadcasted_iota(jnp.int32, (1, E * W), 1)
    in_win = (lane % W).astype(F32)

    def onehot(r):
        want_row = jnp.zeros((1, E * W), F32)
        start_row = jnp.zeros((1, E * W), F32)
        for e in range(E):
            want, start = window(b, t, e, r)
            mine = lane // W == e
            want_row = jnp.where(mine, want.astype(F32), want_row)
            start_row = jnp.where(mine, start.astype(F32), start_row)
        hit = (pos_rep - start_row == in_win) & (pos_rep >= want_row)
        return jnp.where(hit, 1.0, 0.0).astype(BF16)

    lhs = onehot(0)
    for e in range(E):
        copy(b, t, e, 0, buf).wait()
    acc_ref[...] = _dot(lhs, win_ref[buf])

    def more(r, carry):
        for e in range(E):
            copy(b, t, e, r, buf).start()
        lhs_r = onehot(r)
        for e in range(E):
            copy(b, t, e, r, buf).wait()
        acc_ref[...] += _dot(lhs_r, win_ref[buf])
        return carry

    lax.fori_loop(1, rounds, more, 0)
    o_ref[0] = x_ref[0] + gate_ref[0] * (_rms(acc_ref[...]) * gp_ref[...])


def _combine(starts, pcol, ye, xc, g_post, mod3, gate_chunk, *, tile0, ntile, nchunk, cap, stride, off,
             mod_row_ctx, out):
    B, Tt, D = xc.shape
    E = N_EXPERTS
    W = min(cap, COMBINE_WINDOW)
    fresh = out is not None
    if not fresh:
        out = xc
    tok = lambda w: pl.BlockSpec((1, TR, w), lambda b, t, s: (b, tile0 + t, 0))
    mrow = (lambda b: B) if mod_row_ctx else (lambda b: b)
    in_specs = [tok(LANES), pl.BlockSpec(memory_space=pl.ANY), tok(D),
                pl.BlockSpec((1, D), lambda b, t, s: (0, 0)),
                pl.BlockSpec((1, 1, D), lambda b, t, s: (mrow(b), 0, gate_chunk))]
    args = [pcol, ye, xc, g_post.reshape(1, D), mod3]
    kern = functools.partial(_combine_kernel, nchunk=nchunk, cap=cap, W=W, stride=stride, off=off)
    if fresh:
        out_spec = pl.BlockSpec((1, TR, D), lambda b, t, s: (b, t, 0))
        aliases = {}
    else:
        out_spec = tok(D)
        aliases = {3: 0}
    grid_spec = pltpu.PrefetchScalarGridSpec(
        num_scalar_prefetch=1, grid=(B, ntile), in_specs=in_specs, out_specs=out_spec,
        scratch_shapes=[pltpu.VMEM((2, E * W, D), BF16), pltpu.VMEM((TR, D), F32),
                        pltpu.SemaphoreType.DMA((2, E))])
    return pl.pallas_call(
        kern, grid_spec=grid_spec,
        out_shape=jax.ShapeDtypeStruct(out.shape, F32),
        input_output_aliases=aliases,
        compiler_params=_params(("arbitrary", "arbitrary"), 48),
    )(starts, *args)


def _rope_tables(T, Lc):
    nf = HEAD_DIM // 4
    t = np.arange(T)
    inv = ROPE_THETA ** (-jnp.arange(nf, dtype=F32) / nf)
    row = jnp.asarray(t // GRID_W, F32)
    col = jnp.asarray(t % GRID_W, F32)
    ang = jnp.stack([row[:, None] * inv, col[:, None] * inv], axis=1)
    cos = jnp.broadcast_to(jnp.cos(ang)[:, :, None, :], (T, 2, 2, nf)).reshape(T, HEAD_DIM)
    sin = jnp.broadcast_to(jnp.sin(ang)[:, :, None, :], (T, 2, 2, nf)).reshape(T, HEAD_DIM)
    sign = jnp.asarray(np.tile(np.repeat([-1.0, 1.0], nf), 2), F32)
    cos = jnp.concatenate([cos, jnp.ones((Lc, HEAD_DIM), F32)], axis=0)
    sin = jnp.concatenate([sin * sign, jnp.zeros((Lc, HEAD_DIM), F32)], axis=0)
    return cos, sin


def _split_bf16(w):
    hi = w.astype(BF16)
    return hi, (w - hi.astype(F32)).astype(BF16)


def kernel(x, c, ctx, c_ctx, w_mod, b_mod, g_pre_mix, g_post_mix, g_pre_ffn, g_post_ffn, w_in, mlstm_gate_bias, gqa_q_norm, gqa_k_norm, nat_rel_bias, w_branch, w_out, w_router, w_expert_gate, w_expert_up, w_expert_down):
    B, T, D = x.shape
    Lc = ctx.shape[1]
    depth = w_mod.shape[0]
    E = N_EXPERTS
    assert Lc == TR and T % (2 * TR) == 0 and w_router.shape[2] == E
    Tt = T + Lc
    nT = Tt // TR
    M = B * Tt
    cap_l = EC_CAPACITY_FACTOR * T // E
    cap_c = EC_CAPACITY_FACTOR * Lc // E
    scale = HEAD_DIM ** -0.5
    nchunk = T // TR

    xc = jnp.concatenate([x, ctx], axis=1)
    rows_c = -(-(B + 1) // 8) * 8
    cc = jnp.zeros((rows_c, D), F32).at[:B].set(c).at[B].set(c_ctx)
    mod = _modulation(cc, w_mod, b_mod)
    cos_t, sin_t = _rope_tables(T, Lc)

    for l in range(depth):
        last = l == depth - 1
        mod3 = mod[l].reshape(rows_c, 1, N_MOD * D)
        w_l = w_in[l]
        w_main = jnp.concatenate([w_l[:, :W_IN_GATE_COL], w_l[:, W_IN_GATE_COL + N_GATE_COLS:]], axis=1).astype(BF16)
        w_g = jnp.pad(w_l[:, W_IN_GATE_COL:W_IN_GATE_COL + N_GATE_COLS], ((0, 0), (0, LANES - N_GATE_COLS))).astype(BF16)
        gate_bias = jnp.pad(mlstm_gate_bias[l], (0, LANES - N_GATE_COLS)).reshape(1, LANES)
        nw = jnp.concatenate([jnp.tile(gqa_q_norm[l][None] * scale, (GQA_HEADS, 1)),
                              jnp.tile(gqa_k_norm[l][None], (GQA_KV_HEADS, 1))], axis=0)[:, None, :]

        P, gates = _project(xc, g_pre_mix[l], mod3, w_main, w_g, T)
        P2 = P.reshape(M, -1)

        hf, hb = _mlstm(P, gates, gate_bias, T, Lc)

        qk = _qk_prep(P, nw, cos_t, sin_t)
        yg = _flash(qk, 0, qk, GQA_HEADS * HEAD_DIM, P, C_GV, G=GQA_KV_HEADS, R=GQA_HEADS // GQA_KV_HEADS,
                    T=T, Lc=Lc)

        bias_tab = _nat_bias_tables(nat_rel_bias[l], T)
        yn = _nat(P, bias_tab, T, Lc, scale)

        merged = _merge(hf.reshape(M, -1), hb.reshape(M, -1), P2, yg.reshape(M, -1), yn.reshape(M, -1),
                        w_branch[l].astype(BF16))
        wr = jnp.pad(w_router[l], ((0, 0), (0, LANES - E)))
        xc, hx, acol, arow = _wout_post(merged.reshape(B, Tt, D), w_out[l].astype(BF16), xc, g_post_mix[l], mod3, 2,
                                        g_pre_ffn[l], 3, 4, *_split_bf16(wr))

        prow, pcol, starts = _route(arow, acol, T, Lc, cap_l, cap_c)
        starts_flat = starts[:, :, :nchunk + 1].reshape(-1)
        prow3 = prow.reshape(B * E, nT, TR)
        arow3 = arow.reshape(B * E, nT, TR)

        cap_ctx = 0 if last else cap_c
        capb = cap_l + cap_ctx
        xs, gs = _gather(starts_flat, prow3, arow3, hx, nchunk=nchunk, cap_l=cap_l, cap_c=cap_ctx)
        hid = _expert_matmul(_ffn1_kernel, xs, (w_expert_gate, w_expert_up), l)
        ye = _expert_matmul(_ffn2_kernel, hid, (w_expert_down,), l, row_scale=gs)

        if last:
            return _combine(starts_flat, pcol, ye, xc, g_post_ffn[l], mod3, 5, tile0=0, ntile=nchunk,
                            nchunk=nchunk, cap=cap_l, stride=capb, off=0, mod_row_ctx=False,
                            out=jax.ShapeDtypeStruct((B, T, D), F32))
        xc = _combine(starts_flat, pcol, ye, xc, g_post_ffn[l], mod3, 5, tile0=0, ntile=nchunk,
                      nchunk=nchunk, cap=cap_l, stride=capb, off=0, mod_row_ctx=False, out=None)
        ctx_starts = jnp.tile(jnp.asarray([0, cap_c], jnp.int32), B * E)
        xc = _combine(ctx_starts, pcol, ye, xc, g_post_ffn[l], mod3, 5, tile0=nT - 1, ntile=1,
                      nchunk=1, cap=cap_c, stride=capb, off=cap_l, mod_row_ctx=True, out=None)
```

```python
import functools

import numpy as np
import jax
import jax.numpy as jnp
from jax import lax
from jax.experimental import pallas as pl
from jax.experimental.pallas import tpu as pltpu

F32 = jnp.float32
BF16 = jnp.bfloat16

LANES = 128
HEAD_DIM = 128
MLSTM_HEADS = 4
MLSTM_DQK = 128
MLSTM_DV = 256
MLSTM_CHUNK = 256
GQA_HEADS = 8
GQA_KV_HEADS = 2
NAT_HEADS = 8
NAT_WIN_ROWS = 8
NAT_WIN_COLS = 16
GRID_W = 64
Q_BLOCK = 128
N_EXPERTS = 16
EC_CAPACITY_FACTOR = 2
ROPE_THETA = 10000.0
NORM_EPS = 1e-6
NEG_INF = -1e30
N_MOD = 6
N_BRANCHES = 3
BRANCH_WIDTH = 1024
TR = 256
N_GATE_COLS = 4 * MLSTM_HEADS

C_MQ = 0
C_MK = C_MQ + MLSTM_HEADS * MLSTM_DQK
C_MV = C_MK + MLSTM_HEADS * MLSTM_DQK
C_MO = C_MV + MLSTM_HEADS * MLSTM_DV
C_GQ = C_MO + MLSTM_HEADS * MLSTM_DV
C_GK = C_GQ + GQA_HEADS * HEAD_DIM
C_GV = C_GK + GQA_KV_HEADS * HEAD_DIM
C_NQ = C_GV + GQA_KV_HEADS * HEAD_DIM
C_NK = C_NQ + NAT_HEADS * HEAD_DIM
C_NV = C_NK + NAT_HEADS * HEAD_DIM
C_GATE = C_NV + NAT_HEADS * HEAD_DIM
W_IN_GATE_COL = C_GQ


def _params(sem, vmem_mb=None, fuse_inputs=None):
    return pltpu.CompilerParams(
        dimension_semantics=sem, allow_input_fusion=fuse_inputs,
        vmem_limit_bytes=None if vmem_mb is None else vmem_mb * 2 ** 20)


def _pick(n, cands):
    for c in cands:
        if n % c == 0:
            return c
    raise ValueError(f"no tile for {n} in {cands}")


def _dot(a, b):
    return jnp.dot(a, b, preferred_element_type=F32)


def _dot_nt(a, b):
    return lax.dot_general(a, b, (((1,), (1,)), ((), ())), preferred_element_type=F32)


def _sigmoid(x):
    return 0.5 * jnp.tanh(0.5 * x) + 0.5


def _rms(x):
    return x * lax.rsqrt(jnp.mean(x * x, axis=-1, keepdims=True) + NORM_EPS)


def _mod_kernel(c_ref, w_ref, b_ref, o_ref):
    c = c_ref[...]
    a = (c * _sigmoid(c)).astype(BF16)
    o_ref[0] = _dot(a, w_ref[0].astype(BF16)) + b_ref[0]


def _modulation(cc, w_mod, b_mod):
    L, D, N = w_mod.shape
    tn = _pick(N, (768, 512, 256, 128))
    return pl.pallas_call(
        _mod_kernel,
        grid=(L, N // tn),
        in_specs=[pl.BlockSpec((cc.shape[0], D), lambda l, n: (0, 0)),
                  pl.BlockSpec((1, D, tn), lambda l, n: (l, 0, n)),
                  pl.BlockSpec((1, 1, tn), lambda l, n: (l, 0, n))],
        out_specs=pl.BlockSpec((1, cc.shape[0], tn), lambda l, n: (l, 0, n)),
        out_shape=jax.ShapeDtypeStruct((L, cc.shape[0], N), F32),
        compiler_params=_params(("parallel", "parallel")),
    )(cc, w_mod, b_mod.reshape(L, 1, N))


PROJ_ROW_SPLIT = 4


def _proj_kernel(x_ref, g_ref, shx_ref, scx_ref, shc_ref, scc_ref, w_ref, wg_ref, o_ref, og_ref, h_ref, *, T):
    tm = h_ref.shape[0]
    tn = o_ref.shape[2]
    sub = tm // PROJ_ROW_SPLIT
    n = pl.program_id(2)

    @pl.when(n == 0)
    def _():
        for k in range(PROJ_ROW_SPLIT):
            rows = slice(k * sub, (k + 1) * sub)
            y = _rms(x_ref[0, rows, :]) * g_ref[...]
            tok = pl.program_id(1) * tm + k * sub + lax.broadcasted_iota(jnp.int32, (sub, 1), 0)
            is_ctx = tok >= T
            sc = jnp.where(is_ctx, scc_ref[0], scx_ref[0])
            sh = jnp.where(is_ctx, shc_ref[0], shx_ref[0])
            h = (y * (1.0 + sc) + sh).astype(BF16)
            h_ref[rows, :] = h
            og_ref[0, rows, :] = _dot(h, wg_ref[...])

    acc = _dot(h_ref[...], w_ref[...])
    is_gate = n * tn + lax.broadcasted_iota(jnp.int32, (1, tn), 1) >= C_GATE
    o_ref[0] = jnp.where(is_gate, _sigmoid(acc), acc).astype(o_ref.dtype)


def _project(xc, g, mod3, w_main, w_g, T):
    B, Tt, D = xc.shape
    N = w_main.shape[1]
    tm = Tt // PROJ_ROW_SPLIT
    tn = _pick(N, (768, 512, 384, 256, 128))
    mrow = lambda row, chunk: pl.BlockSpec((1, 1, D), lambda b, i, n: (row(b), 0, chunk))
    own, ctx = (lambda b: b), (lambda b: B)
    return pl.pallas_call(
        functools.partial(_proj_kernel, T=T), grid=(B, PROJ_ROW_SPLIT, N // tn),
        in_specs=[pl.BlockSpec((1, tm, D), lambda b, i, n: (b, i, 0)),
                  pl.BlockSpec((1, D), lambda b, i, n: (0, 0)),
                  mrow(own, 0), mrow(own, 1), mrow(ctx, 0), mrow(ctx, 1),
                  pl.BlockSpec((D, tn), lambda b, i, n: (0, n)),
                  pl.BlockSpec((D, LANES), lambda b, i, n: (0, 0))],
        out_specs=[pl.BlockSpec((1, tm, tn), lambda b, i, n: (b, i, n)),
                   pl.BlockSpec((1, tm, LANES), lambda b, i, n: (b, i, 0))],
        out_shape=[jax.ShapeDtypeStruct((B, Tt, N), BF16), jax.ShapeDtypeStruct((B, Tt, LANES), F32)],
        scratch_shapes=[pltpu.VMEM((tm, D), BF16)],
        compiler_params=_params(("parallel", "parallel", "arbitrary"), 48),
    )(xc, g.reshape(1, D), mod3, mod3, mod3, mod3, w_main, w_g)


def _qk_prep_kernel(q_ref, k_ref, nw_ref, cos_ref, sin_ref, o_ref):
    lane = lax.broadcasted_iota(jnp.int32, (q_ref.shape[1], HEAD_DIM), 1)
    first = (lane % (HEAD_DIM // 2)) < HEAD_DIM // 4
    cos = cos_ref[...]
    sin = sin_ref[...]
    for h in range(o_ref.shape[2] // HEAD_DIM):
        src, hh = (q_ref, h) if h < GQA_HEADS else (k_ref, h - GQA_HEADS)
        y = _rms(src[0, :, hh * HEAD_DIM:(hh + 1) * HEAD_DIM].astype(F32)) * nw_ref[:, h * HEAD_DIM:(h + 1) * HEAD_DIM]
        partner = jnp.where(first, pltpu.roll(y, HEAD_DIM - HEAD_DIM // 4, 1), pltpu.roll(y, HEAD_DIM // 4, 1))
        o_ref[0, :, h * HEAD_DIM:(h + 1) * HEAD_DIM] = (y * cos + partner * sin).astype(BF16)


def _qk_prep(P, nw, cos_t, sin_t):
    B, Tt, _ = P.shape
    qw, kw = GQA_HEADS * HEAD_DIM, GQA_KV_HEADS * HEAD_DIM
    w = qw + kw
    tr = _pick(Tt, (544, TR))
    return pl.pallas_call(
        _qk_prep_kernel, grid=(B, Tt // tr),
        in_specs=[pl.BlockSpec((1, tr, qw), lambda b, i: (b, i, C_GQ // qw)),
                  pl.BlockSpec((1, tr, kw), lambda b, i: (b, i, C_GK // kw)),
                  pl.BlockSpec((1, w), lambda b, i: (0, 0)),
                  pl.BlockSpec((tr, HEAD_DIM), lambda b, i: (i, 0)),
                  pl.BlockSpec((tr, HEAD_DIM), lambda b, i: (i, 0))],
        out_specs=pl.BlockSpec((1, tr, w), lambda b, i: (b, i, 0)),
        out_shape=jax.ShapeDtypeStruct((B, Tt, w), BF16),
        compiler_params=_params(("parallel", "parallel")),
    )(P, P, nw.reshape(1, w), cos_t, sin_t)


def _flash_kernel(q_ref, k_ref, v_ref, o_ref, m_ref, acc_ref, *, R, Lc, ck, n_lat_tiles):
    tile = pl.program_id(2)

    m_ref[...] = jnp.full(m_ref.shape, NEG_INF, F32)
    acc_ref[...] = jnp.zeros(acc_ref.shape, F32)

    def step(k, v):
        vaug = jnp.concatenate([v, jnp.ones(v.shape, BF16)], axis=1)
        for r in range(R):
            rows = slice(r * TR, (r + 1) * TR)
            s = _dot_nt(q_ref[0, :, r * HEAD_DIM:(r + 1) * HEAD_DIM], k)
            slabs = [s[:, c * LANES:(c + 1) * LANES] for c in range(s.shape[1] // LANES)]
            m_old = m_ref[rows, :]
            m_new = jnp.maximum(m_old, jnp.max(functools.reduce(jnp.maximum, slabs), axis=1, keepdims=True))
            alpha = jnp.exp(m_old - m_new)
            p = jnp.concatenate([jnp.exp(sl - m_new) for sl in slabs], axis=1).astype(BF16)
            acc_ref[rows, :] = jnp.concatenate([alpha, alpha], axis=1) * acc_ref[rows, :] + _dot(p, vaug)
            m_ref[rows, :] = m_new

    kv_rows = k_ref.shape[1]

    @pl.when(tile >= n_lat_tiles)
    def _():
        step(k_ref[0, kv_rows - Lc:, :], v_ref[0, kv_rows - Lc:, :])

    @pl.when(tile < n_lat_tiles)
    def _():
        def body(j, carry):
            r0 = pl.multiple_of(j * ck, ck)
            step(k_ref[0, pl.ds(r0, ck), :], v_ref[0, pl.ds(r0, ck), :])
            return carry
        lax.fori_loop(0, kv_rows // ck, body, 0)

    o = acc_ref[:, :HEAD_DIM] / acc_ref[:, HEAD_DIM:]
    for r in range(R):
        o_ref[0, :, r * HEAD_DIM:(r + 1) * HEAD_DIM] = o[r * TR:(r + 1) * TR].astype(o_ref.dtype)


def _flash(q_arr, q_col, k_arr, k_col, v_arr, v_col, *, G, R, T, Lc):
    B, Tt, _ = q_arr.shape
    qw = R * HEAD_DIM
    ck = max(c for c in range(LANES, 2304 + 1, LANES) if Tt % c == 0)
    return pl.pallas_call(
        functools.partial(_flash_kernel, R=R, Lc=Lc, ck=ck, n_lat_tiles=T // TR),
        grid=(B, G, Tt // TR),
        in_specs=[pl.BlockSpec((1, TR, qw), lambda b, g, i: (b, i, q_col // qw + g)),
                  pl.BlockSpec((1, Tt, HEAD_DIM), lambda b, g, i: (b, 0, k_col // HEAD_DIM + g)),
                  pl.BlockSpec((1, Tt, HEAD_DIM), lambda b, g, i: (b, 0, v_col // HEAD_DIM + g))],
        out_specs=pl.BlockSpec((1, TR, qw), lambda b, g, i: (b, i, g)),
        out_shape=jax.ShapeDtypeStruct((B, Tt, G * qw), BF16),
        scratch_shapes=[pltpu.VMEM((R * TR, LANES), F32), pltpu.VMEM((R * TR, 2 * HEAD_DIM), F32)],
        compiler_params=_params(("parallel", "parallel", "arbitrary"), 48),
    )(q_arr, k_arr, v_arr)


def _nat_geometry(T):
    rows = T // GRID_W
    kh = min(NAT_WIN_ROWS, rows)
    nkr = min(kh + 1, rows)
    qbr = Q_BLOCK // GRID_W
    nbr = rows // qbr
    kb = np.zeros(nbr, np.int32)
    var = np.zeros(nbr, np.int32)
    variants = []
    for j in range(nbr):
        qrow = j * qbr + np.arange(qbr)
        start_r = np.clip(qrow - kh // 2, 0, rows - kh)
        kb[j] = min(start_r[0], rows - nkr)
        sig = (int(kb[j] - j * qbr), tuple(int(s - kb[j]) for s in start_r))
        if sig not in variants:
            variants.append(sig)
        var[j] = variants.index(sig)
    return kh, nkr, qbr, nbr, kb, var, variants


def _nat_bias_kernel(rb_ref, o_ref, *, variants, kh, nkr, qbr):
    h = pl.program_id(0)
    ndr = 2 * NAT_WIN_ROWS - 1
    ndc = 2 * NAT_WIN_COLS - 1
    qc = lax.broadcasted_iota(jnp.int32, (GRID_W, GRID_W), 0)
    kc = lax.broadcasted_iota(jnp.int32, (GRID_W, GRID_W), 1)
    dc = kc - qc + NAT_WIN_COLS - 1
    start_c = jnp.clip(qc - NAT_WIN_COLS // 2, 0, GRID_W - NAT_WIN_COLS)
    col_in = (kc >= start_c) & (kc < start_c + NAT_WIN_COLS)
    masked = jnp.full((GRID_W, GRID_W), NEG_INF, F32)
    toeplitz = []
    for dr in range(ndr):
        t = jnp.zeros((GRID_W, GRID_W), F32)
        for d in range(ndc):
            t = jnp.where(dc == d, rb_ref[(h * ndr + dr) * ndc + d], t)
        toeplitz.append(jnp.where(col_in, t, NEG_INF))
    for v, (delta, srel) in enumerate(variants):
        for qr in range(qbr):
            blocks = []
            for kr in range(nkr):
                if srel[qr] <= kr < srel[qr] + kh:
                    blocks.append(toeplitz[int(np.clip(kr + delta - qr + NAT_WIN_ROWS - 1, 0, ndr - 1))])
                else:
                    blocks.append(masked)
            rows = slice(qr * GRID_W, (qr + 1) * GRID_W)
            for kr in range(0, nkr - 1, 2):
                o_ref[v, 0, rows, kr * GRID_W:(kr + 2) * GRID_W] = jnp.concatenate(blocks[kr:kr + 2], axis=1)
            if nkr % 2:
                o_ref[v, 0, rows, (nkr - 1) * GRID_W:] = blocks[-1]


def _nat_bias_tables(rel_bias, T):
    kh, nkr, qbr, _, _, _, variants = _nat_geometry(T)
    V = len(variants)
    nk = nkr * GRID_W
    return pl.pallas_call(
        functools.partial(_nat_bias_kernel, variants=variants, kh=kh, nkr=nkr, qbr=qbr),
        grid=(NAT_HEADS,),
        in_specs=[pl.BlockSpec(memory_space=pltpu.SMEM)],
        out_specs=pl.BlockSpec((V, 1, Q_BLOCK, nk), lambda h: (0, h, 0, 0)),
        out_shape=jax.ShapeDtypeStruct((V, NAT_HEADS, Q_BLOCK, nk), F32),
        compiler_params=_params(("parallel",)),
    )(rel_bias.reshape(-1))


def _nat_kernel(kb_ref, var_ref, q_ref, k_ref, v_ref, bias_ref, o_ref, s1_ref, s2_ref, *, T, nbr, nk, scale):
    kc = k_ref[0, T:, :]
    vc = v_ref[0, T:, :]

    def scores(j, slot):
        q = q_ref[0, pl.ds(pl.multiple_of(j * Q_BLOCK, Q_BLOCK), Q_BLOCK), :]
        r0 = pl.multiple_of(kb_ref[j] * GRID_W, GRID_W)
        s1_ref[slot] = _dot_nt(q, k_ref[0, pl.ds(r0, nk), :]) * scale + bias_ref[var_ref[j], 0]
        s2_ref[slot] = _dot_nt(q, kc) * scale

    def finish(j, slot):
        s1 = s1_ref[slot]
        s2 = s2_ref[slot]
        r0 = pl.multiple_of(kb_ref[j] * GRID_W, GRID_W)
        m = jnp.maximum(jnp.max(s1, axis=1, keepdims=True), jnp.max(s2, axis=1, keepdims=True))
        p1 = jnp.exp(s1 - m)
        p2 = jnp.exp(s2 - m)
        l = jnp.sum(p1, axis=1, keepdims=True) + jnp.sum(p2, axis=1, keepdims=True)
        o = _dot(p1.astype(BF16), v_ref[0, pl.ds(r0, nk), :]) + _dot(p2.astype(BF16), vc)
        o_ref[0, pl.ds(pl.multiple_of(j * Q_BLOCK, Q_BLOCK), Q_BLOCK), :] = (o / l).astype(o_ref.dtype)

    scores(0, 0)

    def body(i, carry):
        j = 2 * i
        scores(j + 1, 1)
        finish(j, 0)
        scores(jnp.minimum(j + 2, nbr - 1), 0)
        finish(j + 1, 1)
        return carry

    lax.fori_loop(0, nbr // 2, body, 0, unroll=4)

    s = _dot_nt(q_ref[0, T:, :], kc) * scale
    p = jnp.exp(s - jnp.max(s, axis=1, keepdims=True))
    o_ref[0, T:, :] = (_dot(p.astype(BF16), vc) / jnp.sum(p, axis=1, keepdims=True)).astype(o_ref.dtype)


def _nat(P, bias_tab, T, Lc, scale):
    B, Tt, _ = P.shape
    _, nkr, _, nbr, kb, var, _ = _nat_geometry(T)
    nk = nkr * GRID_W
    V = bias_tab.shape[0]
    grid_spec = pltpu.PrefetchScalarGridSpec(
        num_scalar_prefetch=2, grid=(B, NAT_HEADS),
        in_specs=[pl.BlockSpec((1, Tt, HEAD_DIM), lambda b, h, kb_, vr: (b, 0, C_NQ // HEAD_DIM + h)),
                  pl.BlockSpec((1, Tt, HEAD_DIM), lambda b, h, kb_, vr: (b, 0, C_NK // HEAD_DIM + h)),
                  pl.BlockSpec((1, Tt, HEAD_DIM), lambda b, h, kb_, vr: (b, 0, C_NV // HEAD_DIM + h)),
                  pl.BlockSpec((V, 1, Q_BLOCK, nk), lambda b, h, kb_, vr: (0, h, 0, 0))],
        out_specs=pl.BlockSpec((1, Tt, HEAD_DIM), lambda b, h, kb_, vr: (b, 0, h)),
        scratch_shapes=[pltpu.VMEM((2, Q_BLOCK, nk), F32), pltpu.VMEM((2, Q_BLOCK, Lc), F32)])
    assert nbr % 2 == 0
    return pl.pallas_call(
        functools.partial(_nat_kernel, T=T, nbr=nbr, nk=nk, scale=scale),
        grid_spec=grid_spec,
        out_shape=jax.ShapeDtypeStruct((B, Tt, NAT_HEADS * HEAD_DIM), BF16),
        compiler_params=_params(("parallel", "parallel")),
    )(jnp.asarray(kb), jnp.asarray(var), P, P, P, bias_tab)


def _mlstm_kernel(qf_ref, kf_ref, vf_ref, gf_ref, ktf_ref, qb_ref, kb_ref, vb_ref, gb_ref, ktb_ref, bias_ref,
                  hf_ref, hb_ref, st_ref, m_ref, *, kscale):
    L = MLSTM_CHUNK
    dv = MLSTM_DV

    @pl.when(pl.program_id(1) == 0)
    def _():
        st_ref[...] = jnp.zeros(st_ref.shape, F32)
        m_ref[...] = jnp.full(m_ref.shape, NEG_INF, F32)

    ti = lax.broadcasted_iota(jnp.int32, (L, L), 0)
    si = lax.broadcasted_iota(jnp.int32, (L, L), 1)
    ones_col = (lax.broadcasted_iota(jnp.int32, (L, LANES), 1) == 0).astype(BF16)

    for dirn, (q_ref, k_ref, v_ref, g_ref, kt_ref, h_ref) in enumerate(
            ((qf_ref, kf_ref, vf_ref, gf_ref, ktf_ref, hf_ref), (qb_ref, kb_ref, vb_ref, gb_ref, ktb_ref, hb_ref))):
        tri = (si <= ti) if dirn == 0 else (si >= ti)
        G = g_ref[0] + bias_ref[...]
        LF = jnp.minimum(G, 0.0) - jnp.log(1.0 + jnp.exp(-jnp.abs(G)))
        lf_hi = LF.astype(BF16)
        lf_lo = (LF - lf_hi.astype(F32)).astype(BF16)
        trib = tri.astype(BF16)
        Bc = _dot(trib, lf_hi) + _dot(trib, lf_lo)
        BcT = Bc.T
        GT = G.T
        bl_row = Bc[L - 1:L, :] if dirn == 0 else Bc[0:1, :]
        for h in range(MLSTM_HEADS):
            ci = dirn * 2 * MLSTM_HEADS + h
            cf = ci + MLSTM_HEADS
            bcol = Bc[:, cf:cf + 1]
            brow = BcT[cf:cf + 1, :]
            licol = G[:, ci:ci + 1]
            lirow = GT[ci:ci + 1, :]
            bl = bl_row[:, cf:cf + 1]
            m_old = m_ref[dirn, h]
            gcol = bl - bcol + licol
            m_new = jnp.maximum(bl + m_old, jnp.max(gcol, axis=0, keepdims=True))
            decay = jnp.exp(bl + m_old - m_new)
            wk = jnp.exp(gcol - m_new)
            dmat = jnp.where(tri, bcol - brow + lirow, NEG_INF)
            inter = bcol + m_old
            m_t = jnp.maximum(inter, jnp.max(dmat, axis=1, keepdims=True))
            w = jnp.exp(dmat - m_t)
            a = jnp.exp(inter - m_t)
            qh = q_ref[0, :, h * MLSTM_DQK:(h + 1) * MLSTM_DQK]
            kh = k_ref[0, :, h * MLSTM_DQK:(h + 1) * MLSTM_DQK]
            vaug = jnp.concatenate([v_ref[0, :, h * dv:(h + 1) * dv], ones_col], axis=1)
            smat = _dot_nt(qh, kh) * kscale * w
            state = st_ref[dirn, h]
            nd = _dot(smat.astype(BF16), vaug) + a * _dot(qh, state.astype(BF16))
            den = jnp.maximum(jnp.abs(nd[:, dv:dv + 1]), jnp.exp(-m_t))
            h_ref[0, :, h * dv:(h + 1) * dv] = (nd[:, :dv] / den).astype(h_ref.dtype)
            kT = kt_ref[0, h * MLSTM_DQK:(h + 1) * MLSTM_DQK, :]
            upd = _dot(kT, (wk * vaug.astype(F32)).astype(BF16)) * kscale
            st_ref[dirn, h] = decay * state + upd
            m_ref[dirn, h] = m_new


def _mlstm(P, gates, gate_bias, T, Lc):
    B, Tt, _ = P.shape
    L = MLSTM_CHUNK
    kt = jnp.swapaxes(P[:, :, C_MK:C_MK + MLSTM_HEADS * MLSTM_DQK], 1, 2)
    ncl, ncc = T // L, Lc // L
    fwd = lambda i: jnp.where(i < ncc, ncl + i, i - ncc)
    bwd = lambda i: jnp.where(i < ncc, ncl + ncc - 1 - i, ncl - 1 - (i - ncc))
    qw = MLSTM_HEADS * MLSTM_DQK
    vw = MLSTM_HEADS * MLSTM_DV

    def specs(order):
        return [pl.BlockSpec((1, L, qw), lambda b, i: (b, order(i), C_MQ // qw)),
                pl.BlockSpec((1, L, qw), lambda b, i: (b, order(i), C_MK // qw)),
                pl.BlockSpec((1, L, vw), lambda b, i: (b, order(i), C_MV // vw)),
                pl.BlockSpec((1, L, LANES), lambda b, i: (b, order(i), 0)),
                pl.BlockSpec((1, qw, L), lambda b, i: (b, 0, order(i)))]

    out = jax.ShapeDtypeStruct((B, Tt, vw), BF16)
    return pl.pallas_call(
        functools.partial(_mlstm_kernel, kscale=MLSTM_DQK ** -0.5),
        grid=(B, ncl + ncc),
        in_specs=specs(fwd) + specs(bwd) + [pl.BlockSpec((1, LANES), lambda b, i: (0, 0))],
        out_specs=[pl.BlockSpec((1, L, vw), lambda b, i: (b, fwd(i), 0)),
                   pl.BlockSpec((1, L, vw), lambda b, i: (b, bwd(i), 0))],
        out_shape=[out, out],
        scratch_shapes=[pltpu.VMEM((2, MLSTM_HEADS, MLSTM_DQK, MLSTM_DV + LANES), F32),
                        pltpu.VMEM((2, MLSTM_HEADS, 1, 1), F32)],
        compiler_params=_params(("parallel", "arbitrary")),
    )(P, P, P, gates, kt, P, P, P, gates, kt, gate_bias)


def _merge_kernel(hf_ref, hb_ref, mo_ref, yg_ref, yn_ref, g0_ref, g1_ref, g2_ref, wb_ref, o_ref, ym_ref):
    @pl.when(pl.program_id(1) == 0)
    def _():
        h = hf_ref[...].astype(F32) + hb_ref[...].astype(F32)
        for hh in range(MLSTM_HEADS):
            cols = slice(hh * MLSTM_DV, (hh + 1) * MLSTM_DV)
            ym_ref[:, cols] = (_sigmoid(mo_ref[:, cols].astype(F32)) * _rms(h[:, cols])).astype(BF16)

    gate = lambda r: r[...].astype(F32)
    acc = gate(g0_ref) * _dot(ym_ref[...], wb_ref[0])
    acc += gate(g1_ref) * _dot(yg_ref[...], wb_ref[1])
    acc += gate(g2_ref) * _dot(yn_ref[...], wb_ref[2])
    o_ref[...] = acc.astype(BF16)


def _merge(hf, hb, P2, yg, yn, wb):
    M, W = hf.shape
    D = wb.shape[2]
    tm = _pick(M, (1024, 512, 256))
    tn = _pick(D, (512, 256))
    row = lambda c: pl.BlockSpec((tm, W), lambda i, n: (i, c))
    gate = lambda br: pl.BlockSpec((tm, tn), lambda i, n: (i, (C_GATE + br * D) // tn + n))
    return pl.pallas_call(
        _merge_kernel, grid=(M // tm, D // tn),
        in_specs=[row(0), row(0), row(C_MO // W), row(0), row(0), gate(0), gate(1), gate(2),
                  pl.BlockSpec((N_BRANCHES, W, tn), lambda i, n: (0, 0, n))],
        out_specs=pl.BlockSpec((tm, tn), lambda i, n: (i, n)),
        out_shape=jax.ShapeDtypeStruct((M, D), BF16),
        scratch_shapes=[pltpu.VMEM((tm, W), BF16)],
        compiler_params=_params(("parallel", "arbitrary"), 56),
    )(hf, hb, P2, yg, yn, P2, P2, P2, wb)


def _wout_post_kernel(a_ref, w_ref, x_ref, gp_ref, gate_ref, gf_ref, sh_ref, sc_ref, whi_ref, wlo_ref,
                      o_ref, h_ref, acol_ref, arow_ref):
    y = _dot(a_ref[0], w_ref[...])
    x = x_ref[0] + gate_ref[0] * (_rms(y) * gp_ref[...])
    o_ref[0] = x

    h = _rms(x) * gf_ref[...] * (1.0 + sc_ref[0]) + sh_ref[0]
    hb = h.astype(BF16)
    h_ref[0] = hb
    hlo = (h - hb.astype(F32)).astype(BF16)
    logits = _dot(hb, whi_ref[...]) + _dot(hlo, whi_ref[...]) + _dot(hb, wlo_ref[...])
    lane = lax.broadcasted_iota(jnp.int32, logits.shape, 1)
    valid = lane < N_EXPERTS
    logits = jnp.where(valid, logits, NEG_INF)
    ex = jnp.where(valid, jnp.exp(logits - jnp.max(logits, axis=1, keepdims=True)), 0.0)
    aff = ex / jnp.sum(ex, axis=1, keepdims=True)
    acol_ref[0] = aff
    arow_ref[0] = aff.T[:N_EXPERTS]


def _wout_post(merged, w_out, xc, g_post, mod3, gate_chunk, g_ffn, shift_chunk, scale_chunk, whi, wlo):
    B, Tt, D = xc.shape
    nT = Tt // TR
    tile = pl.BlockSpec((1, TR, D), lambda b, i: (b, i, 0))
    vec = pl.BlockSpec((1, D), lambda b, i: (0, 0))
    row = lambda b, i: jnp.where(i == nT - 1, B, b)
    mod = lambda chunk: pl.BlockSpec((1, 1, D), lambda b, i: (row(b, i), 0, chunk))
    wspec = pl.BlockSpec((D, LANES), lambda b, i: (0, 0))
    return pl.pallas_call(
        _wout_post_kernel, grid=(B, nT),
        in_specs=[tile, pl.BlockSpec((D, D), lambda b, i: (0, 0)), tile, vec, mod(gate_chunk),
                  vec, mod(shift_chunk), mod(scale_chunk), wspec, wspec],
        out_specs=[tile, tile, pl.BlockSpec((1, TR, LANES), lambda b, i: (b, i, 0)),
                   pl.BlockSpec((1, N_EXPERTS, TR), lambda b, i: (b, 0, i))],
        out_shape=[jax.ShapeDtypeStruct((B, Tt, D), F32), jax.ShapeDtypeStruct((B, Tt, D), BF16),
                   jax.ShapeDtypeStruct((B, Tt, LANES), F32),
                   jax.ShapeDtypeStruct((B, N_EXPERTS, Tt), F32)],
        compiler_params=_params(("parallel", "parallel"), 48,
                                fuse_inputs=[i in (1, 8, 9) for i in range(10)]),
    )(merged, w_out, xc, g_post.reshape(1, D), mod3, g_ffn.reshape(1, D), mod3, mod3, whi, wlo)


def _route_segment(arow, acol, k, chunk):
    E, T = arow.shape
    nchunk = T // chunk
    ind = lambda m: jnp.where(m, 1.0, 0.0)
    bits = pltpu.bitcast(arow, jnp.int32)
    thr = jnp.zeros((E, 1), jnp.int32)
    for bit in range(30, -1, -1):
        cand = thr | (1 << bit)
        cnt = jnp.sum(ind(bits >= cand), axis=1, keepdims=True)
        thr = jnp.where(cnt >= k, cand, thr)
    thr_val = pltpu.bitcast(thr, F32)
    gt = arow > thr_val
    eq = arow == thr_val
    need = k - jnp.sum(ind(gt), axis=1, keepdims=True)

    r_i = lax.broadcasted_iota(jnp.int32, (chunk, chunk), 0)
    c_i = lax.broadcasted_iota(jnp.int32, (chunk, chunk), 1)
    before = ind(r_i < c_i).astype(BF16)
    after = ind(c_i < r_i).astype(BF16)

    def prefix_rows(mask):
        carry = jnp.zeros((E, 1), F32)
        parts, carries = [], [carry]
        for c in range(nchunk):
            mc = ind(mask[:, c * chunk:(c + 1) * chunk]).astype(BF16)
            parts.append(_dot(mc, before) + carry)
            carry = carry + jnp.sum(mc.astype(F32), axis=1, keepdims=True)
            carries.append(carry)
        return jnp.concatenate(parts, axis=1), carries

    tie_rank, _ = prefix_rows(eq)
    sel = gt | (eq & (tie_rank < need))
    pos, carries = prefix_rows(sel)
    prow = jnp.where(sel, pos, -1.0)
    lane = lax.broadcasted_iota(jnp.int32, (E, LANES), 1)
    starts = jnp.zeros((E, LANES), F32)
    for c, cv in enumerate(carries):
        starts = jnp.where(lane == c, cv, starts)

    sub = lax.broadcasted_iota(jnp.int32, (E, LANES), 0)
    diag = sub == lane
    thr_row = jnp.sum(jnp.where(diag, thr_val, 0.0), axis=0, keepdims=True)
    need_row = jnp.sum(jnp.where(diag, need, 0.0), axis=0, keepdims=True)
    lane_ok = lax.broadcasted_iota(jnp.int32, (1, LANES), 1) < E
    gt_c = (acol > thr_row) & lane_ok
    eq_c = (acol == thr_row) & lane_ok

    def prefix_cols(mask):
        carry = jnp.zeros((1, LANES), F32)
        parts = []
        for c in range(nchunk):
            mc = ind(mask[c * chunk:(c + 1) * chunk]).astype(BF16)
            parts.append(_dot(after, mc) + carry)
            carry = carry + jnp.sum(mc.astype(F32), axis=0, keepdims=True)
        return jnp.concatenate(parts, axis=0)

    sel_c = gt_c | (eq_c & (prefix_cols(eq_c) < need_row))
    pcol = jnp.where(sel_c, prefix_cols(sel_c), -1.0)
    return prow, pcol, starts


def _route_kernel(arow_ref, acol_ref, prow_ref, pcol_ref, starts_ref, *, T, Lc, cap_l, cap_c):
    prow, pcol, starts = _route_segment(arow_ref[0, :, :T], acol_ref[0, :T], cap_l, TR)
    prow_ref[0, :, :T] = prow
    pcol_ref[0, :T] = pcol
    starts_ref[0] = starts.astype(jnp.int32)
    prow, pcol, _ = _route_segment(arow_ref[0, :, T:], acol_ref[0, T:], cap_c, TR)
    prow_ref[0, :, T:] = prow
    pcol_ref[0, T:] = pcol


def _route(arow, acol, T, Lc, cap_l, cap_c):
    B, E, Tt = arow.shape
    return pl.pallas_call(
        functools.partial(_route_kernel, T=T, Lc=Lc, cap_l=cap_l, cap_c=cap_c),
        grid=(B,),
        in_specs=[pl.BlockSpec((1, E, Tt), lambda b: (b, 0, 0)), pl.BlockSpec((1, Tt, LANES), lambda b: (b, 0, 0))],
        out_specs=[pl.BlockSpec((1, E, Tt), lambda b: (b, 0, 0)), pl.BlockSpec((1, Tt, LANES), lambda b: (b, 0, 0)),
                   pl.BlockSpec((1, E, LANES), lambda b: (b, 0, 0))],
        out_shape=[jax.ShapeDtypeStruct((B, E, Tt), F32), jax.ShapeDtypeStruct((B, Tt, LANES), F32),
                   jax.ShapeDtypeStruct((B, E, LANES), jnp.int32)],
        compiler_params=_params(("parallel",), 48),
    )(arow, acol)


def _gather_kernel(st_ref, p_ref, a_ref, h_ref, o_ref, g_ref, acc_ref, gacc_ref, *, nchunk, cap_l, cap_c, st):
    b = pl.program_id(0)
    e = pl.program_id(1)
    base = (b * N_EXPERTS + e) * (nchunk + 1)
    for S in range(cap_l // st):
        acc_ref[...] = jnp.zeros(acc_ref.shape, F32)
        gacc_ref[...] = jnp.zeros(gacc_ref.shape, F32)
        slot = (lax.broadcasted_iota(jnp.int32, (st, TR), 0) + S * st).astype(F32)

        def body(c, carry):
            lo = st_ref[base + c]
            hi = st_ref[base + c + 1]

            @pl.when((lo < (S + 1) * st) & (hi > S * st))
            def _():
                hit = p_ref[0, pl.ds(c, 1), :] == slot
                r0 = pl.multiple_of(c * TR, TR)
                acc_ref[...] += _dot(jnp.where(hit, 1.0, 0.0).astype(BF16), h_ref[0, pl.ds(r0, TR), :])
                gacc_ref[...] += jnp.sum(jnp.where(hit, a_ref[0, pl.ds(c, 1), :], 0.0), axis=1, keepdims=True)
            return carry

        lax.fori_loop(0, nchunk, body, 0)
        o_ref[0, S * st:(S + 1) * st, :] = acc_ref[...].astype(BF16)
        g_ref[0, S * st:(S + 1) * st, :] = gacc_ref[...]

    if cap_c:
        hit = p_ref[0, nchunk:nchunk + 1, :] == lax.broadcasted_iota(jnp.int32, (cap_c, TR), 0).astype(F32)
        o_ref[0, cap_l:, :] = _dot(jnp.where(hit, 1.0, 0.0).astype(BF16), h_ref[0, nchunk * TR:, :]).astype(BF16)
        g_ref[0, cap_l:, :] = jnp.broadcast_to(
            jnp.sum(jnp.where(hit, a_ref[0, nchunk:nchunk + 1, :], 0.0), axis=1, keepdims=True), (cap_c, LANES))


def _gather(starts, prow3, arow3, hbf, *, nchunk, cap_l, cap_c):
    B, Tt, D = hbf.shape
    E = N_EXPERTS
    nT = Tt // TR
    st = min(cap_l, 128)
    capb = cap_l + cap_c
    pspec = pl.BlockSpec((1, nT, TR), lambda b, e, s: (b * E + e, 0, 0))
    grid_spec = pltpu.PrefetchScalarGridSpec(
        num_scalar_prefetch=1, grid=(B, E),
        in_specs=[pspec, pspec, pl.BlockSpec((1, Tt, D), lambda b, e, s: (b, 0, 0))],
        out_specs=[pl.BlockSpec((1, capb, D), lambda b, e, s: (e, b, 0)),
                   pl.BlockSpec((1, capb, LANES), lambda b, e, s: (e, b, 0))],
        scratch_shapes=[pltpu.VMEM((st, D), F32), pltpu.VMEM((st, LANES), F32)])
    return pl.pallas_call(
        functools.partial(_gather_kernel, nchunk=nchunk, cap_l=cap_l, cap_c=cap_c, st=st),
        grid_spec=grid_spec,
        out_shape=[jax.ShapeDtypeStruct((E, B * capb, D), BF16), jax.ShapeDtypeStruct((E, B * capb, LANES), F32)],
        compiler_params=_params(("parallel", "arbitrary"), 56),
    )(starts, prow3, arow3, hbf)


def _ffn1_kernel(x_ref, wg_ref, wu_ref, o_ref, wgb_ref, wub_ref):
    @pl.when(pl.program_id(2) == 0)
    def _():
        wgb_ref[...] = wg_ref[0].astype(BF16)
        wub_ref[...] = wu_ref[0].astype(BF16)
    x = x_ref[0]
    g = _dot(x, wgb_ref[...])
    u = _dot(x, wub_ref[...])
    o_ref[0] = (g * _sigmoid(g) * u).astype(BF16)


def _ffn2_kernel(x_ref, w_ref, g_ref, o_ref, wb_ref):
    @pl.when(pl.program_id(2) == 0)
    def _():
        wb_ref[...] = w_ref[0].astype(BF16)
    o_ref[0] = (_dot(x_ref[0], wb_ref[...]) * g_ref[0, :, :1]).astype(BF16)


def _expert_matmul(kernel, x, ws, layer, row_scale=None):
    E, S, K = x.shape
    N = ws[0].shape[3]
    ws = [w.reshape(-1, K, N) for w in ws]
    tm = _pick(S, (1088, 1024, 544, 512, 272, 256, 160, 128))
    tn = _pick(N, (1024, 512, 256) if len(ws) == 1 else (512, 256))
    wspec = pl.BlockSpec((1, K, tn), lambda e, n, m: (layer * E + e, 0, n))
    in_specs = [pl.BlockSpec((1, tm, K), lambda e, n, m: (e, m, 0))] + [wspec] * len(ws)
    args = [x, *ws]
    if row_scale is not None:
        in_specs.append(pl.BlockSpec((1, tm, LANES), lambda e, n, m: (e, m, 0)))
        args.append(row_scale)
    return pl.pallas_call(
        kernel, grid=(E, N // tn, S // tm),
        in_specs=in_specs,
        out_specs=pl.BlockSpec((1, tm, tn), lambda e, n, m: (e, m, n)),
        out_shape=jax.ShapeDtypeStruct((E, S, N), BF16),
        scratch_shapes=[pltpu.VMEM((K, tn), BF16)] * len(ws),
        compiler_params=_params(("parallel", "parallel", "arbitrary"), 48),
    )(*args)


SLOT_ALIGN = 16
COMBINE_WINDOW = 64


def _combine_kernel(st_ref, pc_ref, ye_ref, x_ref, gp_ref, gate_ref, o_ref, win_ref, acc_ref, sem,
                    *, nchunk, cap, W, stride, off):
    E = N_EXPERTS
    b = pl.program_id(0)
    t = pl.program_id(1)
    nb, nt = pl.num_programs(0), pl.num_programs(1)
    step = b * nt + t
    buf = step % 2

    def bounds(bb, tt, e):
        base = (bb * E + e) * (nchunk + 1) + tt
        return (st_ref[base] // SLOT_ALIGN) * SLOT_ALIGN, st_ref[base + 1]

    def window(bb, tt, e, r):
        first, _ = bounds(bb, tt, e)
        want = first + r * W
        return want, jnp.minimum(want, cap - W)

    def copy(bb, tt, e, r, into):
        _, start = window(bb, tt, e, r)
        row0 = pl.multiple_of(bb * stride + off + start, SLOT_ALIGN)
        return pltpu.make_async_copy(ye_ref.at[e, pl.ds(row0, W), :],
                                     win_ref.at[into, pl.ds(e * W, W), :], sem.at[into, e])

    @pl.when(step == 0)
    def _():
        for e in range(E):
            copy(b, t, e, 0, buf).start()

    @pl.when(step + 1 < nb * nt)
    def _():
        last_t = t + 1 == nt
        b_next = jnp.where(last_t, b + 1, b)
        t_next = jnp.where(last_t, 0, t + 1)
        for e in range(E):
            copy(b_next, t_next, e, 0, 1 - buf).start()

    rounds = jnp.int32(1)
    for e in range(E):
        first, hi = bounds(b, t, e)
        rounds = jnp.maximum(rounds, (hi - first + W - 1) // W)

    pos1 = pc_ref[0] + 1.0
    pos_hi = jnp.floor(pos1 * (1.0 / SLOT_ALIGN))
    pos_lo = pos1 - pos_hi * SLOT_ALIGN
    col = lax.broadcasted_iota(jnp.int32, (LANES, E * W), 1)
    row = lax.broadcasted_iota(jnp.int32, (LANES, E * W), 0)
    expand = jnp.where(col // W == row, 1.0, 0.0).astype(BF16)
    pos_rep = SLOT_ALIGN * _dot(pos_hi.astype(BF16), expand) + _dot(pos_lo.astype(BF16), expand) - 1.0
    lane = lax.broadcasted_iota(jnp.int32, (1, E * W), 1)
    in_win = (lane % W).astype(F32)

    def onehot(r):
        want_row = jnp.zeros((1, E * W), F32)
        start_row = jnp.zeros((1, E * W), F32)
        for e in range(E):
            want, start = window(b, t, e, r)
            mine = lane // W == e
            want_row = jnp.where(mine, want.astype(F32), want_row)
            start_row = jnp.where(mine, start.astype(F32), start_row)
        hit = (pos_rep - start_row == in_win) & (pos_rep >= want_row)
        return jnp.where(hit, 1.0, 0.0).astype(BF16)

    lhs = onehot(0)
    for e in range(E):
        copy(b, t, e, 0, buf).wait()
    acc_ref[...] = _dot(lhs, win_ref[buf])

    def more(r, carry):
        for e in range(E):
            copy(b, t, e, r, buf).start()
        lhs_r = onehot(r)
        for e in range(E):
            copy(b, t, e, r, buf).wait()
        acc_ref[...] += _dot(lhs_r, win_ref[buf])
        return carry

    lax.fori_loop(1, rounds, more, 0)
    o_ref[0] = x_ref[0] + gate_ref[0] * (_rms(acc_ref[...]) * gp_ref[...])


def _combine(starts, pcol, ye, xc, g_post, mod3, gate_chunk, *, tile0, ntile, nchunk, cap, stride, off,
             mod_row_ctx, out):
    B, Tt, D = xc.shape
    E = N_EXPERTS
    W = min(cap, COMBINE_WINDOW)
    fresh = out is not None
    if not fresh:
        out = xc
    tok = lambda w: pl.BlockSpec((1, TR, w), lambda b, t, s: (b, tile0 + t, 0))
    mrow = (lambda b: B) if mod_row_ctx else (lambda b: b)
    in_specs = [tok(LANES), pl.BlockSpec(memory_space=pl.ANY), tok(D),
                pl.BlockSpec((1, D), lambda b, t, s: (0, 0)),
                pl.BlockSpec((1, 1, D), lambda b, t, s: (mrow(b), 0, gate_chunk))]
    args = [pcol, ye, xc, g_post.reshape(1, D), mod3]
    kern = functools.partial(_combine_kernel, nchunk=nchunk, cap=cap, W=W, stride=stride, off=off)
    if fresh:
        out_spec = pl.BlockSpec((1, TR, D), lambda b, t, s: (b, t, 0))
        aliases = {}
    else:
        out_spec = tok(D)
        aliases = {3: 0}
    grid_spec = pltpu.PrefetchScalarGridSpec(
        num_scalar_prefetch=1, grid=(B, ntile), in_specs=in_specs, out_specs=out_spec,
        scratch_shapes=[pltpu.VMEM((2, E * W, D), BF16), pltpu.VMEM((TR, D), F32),
                        pltpu.SemaphoreType.DMA((2, E))])
    return pl.pallas_call(
        kern, grid_spec=grid_spec,
        out_shape=jax.ShapeDtypeStruct(out.shape, F32),
        input_output_aliases=aliases,
        compiler_params=_params(("arbitrary", "arbitrary"), 48),
    )(starts, *args)


def _rope_tables(T, Lc):
    nf = HEAD_DIM // 4
    t = np.arange(T)
    inv = ROPE_THETA ** (-jnp.arange(nf, dtype=F32) / nf)
    row = jnp.asarray(t // GRID_W, F32)
    col = jnp.asarray(t % GRID_W, F32)
    ang = jnp.stack([row[:, None] * inv, col[:, None] * inv], axis=1)
    cos = jnp.broadcast_to(jnp.cos(ang)[:, :, None, :], (T, 2, 2, nf)).reshape(T, HEAD_DIM)
    sin = jnp.broadcast_to(jnp.sin(ang)[:, :, None, :], (T, 2, 2, nf)).reshape(T, HEAD_DIM)
    sign = jnp.asarray(np.tile(np.repeat([-1.0, 1.0], nf), 2), F32)
    cos = jnp.concatenate([cos, jnp.ones((Lc, HEAD_DIM), F32)], axis=0)
    sin = jnp.concatenate([sin * sign, jnp.zeros((Lc, HEAD_DIM), F32)], axis=0)
    return cos, sin


def _split_bf16(w):
    hi = w.astype(BF16)
    return hi, (w - hi.astype(F32)).astype(BF16)


def kernel(x, c, ctx, c_ctx, w_mod, b_mod, g_pre_mix, g_post_mix, g_pre_ffn, g_post_ffn, w_in, mlstm_gate_bias, gqa_q_norm, gqa_k_norm, nat_rel_bias, w_branch, w_out, w_router, w_expert_gate, w_expert_up, w_expert_down):
    B, T, D = x.shape
    Lc = ctx.shape[1]
    depth = w_mod.shape[0]
    E = N_EXPERTS
    assert Lc == TR and T % (2 * TR) == 0 and w_router.shape[2] == E
    Tt = T + Lc
    nT = Tt // TR
    M = B * Tt
    cap_l = EC_CAPACITY_FACTOR * T // E
    cap_c = EC_CAPACITY_FACTOR * Lc // E
    scale = HEAD_DIM ** -0.5
    nchunk = T // TR

    xc = jnp.concatenate([x, ctx], axis=1)
    rows_c = -(-(B + 1) // 8) * 8
    cc = jnp.zeros((rows_c, D), F32).at[:B].set(c).at[B].set(c_ctx)
    mod = _modulation(cc, w_mod, b_mod)
    cos_t, sin_t = _rope_tables(T, Lc)

    for l in range(depth):
        last = l == depth - 1
        mod3 = mod[l].reshape(rows_c, 1, N_MOD * D)
        w_l = w_in[l]
        w_main = jnp.concatenate([w_l[:, :W_IN_GATE_COL], w_l[:, W_IN_GATE_COL + N_GATE_COLS:]], axis=1).astype(BF16)
        w_g = jnp.pad(w_l[:, W_IN_GATE_COL:W_IN_GATE_COL + N_GATE_COLS], ((0, 0), (0, LANES - N_GATE_COLS))).astype(BF16)
        gate_bias = jnp.pad(mlstm_gate_bias[l], (0, LANES - N_GATE_COLS)).reshape(1, LANES)
        nw = jnp.concatenate([jnp.tile(gqa_q_norm[l][None] * scale, (GQA_HEADS, 1)),
                              jnp.tile(gqa_k_norm[l][None], (GQA_KV_HEADS, 1))], axis=0)[:, None, :]

        P, gates = _project(xc, g_pre_mix[l], mod3, w_main, w_g, T)
        P2 = P.reshape(M, -1)

        hf, hb = _mlstm(P, gates, gate_bias, T, Lc)

        qk = _qk_prep(P, nw, cos_t, sin_t)
        yg = _flash(qk, 0, qk, GQA_HEADS * HEAD_DIM, P, C_GV, G=GQA_KV_HEADS, R=GQA_HEADS // GQA_KV_HEADS,
                    T=T, Lc=Lc)

        bias_tab = _nat_bias_tables(nat_rel_bias[l], T)
        yn = _nat(P, bias_tab, T, Lc, scale)

        merged = _merge(hf.reshape(M, -1), hb.reshape(M, -1), P2, yg.reshape(M, -1), yn.reshape(M, -1),
                        w_branch[l].astype(BF16))
        wr = jnp.pad(w_router[l], ((0, 0), (0, LANES - E)))
        xc, hx, acol, arow = _wout_post(merged.reshape(B, Tt, D), w_out[l].astype(BF16), xc, g_post_mix[l], mod3, 2,
                                        g_pre_ffn[l], 3, 4, *_split_bf16(wr))

        prow, pcol, starts = _route(arow, acol, T, Lc, cap_l, cap_c)
        starts_flat = starts[:, :, :nchunk + 1].reshape(-1)
        prow3 = prow.reshape(B * E, nT, TR)
        arow3 = arow.reshape(B * E, nT, TR)

        cap_ctx = 0 if last else cap_c
        capb = cap_l + cap_ctx
        xs, gs = _gather(starts_flat, prow3, arow3, hx, nchunk=nchunk, cap_l=cap_l, cap_c=cap_ctx)
        hid = _expert_matmul(_ffn1_kernel, xs, (w_expert_gate, w_expert_up), l)
        ye = _expert_matmul(_ffn2_kernel, hid, (w_expert_down,), l, row_scale=gs)

        if last:
            return _combine(starts_flat, pcol, ye, xc, g_post_ffn[l], mod3, 5, tile0=0, ntile=nchunk,
                            nchunk=nchunk, cap=cap_l, stride=capb, off=0, mod_row_ctx=False,
                            out=jax.ShapeDtypeStruct((B, T, D), F32))
        xc = _combine(starts_flat, pcol, ye, xc, g_post_ffn[l], mod3, 5, tile0=0, ntile=nchunk,
                      nchunk=nchunk, cap=cap_l, stride=capb, off=0, mod_row_ctx=False, out=None)
        ctx_starts = jnp.tile(jnp.asarray([0, cap_c], jnp.int32), B * E)
        xc = _combine(ctx_starts, pcol, ye, xc, g_post_ffn[l], mod3, 5, tile0=nT - 1, ntile=1,
                      nchunk=1, cap=cap_c, stride=capb, off=cap_l, mod_row_ctx=True, out=None)
```
